```python
import jax, jax.numpy as jnp
from jax import lax
import numpy as np

D_MODEL = 2048
BATCH = 32
SEQ = 256
DEPTH = 2
DEC_BATCH = 2
DEC_SEQ = 2048
PAST_LEN = 256

GRID_W = 64
NA_HEADS = 6
NA_HEAD_DIM = 128
NA_ROWS = 8
NA_COLS = 16
MLA_HEADS = 6
Q_LORA = 512
KV_LORA = 512
NOPE_DIM = 128
ROPE_DIM = 64
V_DIM = 128
ROPE_BASE = 10000.0
MLA_SCALE = (NOPE_DIM + ROPE_DIM) ** -0.5
SGU_GROUPS = 4
SGU_DIM = 128
CHUNK = 128
NA_W = NA_HEADS * NA_HEAD_DIM
MLA_W = MLA_HEADS * V_DIM
SGU_W = SGU_GROUPS * SGU_DIM
D_MIX = NA_W + MLA_W + SGU_W
IN_DIM = 3 * NA_W + Q_LORA + KV_LORA + ROPE_DIM + 2 * SGU_W
PROJ_SPLITS = (3 * NA_W, 3 * NA_W + Q_LORA, 3 * NA_W + Q_LORA + KV_LORA, 3 * NA_W + Q_LORA + KV_LORA + ROPE_DIM)
N_EXPERTS = 16
N_GROUPS = 4
EXPERTS_PER_GROUP = N_EXPERTS // N_GROUPS
TOP_K = 2
D_EXPERT = 512
ALPHA = (2 * DEPTH) ** 0.25
BETA = (8 * DEPTH) ** -0.25
LN_EPS = 1e-6
RMS_EPS = 1e-6
Q_BLOCK = 128

kernel_name = 'hybrid_diffusion_na_mla_sgu_moe_step'


def layer_norm(x, g=None, b=None):
    xf = x.astype(jnp.float32)
    mu = xf.mean(-1, keepdims=True)
    var = jnp.square(xf - mu).mean(-1, keepdims=True)
    y = ((xf - mu) * lax.rsqrt(var + LN_EPS)).astype(x.dtype)
    if g is not None:
        y = y * g + b
    return y


def rms_norm(x, g):
    xf = x.astype(jnp.float32)
    y = xf * lax.rsqrt(jnp.mean(xf * xf, -1, keepdims=True) + RMS_EPS)
    return y.astype(x.dtype) * g


def rope_2d(x):
    n = x.shape[1]
    t = jnp.arange(n)
    half = ROPE_DIM // 2
    nf = half // 2
    inv = ROPE_BASE ** (-jnp.arange(nf, dtype=jnp.float32) * 2.0 / half)

    def rot(xp, pos):
        ang = pos.astype(jnp.float32)[:, None] * inv[None, :]
        cos = jnp.cos(ang)[:, None, :].astype(x.dtype)
        sin = jnp.sin(ang)[:, None, :].astype(x.dtype)
        x1, x2 = jnp.split(xp, 2, axis=-1)
        return jnp.concatenate([x1 * cos - x2 * sin, x1 * sin + x2 * cos], -1)

    return jnp.concatenate([rot(x[..., :half], t // GRID_W), rot(x[..., half:], t % GRID_W)], -1)


def softmax_attention(q, k, v, scale):
    B, Nq, H, Dk = q.shape
    nb = Nq // Q_BLOCK
    qb = jnp.moveaxis(q.reshape(B, nb, Q_BLOCK, H, Dk), 1, 0)

    def block(qi):
        s = jnp.einsum('bqhd,bkhd->bhqk', qi, k).astype(jnp.float32) * scale
        p = jax.nn.softmax(s, axis=-1).astype(v.dtype)
        return jnp.einsum('bhqk,bkhd->bqhd', p, v)

    o = lax.map(block, qb)
    return jnp.moveaxis(o, 0, 1).reshape(B, Nq, H * v.shape[-1])


def na_latent(q, k, v, ck, cv, rpb):
    B, N, H, Dh = q.shape
    R = N // GRID_W
    kr = min(NA_ROWS, R)
    rows = jnp.arange(R)
    row_start = jnp.clip(rows - kr // 2, 0, R - kr)
    row_idx = row_start[:, None] + jnp.arange(kr)[None, :]
    cols = jnp.arange(GRID_W)
    col_start = jnp.clip(cols - NA_COLS // 2, 0, GRID_W - NA_COLS)
    col_ok = (cols[None, :] >= col_start[:, None]) & (cols[None, :] < col_start[:, None] + NA_COLS)
    d_row = row_idx - rows[:, None] + NA_ROWS - 1
    d_col = jnp.clip(cols[None, :] - cols[:, None] + NA_COLS - 1, 0, 2 * NA_COLS - 2)
    bias = rpb[:, d_row[:, None, :, None], d_col[None, :, None, :]]
    qg = q.reshape(B, R, GRID_W, H, Dh)
    kg = k.reshape(B, R, GRID_W, H, Dh)[:, row_idx]
    vg = v.reshape(B, R, GRID_W, H, Dh)[:, row_idx]
    scale = Dh ** -0.5
    s_loc = jnp.einsum('brqhd,brkwhd->bhrqkw', qg, kg).astype(jnp.float32) * scale + bias.astype(jnp.float32)
    s_loc = jnp.where(col_ok[:, None, :], s_loc, -jnp.inf).reshape(B, H, R, GRID_W, kr * GRID_W)
    s_ctx = jnp.einsum('brqhd,blhd->bhrql', qg, ck).astype(jnp.float32) * scale
    p = jax.nn.softmax(jnp.concatenate([s_loc, s_ctx], -1), axis=-1).astype(v.dtype)
    p_loc = p[..., :kr * GRID_W].reshape(B, H, R, GRID_W, kr, GRID_W)
    p_ctx = p[..., kr * GRID_W:]
    out = jnp.einsum('bhrqkw,brkwhd->brqhd', p_loc, vg) + jnp.einsum('bhrql,blhd->brqhd', p_ctx, cv)
    return out.reshape(B, N, H * Dh)


def sgu_chunks(z, ln_g, ln_b, w_s, b_s):
    B, N, _ = z.shape
    u, v = jnp.split(jax.nn.gelu(z, approximate=False), 2, axis=-1)
    u = u.reshape(B, N // CHUNK, CHUNK, SGU_GROUPS, SGU_DIM)
    v = layer_norm(v.reshape(B, N // CHUNK, CHUNK, SGU_GROUPS, SGU_DIM), ln_g, ln_b)
    v = jnp.einsum('gpq,bnqgc->bnpgc', w_s, v) + b_s.T[:, :, None]
    return (u * v).reshape(B, N, SGU_W)


def mla_query(cq, g_q, w_uq):
    B, N, _ = cq.shape
    return (rms_norm(cq, g_q) @ w_uq).reshape(B, N, MLA_HEADS, NOPE_DIM + ROPE_DIM)


def mla_keys_values(ckv, krope, w_ukv):
    B, L, _ = ckv.shape
    kv = (ckv @ w_ukv).reshape(B, L, MLA_HEADS, NOPE_DIM + V_DIM)
    k_nope, v = kv[..., :NOPE_DIM], kv[..., NOPE_DIM:]
    k = jnp.concatenate([k_nope, jnp.broadcast_to(krope[:, :, None, :], (B, L, MLA_HEADS, ROPE_DIM))], -1)
    return k, v


def modulation(cond, w_ada, b_ada):
    mods = jax.nn.silu(cond) @ w_ada + b_ada
    return [m[:, None, :] for m in jnp.split(mods, 6, axis=-1)]


def mixer_inputs(x, shift, scale, w_in):
    h = layer_norm(x) * (1 + scale) + shift
    return jnp.split(h @ w_in, PROJ_SPLITS, axis=-1)


def moe(h, w_router, b_router, w_gate, w_up, w_down):
    B, N, D = h.shape
    t = h.reshape(B * N, D)
    s = jax.nn.sigmoid((t @ w_router).astype(jnp.float32))
    sb = s + b_router.astype(jnp.float32)
    grp_score = lax.top_k(sb.reshape(-1, N_GROUPS, EXPERTS_PER_GROUP), TOP_K)[0].sum(-1)
    gsel = jnp.argmax(grp_score, axis=-1)
    in_grp = (jnp.arange(N_EXPERTS) // EXPERTS_PER_GROUP)[None, :] == gsel[:, None]
    _, idx = lax.top_k(jnp.where(in_grp, sb, -jnp.inf), TOP_K)
    w = jnp.take_along_axis(s, idx, axis=-1)
    w = w / w.sum(-1, keepdims=True)
    gates = (jax.nn.one_hot(idx, N_EXPERTS, dtype=jnp.float32) * w[..., None]).sum(1).astype(h.dtype)
    out = jnp.zeros_like(t)
    for e in range(N_EXPERTS):
        y = (jax.nn.silu(t @ w_gate[e]) * (t @ w_up[e])) @ w_down[e]
        out = out + gates[:, e:e + 1] * y
    return out.reshape(B, N, D)


def residual_tail(x, mixed, g1, sh2, sc2, g2, lw, w_router, b_router):
    x = layer_norm(ALPHA * x + g1 * mixed, lw['ln1_g'], lw['ln1_b'])
    h = layer_norm(x) * (1 + sc2) + sh2
    y = moe(h, w_router, b_router, lw['w_gate'], lw['w_up'], lw['w_down'])
    return layer_norm(ALPHA * x + g2 * y, lw['ln2_g'], lw['ln2_b'])


def context_layer(x, cond, lw, w_router, b_router):
    sh1, sc1, g1, sh2, sc2, g2 = modulation(cond, lw['w_ada'], lw['b_ada'])
    qkv, cq, ckv, krope, z = mixer_inputs(x, sh1, sc1, lw['w_in'])
    B, N, _ = x.shape
    q, k, v = [a.reshape(B, N, NA_HEADS, NA_HEAD_DIM) for a in jnp.split(qkv, 3, axis=-1)]
    out_a = softmax_attention(q, k, v, NA_HEAD_DIM ** -0.5)
    ckv = rms_norm(ckv, lw['g_kv'])
    mq = mla_query(cq, lw['g_q'], lw['w_uq'])
    mk, mv = mla_keys_values(ckv, krope, lw['w_ukv'])
    out_b = softmax_attention(mq, mk, mv, MLA_SCALE)
    out_c = sgu_chunks(z, lw['sgu_ln_g'], lw['sgu_ln_b'], lw['sgu_w'], lw['sgu_b'])
    mixed = jnp.concatenate([out_a, out_b, out_c], -1) @ lw['w_out']
    x = residual_tail(x, mixed, g1, sh2, sc2, g2, lw, w_router, b_router)
    return x, k, v, ckv, krope


def latent_layer(x, cond, ck, cv, cckv, ckrope, lw, w_router, b_router):
    sh1, sc1, g1, sh2, sc2, g2 = modulation(cond, lw['w_ada'], lw['b_ada'])
    qkv, cq, ckv, krope, z = mixer_inputs(x, sh1, sc1, lw['w_in'])
    B, N, _ = x.shape
    q, k, v = [a.reshape(B, N, NA_HEADS, NA_HEAD_DIM) for a in jnp.split(qkv, 3, axis=-1)]
    out_a = na_latent(q, k, v, ck, cv, lw['na_rpb'])
    ckv = rms_norm(ckv, lw['g_kv'])
    mq = mla_query(cq, lw['g_q'], lw['w_uq'])
    mq = jnp.concatenate([mq[..., :NOPE_DIM], rope_2d(mq[..., NOPE_DIM:])], -1)
    krope = rope_2d(krope[:, :, None, :])[:, :, 0, :]
    mk_lat, mv_lat = mla_keys_values(ckv, krope, lw['w_ukv'])
    mk_ctx, mv_ctx = mla_keys_values(cckv, ckrope, lw['w_ukv'])
    mk = jnp.concatenate([mk_lat, mk_ctx], axis=1)
    mv = jnp.concatenate([mv_lat, mv_ctx], axis=1)
    out_b = softmax_attention(mq, mk, mv, MLA_SCALE)
    out_c = sgu_chunks(z, lw['sgu_ln_g'], lw['sgu_ln_b'], lw['sgu_w'], lw['sgu_b'])
    mixed = jnp.concatenate([out_a, out_b, out_c], -1) @ lw['w_out']
    return residual_tail(x, mixed, g1, sh2, sc2, g2, lw, w_router, b_router)


def setup_inputs(seed: int = 0) -> dict:
    key = jax.random.key(seed)
    ks = jax.random.split(key, 32)

    def nrm(k, shape, scale):
        return jax.random.normal(k, shape, jnp.float32) * scale

    return {
        'x_prompt': nrm(ks[0], (BATCH, SEQ, D_MODEL), 1.0),
        'x_sample': nrm(ks[1], (DEC_BATCH, DEC_SEQ, D_MODEL), 1.0),
        'cache_na_k': nrm(ks[2], (DEC_BATCH, DEPTH, PAST_LEN, NA_HEADS, NA_HEAD_DIM), 1.0),
        'cache_na_v': nrm(ks[3], (DEC_BATCH, DEPTH, PAST_LEN, NA_HEADS, NA_HEAD_DIM), 1.0),
        'cache_mla_ckv': nrm(ks[4], (DEC_BATCH, DEPTH, PAST_LEN, KV_LORA), 1.0),
        'cache_mla_krope': nrm(ks[5], (DEC_BATCH, DEPTH, PAST_LEN, ROPE_DIM), 1.0),
        'c': nrm(ks[6], (DEC_BATCH, D_MODEL), 1.0),
        'c_ctx': nrm(ks[7], (D_MODEL,), 1.0),
        'w_ada': nrm(ks[8], (DEPTH, D_MODEL, 6 * D_MODEL), 0.5 * D_MODEL ** -0.5),
        'b_ada': nrm(ks[9], (DEPTH, 6 * D_MODEL), 0.02),
        'w_in': nrm(ks[10], (DEPTH, D_MODEL, IN_DIM), D_MODEL ** -0.5),
        'g_q': 1.0 + nrm(ks[11], (DEPTH, Q_LORA), 0.02),
        'g_kv': 1.0 + nrm(ks[12], (DEPTH, KV_LORA), 0.02),
        'w_uq': nrm(ks[13], (DEPTH, Q_LORA, MLA_HEADS * (NOPE_DIM + ROPE_DIM)), Q_LORA ** -0.5),
        'w_ukv': nrm(ks[14], (DEPTH, KV_LORA, MLA_HEADS * (NOPE_DIM + V_DIM)), KV_LORA ** -0.5),
        'na_rpb': nrm(ks[15], (DEPTH, NA_HEADS, 2 * NA_ROWS - 1, 2 * NA_COLS - 1), 0.1),
        'sgu_ln_g': 1.0 + nrm(ks[16], (DEPTH, SGU_GROUPS, SGU_DIM), 0.02),
        'sgu_ln_b': nrm(ks[17], (DEPTH, SGU_GROUPS, SGU_DIM), 0.02),
        'sgu_w': nrm(ks[18], (DEPTH, SGU_GROUPS, CHUNK, CHUNK), CHUNK ** -0.5),
        'sgu_b': 1.0 + nrm(ks[19], (DEPTH, SGU_GROUPS, CHUNK), 0.02),
        'w_out': nrm(ks[20], (DEPTH, D_MIX, D_MODEL), BETA * D_MIX ** -0.5),
        'ln1_g': 1.0 + nrm(ks[21], (DEPTH, D_MODEL), 0.02),
        'ln1_b': nrm(ks[22], (DEPTH, D_MODEL), 0.02),
        'ln2_g': 1.0 + nrm(ks[23], (DEPTH, D_MODEL), 0.02),
        'ln2_b': nrm(ks[24], (DEPTH, D_MODEL), 0.02),
        'w_router': nrm(ks[25], (D_MODEL, N_EXPERTS), D_MODEL ** -0.5),
        'b_router': nrm(ks[26], (N_EXPERTS,), 0.01),
        'w_gate': nrm(ks[27], (DEPTH, N_EXPERTS, D_MODEL, D_EXPERT), D_MODEL ** -0.5),
        'w_up': nrm(ks[28], (DEPTH, N_EXPERTS, D_MODEL, D_EXPERT), D_MODEL ** -0.5),
        'w_down': nrm(ks[29], (DEPTH, N_EXPERTS, D_EXPERT, D_MODEL), BETA * D_EXPERT ** -0.5),
    }


def reference(x_prompt, x_sample, cache_na_k, cache_na_v, cache_mla_ckv, cache_mla_krope, c, c_ctx,
              w_ada, b_ada, w_in, g_q, g_kv, w_uq, w_ukv, na_rpb, sgu_ln_g, sgu_ln_b, sgu_w, sgu_b,
              w_out, ln1_g, ln1_b, ln2_g, ln2_b, w_router, b_router, w_gate, w_up, w_down):
    def layer_weights(l):
        return {'w_ada': w_ada[l], 'b_ada': b_ada[l], 'w_in': w_in[l], 'g_q': g_q[l], 'g_kv': g_kv[l],
                'w_uq': w_uq[l], 'w_ukv': w_ukv[l], 'na_rpb': na_rpb[l], 'sgu_ln_g': sgu_ln_g[l],
                'sgu_ln_b': sgu_ln_b[l], 'sgu_w': sgu_w[l], 'sgu_b': sgu_b[l], 'w_out': w_out[l],
                'ln1_g': ln1_g[l], 'ln1_b': ln1_b[l], 'ln2_g': ln2_g[l], 'ln2_b': ln2_b[l],
                'w_gate': w_gate[l], 'w_up': w_up[l], 'w_down': w_down[l]}

    cond_ctx = c_ctx[None, :]
    xp = x_prompt
    ks_, vs_, ckvs_, kropes_ = [], [], [], []
    for l in range(DEPTH):
        xp, k, v, ckv, krope = context_layer(xp, cond_ctx, layer_weights(l), w_router, b_router)
        ks_.append(k)
        vs_.append(v)
        ckvs_.append(ckv)
        kropes_.append(krope)

    xs = x_sample
    for l in range(DEPTH):
        xs = latent_layer(xs, c, cache_na_k[:, l], cache_na_v[:, l], cache_mla_ckv[:, l], cache_mla_krope[:, l],
                          layer_weights(l), w_router, b_router)

    new_na_k = jnp.stack(ks_, axis=1)
    new_na_v = jnp.stack(vs_, axis=1)
    new_mla_ckv = jnp.stack(ckvs_, axis=1)
    new_mla_krope = jnp.stack(kropes_, axis=1)
    return (xp, xs, new_na_k, new_na_v, new_mla_ckv, new_mla_krope)
```

```python
import functools

import numpy as np
import jax
import jax.numpy as jnp
from jax import lax
from jax.experimental import pallas as pl
from jax.experimental.pallas import tpu as pltpu

F32 = jnp.float32
BF16 = jnp.bfloat16

D_MODEL = 2048
BATCH = 32
SEQ = 256
DEPTH = 2
DEC_BATCH = 2
DEC_SEQ = 2048
PAST_LEN = 256
GRID_W = 64
GRID_R = DEC_SEQ // GRID_W
NA_HEADS = 6
NA_HEAD_DIM = 128
NA_ROWS = 8
NA_COLS = 16
MLA_HEADS = 6
Q_LORA = 512
KV_LORA = 512
NOPE_DIM = 128
ROPE_DIM = 64
V_DIM = 128
ROPE_BASE = 10000.0
MLA_SCALE = (NOPE_DIM + ROPE_DIM) ** -0.5
NA_SCALE = NA_HEAD_DIM ** -0.5
SGU_GROUPS = 4
SGU_DIM = 128
CHUNK = 128
NA_W = NA_HEADS * NA_HEAD_DIM
MLA_W = MLA_HEADS * V_DIM
SGU_W = SGU_GROUPS * SGU_DIM
D_MIX = NA_W + MLA_W + SGU_W
N_EXPERTS = 16
N_GROUPS = 4
EXPERTS_PER_GROUP = N_EXPERTS // N_GROUPS
TOP_K = 2
D_EXPERT = 512
ALPHA = (2 * DEPTH) ** 0.25
LN_EPS = 1e-6
RMS_EPS = 1e-6

T_CTX = BATCH * SEQ
T_LAT = DEC_BATCH * DEC_SEQ
T_ALL = T_CTX + T_LAT
N_COND = 8

C_Q, C_K, C_V = 0, NA_W, 2 * NA_W
C_CQ = 3 * NA_W
C_CKV = C_CQ + Q_LORA
C_KR = C_CKV + KV_LORA
C_Z = C_KR + 2 * ROPE_DIM
IN_PAD = C_Z + 2 * SGU_W
MLA_QK = 256

TM_PROJ = 256
TM_OUT = 512
TM_SGU = 256
TM_TAIL = 512
TM_EXP = 256
TQ_MLA = 256
NEG_BIG = -1e30
VMEM_LIMIT = 56 * 1024 * 1024


def _cparams(sem, vmem=VMEM_LIMIT):
    return pltpu.CompilerParams(dimension_semantics=sem, vmem_limit_bytes=vmem)


def _cond_row(i, tm):
    t0 = i * tm
    return jnp.where(t0 < T_CTX, 0, 1 + (t0 - T_CTX) // DEC_SEQ)


def _ln(x):
    mu = jnp.mean(x, axis=-1, keepdims=True)
    xc = x - mu
    var = jnp.mean(xc * xc, axis=-1, keepdims=True)
    return xc * lax.rsqrt(var + LN_EPS)


def _rms(x):
    return x * lax.rsqrt(jnp.mean(x * x, axis=-1, keepdims=True) + RMS_EPS)


def _dot(a, b):
    return jnp.dot(a, b, preferred_element_type=F32)


def _dot_nt(a, b):
    return lax.dot_general(a, b, (((1,), (1,)), ((), ())), preferred_element_type=F32)


def _ada_kernel(c_ref, w_ref, b_ref, o_ref):
    c = c_ref[...]
    s = c * jax.nn.sigmoid(c)
    o_ref[...] = _dot(s.astype(BF16), w_ref[...].astype(BF16)) + b_ref[...]


def _ada(cond, w_ada, b_ada):
    tn = 1024
    n = 6 * D_MODEL
    return pl.pallas_call(
        _ada_kernel,
        grid=(DEPTH, n // tn),
        in_specs=[
            pl.BlockSpec((N_COND, D_MODEL), lambda l, j: (0, 0)),
            pl.BlockSpec((None, D_MODEL, tn), lambda l, j: (l, 0, j)),
            pl.BlockSpec((None, 1, tn), lambda l, j: (l, 0, j)),
        ],
        out_specs=pl.BlockSpec((None, N_COND, tn), lambda l, j: (l, 0, j)),
        out_shape=jax.ShapeDtypeStruct((DEPTH, N_COND, n), F32),
        compiler_params=_cparams(("parallel", "parallel")),
        name="ada_mod",
    )(cond, w_ada, b_ada.reshape(DEPTH, 1, n))


def _inproj_kernel(x_ref, sh_ref, sc_ref, w_ref, gq_ref, gkv_ref, wuq_ref, wukv_ref, cs_ref,
                   q_ref, k_ref, v_ref, k32_ref, v32_ref, qm_ref, km_ref, vv_ref,
                   ckv32_ref, kr32_ref, z_ref):
    i = pl.program_id(0)
    is_ctx = i * TM_PROJ < T_CTX
    h = (_ln(x_ref[...]) * (1.0 + sc_ref[...]) + sh_ref[...]).astype(BF16)

    def proj(a, b):
        return _dot(h, w_ref[:, a:b])

    q_ref[...] = proj(C_Q, C_K).astype(BF16)
    k = proj(C_K, C_V)
    k_ref[...] = k.astype(BF16)
    v = proj(C_V, C_CQ)
    v_ref[...] = v.astype(BF16)

    cs = cs_ref[...]
    lane = lax.broadcasted_iota(jnp.int32, cs.shape, 1)

    def rotate(pair):
        t = pair * cs
        return jnp.where(lane < ROPE_DIM, t + pltpu.roll(t, ROPE_DIM, 1), 0.0)

    cqn = (_rms(proj(C_CQ, C_CKV)) * gq_ref[...]).astype(BF16)
    mq = _dot(cqn, wuq_ref[...])
    for hd in range(MLA_HEADS):
        c0 = hd * MLA_QK
        qm_ref[:, c0:c0 + NOPE_DIM] = mq[:, c0:c0 + NOPE_DIM].astype(BF16)
        qm_ref[:, c0 + NOPE_DIM:c0 + MLA_QK] = rotate(mq[:, c0 + NOPE_DIM:c0 + MLA_QK]).astype(BF16)

    ckvn = _rms(proj(C_CKV, C_KR)) * gkv_ref[...]
    kv = _dot(ckvn.astype(BF16), wukv_ref[...])
    kr2 = proj(C_KR, C_Z)
    krot = rotate(kr2).astype(BF16)
    for hd in range(MLA_HEADS):
        c0 = hd * MLA_QK
        km_ref[:, c0:c0 + NOPE_DIM] = kv[:, hd * NOPE_DIM:(hd + 1) * NOPE_DIM].astype(BF16)
        km_ref[:, c0 + NOPE_DIM:c0 + MLA_QK] = krot
    vv_ref[...] = kv[:, MLA_HEADS * NOPE_DIM:].astype(BF16)

    z_ref[...] = proj(C_Z, IN_PAD)

    @pl.when(is_ctx)
    def _():
        k32_ref[...] = k
        v32_ref[...] = v
        ckv32_ref[...] = ckvn
        kr32_ref[...] = kr2[:, :ROPE_DIM]


def _inproj(l, x, mods, w_in_p, g_q, g_kv, w_uq_p, w_ukv_p, cs_tab):
    tm = TM_PROJ
    nct = T_CTX // tm
    per_b = DEC_SEQ // tm

    def row(i):
        return (i, 0)

    def ctx_row(i):
        return (jnp.minimum(i, nct - 1), 0)

    def cs_row(i):
        return (jnp.where(i < nct, 0, 1 + (i - nct) % per_b), 0)

    def mod(kind):
        return pl.BlockSpec((None, None, None, 1, D_MODEL), lambda i: (l, _cond_row(i, tm), kind, 0, 0))

    def const2(shape):
        return pl.BlockSpec(shape, lambda i: (l, 0, 0), pipeline_mode=pl.Buffered(1))

    out_shapes = [
        jax.ShapeDtypeStruct((T_ALL, NA_W), BF16),
        jax.ShapeDtypeStruct((T_ALL, NA_W), BF16),
        jax.ShapeDtypeStruct((T_ALL, NA_W), BF16),
        jax.ShapeDtypeStruct((T_CTX, NA_W), F32),
        jax.ShapeDtypeStruct((T_CTX, NA_W), F32),
        jax.ShapeDtypeStruct((T_ALL, MLA_HEADS * MLA_QK), BF16),
        jax.ShapeDtypeStruct((T_ALL, MLA_HEADS * MLA_QK), BF16),
        jax.ShapeDtypeStruct((T_ALL, MLA_W), BF16),
        jax.ShapeDtypeStruct((T_CTX, KV_LORA), F32),
        jax.ShapeDtypeStruct((T_CTX, ROPE_DIM), F32),
        jax.ShapeDtypeStruct((T_ALL, 2 * SGU_W), F32),
    ]
    out_specs = [
        pl.BlockSpec((tm, NA_W), row), pl.BlockSpec((tm, NA_W), row), pl.BlockSpec((tm, NA_W), row),
        pl.BlockSpec((tm, NA_W), ctx_row), pl.BlockSpec((tm, NA_W), ctx_row),
        pl.BlockSpec((tm, MLA_HEADS * MLA_QK), row), pl.BlockSpec((tm, MLA_HEADS * MLA_QK), row),
        pl.BlockSpec((tm, MLA_W), row),
        pl.BlockSpec((tm, KV_LORA), ctx_row), pl.BlockSpec((tm, ROPE_DIM), ctx_row),
        pl.BlockSpec((tm, 2 * SGU_W), row),
    ]
    return pl.pallas_call(
        _inproj_kernel,
        grid=(T_ALL // tm,),
        in_specs=[
            pl.BlockSpec((tm, D_MODEL), row),
            mod(0), mod(1),
            const2((None, D_MODEL, IN_PAD)),
            pl.BlockSpec((None, 1, Q_LORA), lambda i: (l, 0, 0)),
            pl.BlockSpec((None, 1, KV_LORA), lambda i: (l, 0, 0)),
            const2((None, Q_LORA, MLA_HEADS * MLA_QK)),
            const2((None, KV_LORA, 2 * MLA_W)),
            pl.BlockSpec((tm, 2 * ROPE_DIM), cs_row),
        ],
        out_specs=out_specs,
        out_shape=out_shapes,
        compiler_params=_cparams(("arbitrary",)),
        name="in_proj",
    )(x, mods, mods, w_in_p, g_q.reshape(DEPTH, 1, Q_LORA), g_kv.reshape(DEPTH, 1, KV_LORA),
      w_uq_p, w_ukv_p, cs_tab)


def _ctxkv_kernel(ckv_ref, kr_ref, w_ref, km_ref, vv_ref):
    kv = _dot(ckv_ref[...].astype(BF16), w_ref[...])
    kr = jnp.concatenate([kr_ref[...], jnp.zeros((PAST_LEN, ROPE_DIM), F32)], axis=-1).astype(BF16)
    for hd in range(MLA_HEADS):
        c0 = hd * MLA_QK
        km_ref[:, c0:c0 + NOPE_DIM] = kv[:, hd * NOPE_DIM:(hd + 1) * NOPE_DIM].astype(BF16)
        km_ref[:, c0 + NOPE_DIM:c0 + MLA_QK] = kr
    vv_ref[...] = kv[:, MLA_HEADS * NOPE_DIM:].astype(BF16)


def _ctxkv(cache_ckv, cache_krope, w_ukv_p):
    return pl.pallas_call(
        _ctxkv_kernel,
        grid=(DEPTH, DEC_BATCH),
        in_specs=[
            pl.BlockSpec((None, None, PAST_LEN, KV_LORA), lambda l, b: (b, l, 0, 0)),
            pl.BlockSpec((None, None, PAST_LEN, ROPE_DIM), lambda l, b: (b, l, 0, 0)),
            pl.BlockSpec((None, KV_LORA, 2 * MLA_W), lambda l, b: (l, 0, 0)),
        ],
        out_specs=[
            pl.BlockSpec((None, None, PAST_LEN, MLA_HEADS * MLA_QK), lambda l, b: (l, b, 0, 0)),
            pl.BlockSpec((None, None, PAST_LEN, MLA_W), lambda l, b: (l, b, 0, 0)),
        ],
        out_shape=[
            jax.ShapeDtypeStruct((DEPTH, DEC_BATCH, PAST_LEN, MLA_HEADS * MLA_QK), BF16),
            jax.ShapeDtypeStruct((DEPTH, DEC_BATCH, PAST_LEN, MLA_W), BF16),
        ],
        compiler_params=_cparams(("parallel", "parallel")),
        name="ctx_cache_kv",
    )(cache_ckv, cache_krope, w_ukv_p)


def _softmax_pv(blocks):
    ss = [s for s, _ in blocks]
    m = ss[0].max(axis=-1, keepdims=True)
    for s in ss[1:]:
        m = jnp.maximum(m, s.max(axis=-1, keepdims=True))
    num = None
    den = None
    for s, (_, v) in zip(ss, blocks):
        e = jnp.exp(s - m)
        d = e.sum(axis=-1, keepdims=True)
        o = _dot(e.astype(BF16), v)
        num = o if num is None else num + o
        den = d if den is None else den + d
    return num / den


def _ctx_attn_kernel(q_ref, k_ref, v_ref, qm_ref, km_ref, vv_ref, oa_ref, ob_ref):
    for hd in range(NA_HEADS):
        sl = slice(hd * NA_HEAD_DIM, (hd + 1) * NA_HEAD_DIM)
        s = _dot_nt(q_ref[:, sl], k_ref[:, sl]) * NA_SCALE
        oa_ref[:, sl] = _softmax_pv([(s, v_ref[:, sl])]).astype(BF16)
    for hd in range(MLA_HEADS):
        sq = slice(hd * MLA_QK, (hd + 1) * MLA_QK)
        sv = slice(hd * V_DIM, (hd + 1) * V_DIM)
        s = _dot_nt(qm_ref[:, sq], km_ref[:, sq]) * MLA_SCALE
        ob_ref[:, sv] = _softmax_pv([(s, vv_ref[:, sv])]).astype(BF16)


def _ctx_attn(q, k, v, qm, km, vv):
    def spec(w):
        return pl.BlockSpec((SEQ, w), lambda b: (b, 0))

    return pl.pallas_call(
        _ctx_attn_kernel,
        grid=(BATCH,),
        in_specs=[spec(NA_W), spec(NA_W), spec(NA_W),
                  spec(MLA_HEADS * MLA_QK), spec(MLA_HEADS * MLA_QK), spec(MLA_W)],
        out_specs=[spec(NA_W), spec(MLA_W)],
        out_shape=[jax.ShapeDtypeStruct((T_ALL, NA_W), BF16), jax.ShapeDtypeStruct((T_ALL, MLA_W), BF16)],
        compiler_params=_cparams(("parallel",)),
        name="ctx_attn",
    )(q, k, v, qm, km, vv)


def _lat_mla_kernel(q_ref, k_ref, v_ref, kc_ref, vc_ref, prev_ref, o_ref):
    del prev_ref
    q = q_ref[...]
    s1 = _dot_nt(q, k_ref[...]) * MLA_SCALE
    s2 = _dot_nt(q, kc_ref[...]) * MLA_SCALE
    o_ref[...] = _softmax_pv([(s1, v_ref[...]), (s2, vc_ref[...])]).astype(BF16)


def _lat_mla(l, qm, km, vv, kmc, vvc, ob):
    tq = TQ_MLA
    nq = DEC_SEQ // tq
    row0 = T_CTX // tq
    kb0 = T_CTX // DEC_SEQ
    return pl.pallas_call(
        _lat_mla_kernel,
        grid=(DEC_BATCH, MLA_HEADS, nq),
        in_specs=[
            pl.BlockSpec((tq, MLA_QK), lambda b, h, t: (row0 + b * nq + t, h)),
            pl.BlockSpec((DEC_SEQ, MLA_QK), lambda b, h, t: (kb0 + b, h)),
            pl.BlockSpec((DEC_SEQ, V_DIM), lambda b, h, t: (kb0 + b, h)),
            pl.BlockSpec((None, None, PAST_LEN, MLA_QK), lambda b, h, t: (l, b, 0, h)),
            pl.BlockSpec((None, None, PAST_LEN, V_DIM), lambda b, h, t: (l, b, 0, h)),
            pl.BlockSpec(memory_space=pl.ANY),
        ],
        out_specs=pl.BlockSpec((tq, V_DIM), lambda b, h, t: (row0 + b * nq + t, h)),
        out_shape=jax.ShapeDtypeStruct((T_ALL, MLA_W), BF16),
        input_output_aliases={5: 0},
        compiler_params=_cparams(("parallel", "parallel", "arbitrary")),
        name="lat_mla",
    )(qm, km, vv, kmc, vvc, ob)


def _lat_na_kernel(q_ref, k_ref, v_ref, ck_ref, cv_ref, bias_ref, prev_ref, o_ref):
    del prev_ref
    ck = ck_ref[...].astype(BF16)
    cv = cv_ref[...].astype(BF16)
    win = NA_ROWS * GRID_W

    def body(r, carry):
        rs = jnp.clip(r - NA_ROWS // 2, 0, GRID_R - NA_ROWS)
        tb = jnp.where(r < NA_ROWS // 2, r, jnp.where(r > GRID_R - NA_ROWS // 2, r - (GRID_R - NA_ROWS), NA_ROWS // 2))
        q0 = pl.multiple_of(r * GRID_W, GRID_W)
        k0 = pl.multiple_of(rs * GRID_W, GRID_W)
        q = q_ref[pl.ds(q0, GRID_W), :]
        kw = k_ref[pl.ds(k0, win), :]
        vw = v_ref[pl.ds(k0, win), :]
        s1 = _dot_nt(q, kw) * NA_SCALE + bias_ref[tb]
        s2 = _dot_nt(q, ck) * NA_SCALE
        o_ref[pl.ds(q0, GRID_W), :] = _softmax_pv([(s1, vw), (s2, cv)]).astype(BF16)
        return carry

    lax.fori_loop(0, GRID_R, body, 0)


def _lat_na(l, q, k, v, cache_k, cache_v, bias_tab, oa):
    kb0 = T_CTX // DEC_SEQ

    def lat(b, h):
        return (kb0 + b, h)

    return pl.pallas_call(
        _lat_na_kernel,
        grid=(DEC_BATCH, NA_HEADS),
        in_specs=[
            pl.BlockSpec((DEC_SEQ, NA_HEAD_DIM), lat),
            pl.BlockSpec((DEC_SEQ, NA_HEAD_DIM), lat),
            pl.BlockSpec((DEC_SEQ, NA_HEAD_DIM), lat),
            pl.BlockSpec((None, None, PAST_LEN, NA_HEAD_DIM), lambda b, h: (b, l, 0, h)),
            pl.BlockSpec((None, None, PAST_LEN, NA_HEAD_DIM), lambda b, h: (b, l, 0, h)),
            pl.BlockSpec((None, None, NA_ROWS, GRID_W, NA_ROWS * GRID_W), lambda b, h: (l, h, 0, 0, 0)),
            pl.BlockSpec(memory_space=pl.ANY),
        ],
        out_specs=pl.BlockSpec((DEC_SEQ, NA_HEAD_DIM), lat),
        out_shape=jax.ShapeDtypeStruct((T_ALL, NA_W), BF16),
        input_output_aliases={6: 0},
        compiler_params=_cparams(("parallel", "parallel")),
        name="lat_na",
    )(q, k, v, cache_k, cache_v, bias_tab, oa)


def _sgu_kernel(z_ref, g_ref, b_ref, w_ref, bs_ref, o_ref):
    z = z_ref[...]
    a = 0.5 * z * (1.0 + lax.erf(z * (0.5 ** 0.5)))
    for c in range(TM_SGU // CHUNK):
        rows = slice(c * CHUNK, (c + 1) * CHUNK)
        for g in range(SGU_GROUPS):
            u = a[rows, g * SGU_DIM:(g + 1) * SGU_DIM]
            v = a[rows, SGU_W + g * SGU_DIM:SGU_W + (g + 1) * SGU_DIM]
            vn = _ln(v) * g_ref[g:g + 1, :] + b_ref[g:g + 1, :]
            t = _dot(w_ref[g].astype(BF16), vn.astype(BF16)) + bs_ref[g]
            o_ref[rows, g * SGU_DIM:(g + 1) * SGU_DIM] = (u * t).astype(BF16)


def _sgu(l, z, ln_g, ln_b, w_s, b_s):
    tm = TM_SGU
    return pl.pallas_call(
        _sgu_kernel,
        grid=(T_ALL // tm,),
        in_specs=[
            pl.BlockSpec((tm, 2 * SGU_W), lambda i: (i, 0)),
            pl.BlockSpec((None, SGU_GROUPS, SGU_DIM), lambda i: (l, 0, 0)),
            pl.BlockSpec((None, SGU_GROUPS, SGU_DIM), lambda i: (l, 0, 0)),
            pl.BlockSpec((None, SGU_GROUPS, CHUNK, CHUNK), lambda i: (l, 0, 0, 0)),
            pl.BlockSpec((None, SGU_GROUPS, CHUNK, 1), lambda i: (l, 0, 0, 0)),
        ],
        out_specs=pl.BlockSpec((tm, SGU_W), lambda i: (i, 0)),
        out_shape=jax.ShapeDtypeStruct((T_ALL, SGU_W), BF16),
        compiler_params=_cparams(("parallel",)),
        name="sgu",
    )(z, ln_g, ln_b, w_s, b_s.reshape(DEPTH, SGU_GROUPS, CHUNK, 1))


def _outproj_kernel(a_ref, b_ref, c_ref, wa_ref, wb_ref, wc_ref, x_ref, g1_ref, sh2_ref, sc2_ref,
                    lg_ref, lb_ref, wr_ref, x1_ref, h2_ref, logit_ref):
    mixed = _dot(a_ref[...], wa_ref[...]) + _dot(b_ref[...], wb_ref[...]) + _dot(c_ref[...], wc_ref[...])
    x1 = _ln(ALPHA * x_ref[...] + g1_ref[...] * mixed) * lg_ref[...] + lb_ref[...]
    x1_ref[...] = x1
    h2 = (_ln(x1) * (1.0 + sc2_ref[...]) + sh2_ref[...]).astype(BF16)
    h2_ref[...] = h2
    logit_ref[...] = _dot(h2, wr_ref[...])


def _outproj(l, oa, ob, oc, w_out_b, x, mods, ln_g, ln_b, w_router_b):
    tm = TM_OUT

    def row(i):
        return (i, 0)

    def mod(kind):
        return pl.BlockSpec((None, None, None, 1, D_MODEL), lambda i: (l, _cond_row(i, tm), kind, 0, 0))

    def wblk(rows, blk):
        return pl.BlockSpec((None, rows, D_MODEL), lambda i: (l, blk, 0), pipeline_mode=pl.Buffered(1))

    vec = pl.BlockSpec((None, 1, D_MODEL), lambda i: (l, 0, 0))
    return pl.pallas_call(
        _outproj_kernel,
        grid=(T_ALL // tm,),
        in_specs=[
            pl.BlockSpec((tm, NA_W), row), pl.BlockSpec((tm, MLA_W), row), pl.BlockSpec((tm, SGU_W), row),
            wblk(NA_W, 0), wblk(MLA_W, 1), wblk(SGU_W, (NA_W + MLA_W) // SGU_W),
            pl.BlockSpec((tm, D_MODEL), row),
            mod(2), mod(3), mod(4), vec, vec,
            pl.BlockSpec((D_MODEL, N_EXPERTS), lambda i: (0, 0)),
        ],
        out_specs=[pl.BlockSpec((tm, D_MODEL), row), pl.BlockSpec((tm, D_MODEL), row),
                   pl.BlockSpec((tm, N_EXPERTS), row)],
        out_shape=[jax.ShapeDtypeStruct((T_ALL, D_MODEL), F32), jax.ShapeDtypeStruct((T_ALL, D_MODEL), BF16),
                   jax.ShapeDtypeStruct((T_ALL, N_EXPERTS), F32)],
        compiler_params=_cparams(("parallel",)),
        name="out_proj",
    )(oa, ob, oc, w_out_b, w_out_b, w_out_b, x, mods, mods, mods,
      ln_g.reshape(DEPTH, 1, D_MODEL), ln_b.reshape(DEPTH, 1, D_MODEL), w_router_b)


def _expert_kernel(te_ref, nt_ref, x_ref, gate_ref, wg_ref, wu_ref, wd_ref, y_ref, wg_s, wu_s, wd_s):
    i = pl.program_id(0)
    prev = te_ref[jnp.maximum(i - 1, 0)]
    fresh = jnp.logical_or(i == 0, te_ref[i] != prev)

    @pl.when(fresh)
    def _():
        wg_s[...] = wg_ref[...].astype(BF16)
        wu_s[...] = wu_ref[...].astype(BF16)
        wd_s[...] = wd_ref[...].astype(BF16)

    @pl.when(i < nt_ref[0])
    def _():
        x = x_ref[...]
        g = _dot(x, wg_s[...])
        u = _dot(x, wu_s[...])
        a = (g * jax.nn.sigmoid(g) * u).astype(BF16)
        y_ref[...] = _dot(a, wd_s[...]) * gate_ref[...]


def _experts(l, tile_expert, n_tiles, xs, gates, w_gate, w_up, w_down, max_tiles):
    tm = TM_EXP
    grid_spec = pltpu.PrefetchScalarGridSpec(
        num_scalar_prefetch=2,
        grid=(max_tiles,),
        in_specs=[
            pl.BlockSpec((tm, D_MODEL), lambda i, te, nt: (i, 0)),
            pl.BlockSpec((tm, 1), lambda i, te, nt: (i, 0)),
            pl.BlockSpec((None, None, D_MODEL, D_EXPERT), lambda i, te, nt: (l, te[i], 0, 0)),
            pl.BlockSpec((None, None, D_MODEL, D_EXPERT), lambda i, te, nt: (l, te[i], 0, 0)),
            pl.BlockSpec((None, None, D_EXPERT, D_MODEL), lambda i, te, nt: (l, te[i], 0, 0)),
        ],
        out_specs=pl.BlockSpec((tm, D_MODEL), lambda i, te, nt: (i, 0)),
        scratch_shapes=[pltpu.VMEM((D_MODEL, D_EXPERT), BF16), pltpu.VMEM((D_MODEL, D_EXPERT), BF16),
                        pltpu.VMEM((D_EXPERT, D_MODEL), BF16)],
    )
    return pl.pallas_call(
        _expert_kernel,
        grid_spec=grid_spec,
        out_shape=jax.ShapeDtypeStruct((max_tiles * tm, D_MODEL), F32),
        compiler_params=_cparams(("arbitrary",)),
        name="experts",
    )(tile_expert, n_tiles, xs, gates, w_gate, w_up, w_down)


def _tail_kernel(x1_ref, y_ref, g2_ref, lg_ref, lb_ref, o_ref):
    o_ref[...] = _ln(ALPHA * x1_ref[...] + g2_ref[...] * y_ref[...]) * lg_ref[...] + lb_ref[...]


def _tail(l, x1, y, mods, ln_g, ln_b):
    tm = TM_TAIL
    row = pl.BlockSpec((tm, D_MODEL), lambda i: (i, 0))
    vec = pl.BlockSpec((None, 1, D_MODEL), lambda i: (l, 0, 0))
    return pl.pallas_call(
        _tail_kernel,
        grid=(T_ALL // tm,),
        in_specs=[row, row,
                  pl.BlockSpec((None, None, None, 1, D_MODEL), lambda i: (l, _cond_row(i, tm), 5, 0, 0)),
                  vec, vec],
        out_specs=row,
        out_shape=jax.ShapeDtypeStruct((T_ALL, D_MODEL), F32),
        compiler_params=_cparams(("parallel",)),
        name="tail",
    )(x1, y, mods, ln_g.reshape(DEPTH, 1, D_MODEL), ln_b.reshape(DEPTH, 1, D_MODEL))


def _swap_perm():
    half = ROPE_DIM // 2
    nf = half // 2
    base = np.concatenate([np.arange(nf, half), np.arange(0, nf)])
    return np.concatenate([base, half + base])


def _prep_w_in(w_in):
    kr = w_in[..., C_KR:C_KR + ROPE_DIM]
    return jnp.concatenate([w_in[..., :C_KR + ROPE_DIM], kr[..., _swap_perm()], w_in[..., C_KR + ROPE_DIM:]],
                           axis=-1).astype(BF16)


def _prep_w_uq(w_uq):
    w = w_uq.reshape(DEPTH, Q_LORA, MLA_HEADS, NOPE_DIM + ROPE_DIM)
    rope = w[..., NOPE_DIM:]
    return jnp.concatenate([w, rope[..., _swap_perm()]], axis=-1).reshape(DEPTH, Q_LORA, MLA_HEADS * MLA_QK).astype(BF16)


def _prep_w_ukv(w_ukv):
    w = w_ukv.reshape(DEPTH, KV_LORA, MLA_HEADS, NOPE_DIM + V_DIM)
    return jnp.concatenate([w[..., :NOPE_DIM].reshape(DEPTH, KV_LORA, MLA_W),
                            w[..., NOPE_DIM:].reshape(DEPTH, KV_LORA, MLA_W)], axis=-1).astype(BF16)


def _rope_table(tm):
    half = ROPE_DIM // 2
    nf = half // 2
    t = jnp.arange(DEC_SEQ)
    inv = ROPE_BASE ** (-jnp.arange(nf, dtype=F32) * 2.0 / half)
    ar = (t // GRID_W).astype(F32)[:, None] * inv[None, :]
    ac = (t % GRID_W).astype(F32)[:, None] * inv[None, :]
    cos = jnp.concatenate([jnp.cos(ar), jnp.cos(ar), jnp.cos(ac), jnp.cos(ac)], axis=-1)
    sin = jnp.concatenate([-jnp.sin(ar), jnp.sin(ar), -jnp.sin(ac), jnp.sin(ac)], axis=-1)
    ident = jnp.concatenate([jnp.ones((tm, ROPE_DIM), F32), jnp.zeros((tm, ROPE_DIM), F32)], axis=-1)
    return jnp.concatenate([ident, jnp.concatenate([cos, sin], axis=-1)], axis=0)


def _na_bias_table(na_rpb):
    reps = list(range(NA_ROWS // 2)) + [NA_ROWS // 2] + list(range(GRID_R - NA_ROWS // 2 + 1, GRID_R))
    rows = np.array(reps)
    row_start = np.clip(rows - NA_ROWS // 2, 0, GRID_R - NA_ROWS)
    d_row = row_start[:, None] + np.arange(NA_ROWS)[None, :] - rows[:, None] + NA_ROWS - 1
    cols = np.arange(GRID_W)
    col_start = np.clip(cols - NA_COLS // 2, 0, GRID_W - NA_COLS)
    col_ok = (cols[None, :] >= col_start[:, None]) & (cols[None, :] < col_start[:, None] + NA_COLS)
    d_col = np.clip(cols[None, :] - cols[:, None] + NA_COLS - 1, 0, 2 * NA_COLS - 2)
    bias = na_rpb[:, :, d_row[:, None, :, None], d_col[None, :, None, :]]
    bias = jnp.where(col_ok[None, None, None, :, None, :], bias, NEG_BIG)
    return bias.reshape(DEPTH, NA_HEADS, NA_ROWS, GRID_W, NA_ROWS * GRID_W)


def _route(logits, b_router):
    s = jax.nn.sigmoid(logits)
    sb = s + b_router.astype(F32)
    grp_score = lax.top_k(sb.reshape(-1, N_GROUPS, EXPERTS_PER_GROUP), TOP_K)[0].sum(-1)
    gsel = jnp.argmax(grp_score, axis=-1)
    in_grp = (jnp.arange(N_EXPERTS) // EXPERTS_PER_GROUP)[None, :] == gsel[:, None]
    _, idx = lax.top_k(jnp.where(in_grp, sb, -jnp.inf), TOP_K)
    w = jnp.take_along_axis(s, idx, axis=-1)
    w = w / w.sum(-1, keepdims=True)
    return idx.astype(jnp.int32), w


def _dispatch_plan(idx, w, max_tiles):
    tm = TM_EXP
    e = idx.reshape(-1)
    onehot = (e[:, None] == jnp.arange(N_EXPERTS)[None, :]).astype(jnp.int32)
    csum = jnp.cumsum(onehot, axis=0)
    counts = csum[-1]
    rank = jnp.take_along_axis(csum, e[:, None], axis=1)[:, 0] - 1
    tiles = (counts + tm - 1) // tm
    tile_end = jnp.cumsum(tiles)
    off = (tile_end - tiles) * tm
    pos = off[e] + rank
    n_slots = max_tiles * tm
    tok = jnp.arange(T_ALL * TOP_K, dtype=jnp.int32) // TOP_K
    src = jnp.zeros((n_slots,), jnp.int32).at[pos].set(tok)
    gate = jnp.zeros((n_slots,), F32).at[pos].set(w.reshape(-1))
    n_tiles = tile_end[-1]
    tile_expert = jnp.searchsorted(tile_end, jnp.arange(max_tiles), side="right").astype(jnp.int32)
    last = jnp.take(tile_expert, jnp.maximum(n_tiles - 1, 0))
    tile_expert = jnp.where(jnp.arange(max_tiles) < n_tiles, tile_expert, last)
    return src, gate, pos.reshape(T_ALL, TOP_K), tile_expert, n_tiles.reshape(1).astype(jnp.int32)


def kernel(x_prompt, x_sample, cache_na_k, cache_na_v, cache_mla_ckv, cache_mla_krope, c, c_ctx, w_ada, b_ada,
           w_in, g_q, g_kv, w_uq, w_ukv, na_rpb, sgu_ln_g, sgu_ln_b, sgu_w, sgu_b, w_out, ln1_g, ln1_b, ln2_g,
           ln2_b, w_router, b_router, w_gate, w_up, w_down):
    max_tiles = (T_ALL * TOP_K) // TM_EXP + N_EXPERTS

    cond = jnp.zeros((N_COND, D_MODEL), F32).at[0].set(c_ctx).at[1:1 + DEC_BATCH].set(c)
    mods = _ada(cond, w_ada, b_ada).reshape(DEPTH, N_COND, 6, 1, D_MODEL)

    w_in_p = _prep_w_in(w_in)
    w_uq_p = _prep_w_uq(w_uq)
    w_ukv_p = _prep_w_ukv(w_ukv)
    w_out_b = w_out.astype(BF16)
    w_router_b = w_router.astype(BF16)
    cs_tab = _rope_table(TM_PROJ)
    bias_tab = _na_bias_table(na_rpb)
    cache_k = cache_na_k.reshape(DEC_BATCH, DEPTH, PAST_LEN, NA_W)
    cache_v = cache_na_v.reshape(DEC_BATCH, DEPTH, PAST_LEN, NA_W)
    kmc, vvc = _ctxkv(cache_mla_ckv, cache_mla_krope, w_ukv_p)

    x = jnp.concatenate([x_prompt.reshape(T_CTX, D_MODEL), x_sample.reshape(T_LAT, D_MODEL)], axis=0)
    ks, vs, ckvs, krs = [], [], [], []
    for l in range(DEPTH):
        q, k, v, k32, v32, qm, km, vv, ckv32, kr32, z = _inproj(l, x, mods, w_in_p, g_q, g_kv, w_uq_p, w_ukv_p, cs_tab)
        ks.append(k32.reshape(BATCH, SEQ, NA_HEADS, NA_HEAD_DIM))
        vs.append(v32.reshape(BATCH, SEQ, NA_HEADS, NA_HEAD_DIM))
        ckvs.append(ckv32.reshape(BATCH, SEQ, KV_LORA))
        krs.append(kr32.reshape(BATCH, SEQ, ROPE_DIM))

        oa, ob = _ctx_attn(q, k, v, qm, km, vv)
        oa = _lat_na(l, q, k, v, cache_k, cache_v, bias_tab, oa)
        ob = _lat_mla(l, qm, km, vv, kmc, vvc, ob)
        oc = _sgu(l, z, sgu_ln_g, sgu_ln_b, sgu_w, sgu_b)

        x1, h2, logits = _outproj(l, oa, ob, oc, w_out_b, x, mods, ln1_g, ln1_b, w_router_b)
        idx, w = _route(logits, b_router)
        src, gate, pos, tile_expert, n_tiles = _dispatch_plan(idx, w, max_tiles)
        xs = jnp.take(h2, src, axis=0)
        ys = _experts(l, tile_expert, n_tiles, xs, gate.reshape(-1, 1), w_gate, w_up, w_down, max_tiles)
        y = jnp.take(ys, pos[:, 0], axis=0) + jnp.take(ys, pos[:, 1], axis=0)
        x = _tail(l, x1, y, mods, ln2_g, ln2_b)

    xp = x[:T_CTX].reshape(BATCH, SEQ, D_MODEL)
    xsamp = x[T_CTX:].reshape(DEC_BATCH, DEC_SEQ, D_MODEL)
    return (xp, xsamp, jnp.stack(ks, axis=1), jnp.stack(vs, axis=1), jnp.stack(ckvs, axis=1), jnp.stack(krs, axis=1))
```

```python
import functools

import numpy as np
import jax
import jax.numpy as jnp
from jax import lax
from jax.experimental import pallas as pl
from jax.experimental.pallas import tpu as pltpu

F32 = jnp.float32
BF16 = jnp.bfloat16

D_MODEL = 2048
BATCH = 32
SEQ = 256
DEPTH = 2
DEC_BATCH = 2
DEC_SEQ = 2048
PAST_LEN = 256
GRID_W = 64
GRID_R = DEC_SEQ // GRID_W
NA_HEADS = 6
NA_HEAD_DIM = 128
NA_ROWS = 8
NA_COLS = 16
MLA_HEADS = 6
Q_LORA = 512
KV_LORA = 512
NOPE_DIM = 128
ROPE_DIM = 64
V_DIM = 128
ROPE_BASE = 10000.0
MLA_SCALE = (NOPE_DIM + ROPE_DIM) ** -0.5
NA_SCALE = NA_HEAD_DIM ** -0.5
SGU_GROUPS = 4
SGU_DIM = 128
CHUNK = 128
NA_W = NA_HEADS * NA_HEAD_DIM
MLA_W = MLA_HEADS * V_DIM
SGU_W = SGU_GROUPS * SGU_DIM
D_MIX = NA_W + MLA_W + SGU_W
N_EXPERTS = 16
N_GROUPS = 4
EXPERTS_PER_GROUP = N_EXPERTS // N_GROUPS
TOP_K = 2
D_EXPERT = 512
ALPHA = (2 * DEPTH) ** 0.25
LN_EPS = 1e-6
RMS_EPS = 1e-6

T_CTX = BATCH * SEQ
T_LAT = DEC_BATCH * DEC_SEQ
T_ALL = T_CTX + T_LAT
N_COND = 8

C_Q, C_K, C_V = 0, NA_W, 2 * NA_W
C_CQ = 3 * NA_W
C_CKV = C_CQ + Q_LORA
C_KR = C_CKV + KV_LORA
C_Z = C_KR + 2 * ROPE_DIM
IN_PAD = C_Z + 2 * SGU_W
MLA_QK = 256

TM_PROJ = 256
TM_OUT = 512
TM_SGU = 256
TM_TAIL = 256
TM_DISP = 512
TM_EXP = 256
TQ_MLA = 256
NEG_BIG = -1e30
VMEM_LIMIT = 56 * 1024 * 1024


def _cparams(sem, vmem=VMEM_LIMIT):
    return pltpu.CompilerParams(dimension_semantics=sem, vmem_limit_bytes=vmem)


def _cond_row(i, tm):
    t0 = i * tm
    return jnp.where(t0 < T_CTX, 0, 1 + (t0 - T_CTX) // DEC_SEQ)


def _ln(x):
    mu = jnp.mean(x, axis=-1, keepdims=True)
    xc = x - mu
    var = jnp.mean(xc * xc, axis=-1, keepdims=True)
    return xc * lax.rsqrt(var + LN_EPS)


def _rms(x):
    return x * lax.rsqrt(jnp.mean(x * x, axis=-1, keepdims=True) + RMS_EPS)


def _dot(a, b):
    return jnp.dot(a, b, preferred_element_type=F32)


def _dot_nt(a, b):
    return lax.dot_general(a, b, (((1,), (1,)), ((), ())), preferred_element_type=F32)


def _ada_kernel(c_ref, w_ref, b_ref, o_ref):
    c = c_ref[...]
    s = c * jax.nn.sigmoid(c)
    o_ref[...] = _dot(s.astype(BF16), w_ref[...].astype(BF16)) + b_ref[...]


def _ada(cond, w_ada, b_ada):
    tn = 1024
    n = 6 * D_MODEL
    return pl.pallas_call(
        _ada_kernel,
        grid=(DEPTH, n // tn),
        in_specs=[
            pl.BlockSpec((N_COND, D_MODEL), lambda l, j: (0, 0)),
            pl.BlockSpec((None, D_MODEL, tn), lambda l, j: (l, 0, j)),
            pl.BlockSpec((None, 1, tn), lambda l, j: (l, 0, j)),
        ],
        out_specs=pl.BlockSpec((None, N_COND, tn), lambda l, j: (l, 0, j)),
        out_shape=jax.ShapeDtypeStruct((DEPTH, N_COND, n), F32),
        compiler_params=_cparams(("parallel", "parallel")),
        name="ada_mod",
    )(cond, w_ada, b_ada.reshape(DEPTH, 1, n))


def _inproj_kernel(x_ref, sh_ref, sc_ref, w_ref, gq_ref, gkv_ref, wuq_ref, wukv_ref, cs_ref,
                   q_ref, k_ref, v_ref, k32_ref, v32_ref, qm_ref, km_ref, vv_ref,
                   ckv32_ref, kr32_ref, z_ref):
    i = pl.program_id(0)
    is_ctx = i * TM_PROJ < T_CTX
    h = (_ln(x_ref[...]) * (1.0 + sc_ref[...]) + sh_ref[...]).astype(BF16)

    def proj(a, b):
        return _dot(h, w_ref[:, a:b])

    q_ref[...] = proj(C_Q, C_K).astype(BF16)
    k = proj(C_K, C_V)
    k_ref[...] = k.astype(BF16)
    v = proj(C_V, C_CQ)
    v_ref[...] = v.astype(BF16)

    cs = cs_ref[...]
    lane = lax.broadcasted_iota(jnp.int32, cs.shape, 1)

    def rotate(pair):
        t = pair * cs
        return jnp.where(lane < ROPE_DIM, t + pltpu.roll(t, ROPE_DIM, 1), 0.0)

    cqn = (_rms(proj(C_CQ, C_CKV)) * gq_ref[...]).astype(BF16)
    mq = _dot(cqn, wuq_ref[...])
    for hd in range(MLA_HEADS):
        c0 = hd * MLA_QK
        qm_ref[:, c0:c0 + NOPE_DIM] = mq[:, c0:c0 + NOPE_DIM].astype(BF16)
        qm_ref[:, c0 + NOPE_DIM:c0 + MLA_QK] = rotate(mq[:, c0 + NOPE_DIM:c0 + MLA_QK]).astype(BF16)

    ckvn = _rms(proj(C_CKV, C_KR)) * gkv_ref[...]
    kv = _dot(ckvn.astype(BF16), wukv_ref[...])
    kr2 = proj(C_KR, C_Z)
    krot = rotate(kr2).astype(BF16)
    for hd in range(MLA_HEADS):
        c0 = hd * MLA_QK
        km_ref[:, c0:c0 + NOPE_DIM] = kv[:, hd * NOPE_DIM:(hd + 1) * NOPE_DIM].astype(BF16)
        km_ref[:, c0 + NOPE_DIM:c0 + MLA_QK] = krot
    vv_ref[...] = kv[:, MLA_HEADS * NOPE_DIM:].astype(BF16)

    z_ref[...] = proj(C_Z, IN_PAD)

    @pl.when(is_ctx)
    def _():
        k32_ref[...] = k
        v32_ref[...] = v
        ckv32_ref[...] = ckvn
        kr32_ref[...] = kr2[:, :ROPE_DIM]


def _inproj(l, x, mods, w_in_p, g_q, g_kv, w_uq_p, w_ukv_p, cs_tab):
    tm = TM_PROJ
    nct = T_CTX // tm
    per_b = DEC_SEQ // tm

    def row(i):
        return (i, 0)

    def ctx_row(i):
        return (jnp.minimum(i, nct - 1), 0)

    def cs_row(i):
        return (jnp.where(i < nct, 0, 1 + (i - nct) % per_b), 0)

    def mod(kind):
        return pl.BlockSpec((None, None, None, 1, D_MODEL), lambda i: (l, _cond_row(i, tm), kind, 0, 0))

    def const2(shape):
        return pl.BlockSpec(shape, lambda i: (l, 0, 0), pipeline_mode=pl.Buffered(1))

    out_shapes = [
        jax.ShapeDtypeStruct((T_ALL, NA_W), BF16),
        jax.ShapeDtypeStruct((T_ALL, NA_W), BF16),
        jax.ShapeDtypeStruct((T_ALL, NA_W), BF16),
        jax.ShapeDtypeStruct((T_CTX, NA_W), F32),
        jax.ShapeDtypeStruct((T_CTX, NA_W), F32),
        jax.ShapeDtypeStruct((T_ALL, MLA_HEADS * MLA_QK), BF16),
        jax.ShapeDtypeStruct((T_ALL, MLA_HEADS * MLA_QK), BF16),
        jax.ShapeDtypeStruct((T_ALL, MLA_W), BF16),
        jax.ShapeDtypeStruct((T_CTX, KV_LORA), F32),
        jax.ShapeDtypeStruct((T_CTX, ROPE_DIM), F32),
        jax.ShapeDtypeStruct((T_ALL, 2 * SGU_W), F32),
    ]
    out_specs = [
        pl.BlockSpec((tm, NA_W), row), pl.BlockSpec((tm, NA_W), row), pl.BlockSpec((tm, NA_W), row),
        pl.BlockSpec((tm, NA_W), ctx_row), pl.BlockSpec((tm, NA_W), ctx_row),
        pl.BlockSpec((tm, MLA_HEADS * MLA_QK), row), pl.BlockSpec((tm, MLA_HEADS * MLA_QK), row),
        pl.BlockSpec((tm, MLA_W), row),
        pl.BlockSpec((tm, KV_LORA), ctx_row), pl.BlockSpec((tm, ROPE_DIM), ctx_row),
        pl.BlockSpec((tm, 2 * SGU_W), row),
    ]
    return pl.pallas_call(
        _inproj_kernel,
        grid=(T_ALL // tm,),
        in_specs=[
            pl.BlockSpec((tm, D_MODEL), row),
            mod(0), mod(1),
            const2((None, D_MODEL, IN_PAD)),
            pl.BlockSpec((None, 1, Q_LORA), lambda i: (l, 0, 0)),
            pl.BlockSpec((None, 1, KV_LORA), lambda i: (l, 0, 0)),
            const2((None, Q_LORA, MLA_HEADS * MLA_QK)),
            const2((None, KV_LORA, 2 * MLA_W)),
            pl.BlockSpec((tm, 2 * ROPE_DIM), cs_row),
        ],
        out_specs=out_specs,
        out_shape=out_shapes,
        compiler_params=_cparams(("arbitrary",)),
        name="in_proj",
    )(x, mods, mods, w_in_p, g_q.reshape(DEPTH, 1, Q_LORA), g_kv.reshape(DEPTH, 1, KV_LORA),
      w_uq_p, w_ukv_p, cs_tab)


def _ctxkv_kernel(ckv_ref, kr_ref, w_ref, km_ref, vv_ref):
    kv = _dot(ckv_ref[...].astype(BF16), w_ref[...])
    kr = jnp.concatenate([kr_ref[...], jnp.zeros((PAST_LEN, ROPE_DIM), F32)], axis=-1).astype(BF16)
    for hd in range(MLA_HEADS):
        c0 = hd * MLA_QK
        km_ref[:, c0:c0 + NOPE_DIM] = kv[:, hd * NOPE_DIM:(hd + 1) * NOPE_DIM].astype(BF16)
        km_ref[:, c0 + NOPE_DIM:c0 + MLA_QK] = kr
    vv_ref[...] = kv[:, MLA_HEADS * NOPE_DIM:].astype(BF16)


def _ctxkv(cache_ckv, cache_krope, w_ukv_p):
    return pl.pallas_call(
        _ctxkv_kernel,
        grid=(DEPTH, DEC_BATCH),
        in_specs=[
            pl.BlockSpec((None, None, PAST_LEN, KV_LORA), lambda l, b: (b, l, 0, 0)),
            pl.BlockSpec((None, None, PAST_LEN, ROPE_DIM), lambda l, b: (b, l, 0, 0)),
            pl.BlockSpec((None, KV_LORA, 2 * MLA_W), lambda l, b: (l, 0, 0)),
        ],
        out_specs=[
            pl.BlockSpec((None, None, PAST_LEN, MLA_HEADS * MLA_QK), lambda l, b: (l, b, 0, 0)),
            pl.BlockSpec((None, None, PAST_LEN, MLA_W), lambda l, b: (l, b, 0, 0)),
        ],
        out_shape=[
            jax.ShapeDtypeStruct((DEPTH, DEC_BATCH, PAST_LEN, MLA_HEADS * MLA_QK), BF16),
            jax.ShapeDtypeStruct((DEPTH, DEC_BATCH, PAST_LEN, MLA_W), BF16),
        ],
        compiler_params=_cparams(("parallel", "parallel")),
        name="ctx_cache_kv",
    )(cache_ckv, cache_krope, w_ukv_p)


def _softmax_pv(blocks):
    ss = [s for s, _ in blocks]
    m = ss[0].max(axis=-1, keepdims=True)
    for s in ss[1:]:
        m = jnp.maximum(m, s.max(axis=-1, keepdims=True))
    num = None
    den = None
    for s, (_, v) in zip(ss, blocks):
        e = jnp.exp(s - m)
        d = e.sum(axis=-1, keepdims=True)
        o = _dot(e.astype(BF16), v)
        num = o if num is None else num + o
        den = d if den is None else den + d
    return num / den


def _ctx_attn_kernel(q_ref, k_ref, v_ref, qm_ref, km_ref, vv_ref, oa_ref, ob_ref):
    for hd in range(NA_HEADS):
        sl = slice(hd * NA_HEAD_DIM, (hd + 1) * NA_HEAD_DIM)
        s = _dot_nt(q_ref[:, sl], k_ref[:, sl]) * NA_SCALE
        oa_ref[:, sl] = _softmax_pv([(s, v_ref[:, sl])]).astype(BF16)
    for hd in range(MLA_HEADS):
        sq = slice(hd * MLA_QK, (hd + 1) * MLA_QK)
        sv = slice(hd * V_DIM, (hd + 1) * V_DIM)
        s = _dot_nt(qm_ref[:, sq], km_ref[:, sq]) * MLA_SCALE
        ob_ref[:, sv] = _softmax_pv([(s, vv_ref[:, sv])]).astype(BF16)


def _ctx_attn(q, k, v, qm, km, vv):
    def spec(w):
        return pl.BlockSpec((SEQ, w), lambda b: (b, 0))

    return pl.pallas_call(
        _ctx_attn_kernel,
        grid=(BATCH,),
        in_specs=[spec(NA_W), spec(NA_W), spec(NA_W),
                  spec(MLA_HEADS * MLA_QK), spec(MLA_HEADS * MLA_QK), spec(MLA_W)],
        out_specs=[spec(NA_W), spec(MLA_W)],
        out_shape=[jax.ShapeDtypeStruct((T_CTX, NA_W), BF16), jax.ShapeDtypeStruct((T_CTX, MLA_W), BF16)],
        compiler_params=_cparams(("parallel",)),
        name="ctx_attn",
    )(q, k, v, qm, km, vv)


def _lat_mla_kernel(q_ref, k_ref, v_ref, kc_ref, vc_ref, o_ref):
    q = q_ref[...]
    s1 = _dot_nt(q, k_ref[...]) * MLA_SCALE
    s2 = _dot_nt(q, kc_ref[...]) * MLA_SCALE
    o_ref[...] = _softmax_pv([(s1, v_ref[...]), (s2, vc_ref[...])]).astype(BF16)


def _lat_mla(l, qm, km, vv, kmc, vvc):
    tq = TQ_MLA
    nq = DEC_SEQ // tq
    row0 = T_CTX // tq
    kb0 = T_CTX // DEC_SEQ
    return pl.pallas_call(
        _lat_mla_kernel,
        grid=(DEC_BATCH, MLA_HEADS, nq),
        in_specs=[
            pl.BlockSpec((tq, MLA_QK), lambda b, h, t: (row0 + b * nq + t, h)),
            pl.BlockSpec((DEC_SEQ, MLA_QK), lambda b, h, t: (kb0 + b, h)),
            pl.BlockSpec((DEC_SEQ, V_DIM), lambda b, h, t: (kb0 + b, h)),
            pl.BlockSpec((None, None, PAST_LEN, MLA_QK), lambda b, h, t: (l, b, 0, h)),
            pl.BlockSpec((None, None, PAST_LEN, V_DIM), lambda b, h, t: (l, b, 0, h)),
        ],
        out_specs=pl.BlockSpec((tq, V_DIM), lambda b, h, t: (b * nq + t, h)),
        out_shape=jax.ShapeDtypeStruct((T_LAT, MLA_W), BF16),
        compiler_params=_cparams(("parallel", "parallel", "arbitrary")),
        name="lat_mla",
    )(qm, km, vv, kmc, vvc)


def _lat_na_kernel(q_ref, k_ref, v_ref, ck_ref, cv_ref, bias_ref, o_ref):
    ck = ck_ref[...].astype(BF16)
    cv = cv_ref[...].astype(BF16)
    win = NA_ROWS * GRID_W

    def body(r, carry):
        rs = jnp.clip(r - NA_ROWS // 2, 0, GRID_R - NA_ROWS)
        tb = jnp.where(r < NA_ROWS // 2, r, jnp.where(r > GRID_R - NA_ROWS // 2, r - (GRID_R - NA_ROWS), NA_ROWS // 2))
        q0 = pl.multiple_of(r * GRID_W, GRID_W)
        k0 = pl.multiple_of(rs * GRID_W, GRID_W)
        q = q_ref[pl.ds(q0, GRID_W), :]
        kw = k_ref[pl.ds(k0, win), :]
        vw = v_ref[pl.ds(k0, win), :]
        s1 = _dot_nt(q, kw) * NA_SCALE + bias_ref[tb]
        s2 = _dot_nt(q, ck) * NA_SCALE
        o_ref[pl.ds(q0, GRID_W), :] = _softmax_pv([(s1, vw), (s2, cv)]).astype(BF16)
        return carry

    lax.fori_loop(0, GRID_R, body, 0)


def _lat_na(l, q, k, v, cache_k, cache_v, bias_tab):
    kb0 = T_CTX // DEC_SEQ

    def lat(b, h):
        return (kb0 + b, h)

    return pl.pallas_call(
        _lat_na_kernel,
        grid=(DEC_BATCH, NA_HEADS),
        in_specs=[
            pl.BlockSpec((DEC_SEQ, NA_HEAD_DIM), lat),
            pl.BlockSpec((DEC_SEQ, NA_HEAD_DIM), lat),
            pl.BlockSpec((DEC_SEQ, NA_HEAD_DIM), lat),
            pl.BlockSpec((None, None, PAST_LEN, NA_HEAD_DIM), lambda b, h: (b, l, 0, h)),
            pl.BlockSpec((None, None, PAST_LEN, NA_HEAD_DIM), lambda b, h: (b, l, 0, h)),
            pl.BlockSpec((None, None, NA_ROWS, GRID_W, NA_ROWS * GRID_W), lambda b, h: (l, h, 0, 0, 0)),
        ],
        out_specs=pl.BlockSpec((DEC_SEQ, NA_HEAD_DIM), lambda b, h: (b, h)),
        out_shape=jax.ShapeDtypeStruct((T_LAT, NA_W), BF16),
        compiler_params=_cparams(("parallel", "parallel")),
        name="lat_na",
    )(q, k, v, cache_k, cache_v, bias_tab)


def _sgu_kernel(z_ref, g_ref, b_ref, w_ref, bs_ref, o_ref):
    z = z_ref[...]
    a = 0.5 * z * (1.0 + lax.erf(z * (0.5 ** 0.5)))
    for c in range(TM_SGU // CHUNK):
        rows = slice(c * CHUNK, (c + 1) * CHUNK)
        for g in range(SGU_GROUPS):
            u = a[rows, g * SGU_DIM:(g + 1) * SGU_DIM]
            v = a[rows, SGU_W + g * SGU_DIM:SGU_W + (g + 1) * SGU_DIM]
            vn = _ln(v) * g_ref[g:g + 1, :] + b_ref[g:g + 1, :]
            t = _dot(w_ref[g].astype(BF16), vn.astype(BF16)) + bs_ref[g]
            o_ref[rows, g * SGU_DIM:(g + 1) * SGU_DIM] = (u * t).astype(BF16)


def _sgu(l, z, ln_g, ln_b, w_s, b_s):
    tm = TM_SGU
    return pl.pallas_call(
        _sgu_kernel,
        grid=(T_ALL // tm,),
        in_specs=[
            pl.BlockSpec((tm, 2 * SGU_W), lambda i: (i, 0)),
            pl.BlockSpec((None, SGU_GROUPS, SGU_DIM), lambda i: (l, 0, 0)),
            pl.BlockSpec((None, SGU_GROUPS, SGU_DIM), lambda i: (l, 0, 0)),
            pl.BlockSpec((None, SGU_GROUPS, CHUNK, CHUNK), lambda i: (l, 0, 0, 0)),
            pl.BlockSpec((None, SGU_GROUPS, CHUNK, 1), lambda i: (l, 0, 0, 0)),
        ],
        out_specs=pl.BlockSpec((tm, SGU_W), lambda i: (i, 0)),
        out_shape=jax.ShapeDtypeStruct((T_ALL, SGU_W), BF16),
        compiler_params=_cparams(("parallel",)),
        name="sgu",
    )(z, ln_g, ln_b, w_s, b_s.reshape(DEPTH, SGU_GROUPS, CHUNK, 1))


def _route_rows(logits_t, bias_col):
    s = jax.nn.sigmoid(logits_t)
    sb = s + bias_col
    rows = [sb[r:r + 1, :] for r in range(N_EXPERTS)]
    srows = [s[r:r + 1, :] for r in range(N_EXPERTS)]
    best, gsel = None, None
    for g in range(N_GROUPS):
        m = rows[g * EXPERTS_PER_GROUP:(g + 1) * EXPERTS_PER_GROUP]
        score = None
        for a in range(EXPERTS_PER_GROUP):
            for b in range(a + 1, EXPERTS_PER_GROUP):
                pair = m[a] + m[b]
                score = pair if score is None else jnp.maximum(score, pair)
        if best is None:
            best, gsel = score, jnp.zeros(score.shape, jnp.int32)
        else:
            upd = score > best
            best = jnp.where(upd, score, best)
            gsel = jnp.where(upd, g, gsel)

    def pick(table, j):
        out = table[j]
        for g in range(1, N_GROUPS):
            out = jnp.where(gsel == g, table[g * EXPERTS_PER_GROUP + j], out)
        return out

    v = [pick(rows, j) for j in range(EXPERTS_PER_GROUP)]
    sv = [pick(srows, j) for j in range(EXPERTS_PER_GROUP)]
    m1, i1, w1 = v[0], jnp.zeros(gsel.shape, jnp.int32), sv[0]
    for j in range(1, EXPERTS_PER_GROUP):
        upd = v[j] > m1
        m1 = jnp.where(upd, v[j], m1)
        i1 = jnp.where(upd, j, i1)
        w1 = jnp.where(upd, sv[j], w1)
    m2 = jnp.full(gsel.shape, -jnp.inf, F32)
    i2 = jnp.zeros(gsel.shape, jnp.int32)
    w2 = jnp.zeros(gsel.shape, F32)
    for j in range(EXPERTS_PER_GROUP):
        upd = jnp.logical_and(i1 != j, v[j] > m2)
        m2 = jnp.where(upd, v[j], m2)
        i2 = jnp.where(upd, j, i2)
        w2 = jnp.where(upd, sv[j], w2)
    den = w1 + w2
    return gsel * EXPERTS_PER_GROUP + i1, gsel * EXPERTS_PER_GROUP + i2, w1 / den, w2 / den


def _outproj_kernel(ac_ref, al_ref, bc_ref, bl_ref, c_ref, wa_ref, wb_ref, wc_ref, x_ref, g1_ref, sh2_ref, sc2_ref,
                    lg_ref, lb_ref, wrt_ref, br_ref, x1_ref, hp_ref, e_ref, rank_ref, gate_ref, cnt_ref, base_ref):
    tm = TM_OUT
    is_ctx = pl.program_id(0) * tm < T_CTX
    a = jnp.where(is_ctx, ac_ref[...], al_ref[...])
    b = jnp.where(is_ctx, bc_ref[...], bl_ref[...])

    @pl.when(pl.program_id(0) == 0)
    def _():
        base_ref[...] = jnp.zeros_like(base_ref)

    mixed = _dot(a, wa_ref[...]) + _dot(b, wb_ref[...]) + _dot(c_ref[...], wc_ref[...])
    x1 = _ln(ALPHA * x_ref[...] + g1_ref[...] * mixed) * lg_ref[...] + lb_ref[...]
    x1_ref[...] = x1
    h2 = (_ln(x1) * (1.0 + sc2_ref[...]) + sh2_ref[...]).astype(BF16)

    hf = h2.astype(F32)
    hi = pltpu.bitcast(hf[:, :D_MODEL // 2], jnp.uint32)
    lo = pltpu.bitcast(hf[:, D_MODEL // 2:], jnp.uint32)
    hp_ref[...] = hi | (lo >> 16)

    e1, e2, w1, w2 = _route_rows(_dot_nt(wrt_ref[...], h2), br_ref[...])
    e_ref[0:1, :] = e1
    e_ref[1:2, :] = e2
    gate_ref[0:1, :] = w1
    gate_ref[1:2, :] = w2

    sub = lax.broadcasted_iota(jnp.int32, (N_EXPERTS, tm), 0)
    o1 = (sub == e1).astype(F32)
    o2 = (sub == e2).astype(F32)
    before = (lax.broadcasted_iota(jnp.int32, (tm, tm), 0) < lax.broadcasted_iota(jnp.int32, (tm, tm), 1)).astype(BF16)
    p1 = _dot(o1.astype(BF16), before)
    p2 = _dot(o2.astype(BF16), before)
    c1 = o1.sum(axis=1, keepdims=True)
    c2 = o2.sum(axis=1, keepdims=True)
    base = base_ref[:, 0:1]
    rank_ref[0:1, :] = jnp.sum(o1 * (base + p1), axis=0, keepdims=True).astype(jnp.int32)
    rank_ref[1:2, :] = jnp.sum(o2 * (base + c1 + p2), axis=0, keepdims=True).astype(jnp.int32)
    base_ref[...] = base_ref[...] + (c1 + c2)
    cnt_ref[...] = base_ref[...]


def _outproj(l, oa_c, oa_l, ob_c, ob_l, oc, w_out_b, x, mods, ln_g, ln_b, w_router_t, b_router_col):
    tm = TM_OUT
    nct = T_CTX // tm

    def row(i):
        return (i, 0)

    def ctx_row(i):
        return (jnp.minimum(i, nct - 1), 0)

    def lat_row(i):
        return (jnp.maximum(i - nct, 0), 0)

    def tok(i):
        return (0, i)

    def mod(kind):
        return pl.BlockSpec((None, None, None, 1, D_MODEL), lambda i: (l, _cond_row(i, tm), kind, 0, 0))

    def wblk(rows, blk):
        return pl.BlockSpec((None, rows, D_MODEL), lambda i: (l, blk, 0), pipeline_mode=pl.Buffered(1))

    vec = pl.BlockSpec((None, 1, D_MODEL), lambda i: (l, 0, 0))
    return pl.pallas_call(
        _outproj_kernel,
        grid=(T_ALL // tm,),
        in_specs=[
            pl.BlockSpec((tm, NA_W), ctx_row), pl.BlockSpec((tm, NA_W), lat_row),
            pl.BlockSpec((tm, MLA_W), ctx_row), pl.BlockSpec((tm, MLA_W), lat_row),
            pl.BlockSpec((tm, SGU_W), row),
            wblk(NA_W, 0), wblk(MLA_W, 1), wblk(SGU_W, (NA_W + MLA_W) // SGU_W),
            pl.BlockSpec((tm, D_MODEL), row),
            mod(2), mod(3), mod(4), vec, vec,
            pl.BlockSpec((N_EXPERTS, D_MODEL), lambda i: (0, 0)),
            pl.BlockSpec((N_EXPERTS, 1), lambda i: (0, 0)),
        ],
        out_specs=[pl.BlockSpec((tm, D_MODEL), row), pl.BlockSpec((tm, D_MODEL // 2), row),
                   pl.BlockSpec((TOP_K, tm), tok), pl.BlockSpec((TOP_K, tm), tok), pl.BlockSpec((TOP_K, tm), tok),
                   pl.BlockSpec((N_EXPERTS, 128), lambda i: (0, 0))],
        out_shape=[jax.ShapeDtypeStruct((T_ALL, D_MODEL), F32),
                   jax.ShapeDtypeStruct((T_ALL, D_MODEL // 2), jnp.uint32),
                   jax.ShapeDtypeStruct((TOP_K, T_ALL), jnp.int32),
                   jax.ShapeDtypeStruct((TOP_K, T_ALL), jnp.int32),
                   jax.ShapeDtypeStruct((TOP_K, T_ALL), F32),
                   jax.ShapeDtypeStruct((N_EXPERTS, 128), F32)],
        scratch_shapes=[pltpu.VMEM((N_EXPERTS, 128), F32)],
        compiler_params=_cparams(("arbitrary",)),
        name="out_proj",
    )(oa_c, oa_l, ob_c, ob_l, oc, w_out_b, w_out_b, w_out_b, x, mods, mods, mods,
      ln_g.reshape(DEPTH, 1, D_MODEL), ln_b.reshape(DEPTH, 1, D_MODEL), w_router_t, b_router_col)


def _dispatch_plan(e, rank, counts, max_tiles):
    tiles = (counts + TM_EXP - 1) // TM_EXP
    tile_end = jnp.cumsum(tiles)
    off = (tile_end - tiles) * TM_EXP
    n_tiles = tile_end[-1]
    ids = jnp.arange(N_EXPERTS, dtype=jnp.int32)
    pos = rank + jnp.sum(jnp.where(e[..., None] == ids, off, 0), axis=-1)
    ti = jnp.arange(max_tiles, dtype=jnp.int32)
    owner = jnp.sum((tile_end[None, :] <= ti[:, None]).astype(jnp.int32), axis=1)
    last = jnp.sum((tile_end <= n_tiles - 1).astype(jnp.int32))
    tile_expert = jnp.where(ti < n_tiles, owner, last).astype(jnp.int32)
    return pos.astype(jnp.int32), tile_expert, n_tiles.reshape(1).astype(jnp.int32)


def _row_copy(src_ref, src_row, dst_ref, dst_row, sem):
    return pltpu.make_async_copy(src_ref.at[pl.ds(src_row, 1)], dst_ref.at[pl.ds(dst_row, 1)], sem)


def _dispatch_kernel(pos_ref, hp_ref, zeros_ref, xs_ref, sem):
    del zeros_ref
    row0 = pl.program_id(0) * TM_DISP

    def issue(t, carry):
        for k in range(TOP_K):
            _row_copy(hp_ref, row0 + t, xs_ref, pos_ref[k, t], sem).start()
        return carry

    lax.fori_loop(0, TM_DISP, issue, 0, unroll=8)

    def drain(t, carry):
        for k in range(TOP_K):
            _row_copy(hp_ref, 0, xs_ref, 0, sem).wait()
        return carry

    lax.fori_loop(0, TM_DISP, drain, 0, unroll=8)


def _dispatch(pos, hp, xs_zero):
    return pl.pallas_call(
        _dispatch_kernel,
        grid=(T_ALL // TM_DISP,),
        in_specs=[
            pl.BlockSpec((TOP_K, TM_DISP), lambda i: (0, i), memory_space=pltpu.SMEM),
            pl.BlockSpec(memory_space=pl.ANY),
            pl.BlockSpec(memory_space=pl.ANY),
        ],
        out_specs=pl.BlockSpec(memory_space=pl.ANY),
        out_shape=jax.ShapeDtypeStruct(xs_zero.shape, xs_zero.dtype),
        input_output_aliases={2: 0},
        scratch_shapes=[pltpu.SemaphoreType.DMA(())],
        compiler_params=_cparams(("arbitrary",)),
        name="dispatch",
    )(pos, hp, xs_zero)


def _expert_kernel(te_ref, nt_ref, x_ref, wg_ref, wu_ref, wd_ref, y_ref, wg_s, wu_s, wd_s):
    i = pl.program_id(0)
    prev = te_ref[jnp.maximum(i - 1, 0)]
    fresh = jnp.logical_or(i == 0, te_ref[i] != prev)

    @pl.when(fresh)
    def _():
        wg_s[...] = wg_ref[...].astype(BF16)
        wu_s[...] = wu_ref[...].astype(BF16)
        wd_s[...] = wd_ref[...].astype(BF16)

    @pl.when(i < nt_ref[0])
    def _():
        xp = x_ref[...]
        hi = pltpu.bitcast(xp & jnp.uint32(0xFFFF0000), F32)
        lo = pltpu.bitcast(xp << 16, F32)
        x = jnp.concatenate([hi, lo], axis=-1).astype(BF16)
        g = _dot(x, wg_s[...])
        u = _dot(x, wu_s[...])
        a = (g * jax.nn.sigmoid(g) * u).astype(BF16)
        y_ref[...] = _dot(a, wd_s[...])

    @pl.when(i >= nt_ref[0])
    def _():
        y_ref[...] = jnp.zeros_like(y_ref)


def _experts(l, tile_expert, n_tiles, xs, w_gate, w_up, w_down, max_tiles):
    tm = TM_EXP
    grid_spec = pltpu.PrefetchScalarGridSpec(
        num_scalar_prefetch=2,
        grid=(max_tiles,),
        in_specs=[
            pl.BlockSpec((tm, D_MODEL // 2), lambda i, te, nt: (i, 0)),
            pl.BlockSpec((None, None, D_MODEL, D_EXPERT), lambda i, te, nt: (l, te[i], 0, 0)),
            pl.BlockSpec((None, None, D_MODEL, D_EXPERT), lambda i, te, nt: (l, te[i], 0, 0)),
            pl.BlockSpec((None, None, D_EXPERT, D_MODEL), lambda i, te, nt: (l, te[i], 0, 0)),
        ],
        out_specs=pl.BlockSpec((tm, D_MODEL), lambda i, te, nt: (i, 0)),
        scratch_shapes=[pltpu.VMEM((D_MODEL, D_EXPERT), BF16), pltpu.VMEM((D_MODEL, D_EXPERT), BF16),
                        pltpu.VMEM((D_EXPERT, D_MODEL), BF16)],
    )
    return pl.pallas_call(
        _expert_kernel,
        grid_spec=grid_spec,
        out_shape=jax.ShapeDtypeStruct((max_tiles * tm, D_MODEL), F32),
        compiler_params=_cparams(("arbitrary",)),
        name="experts",
    )(tile_expert, n_tiles, xs, w_gate, w_up, w_down)


def _tail_kernel(pos_ref, posn_ref, x1_ref, gate_ref, g2_ref, lg_ref, lb_ref, ys_ref, o_ref, ybuf, sem):
    tm = TM_TAIL
    i = pl.program_id(0)
    slot = i % 2

    def issue(p_ref, s):
        def body(t, carry):
            for k in range(TOP_K):
                _row_copy(ys_ref, p_ref[k, t], ybuf.at[s, k], t, sem.at[s]).start()
            return carry

        lax.fori_loop(0, tm, body, 0, unroll=8)

    @pl.when(i == 0)
    def _():
        issue(pos_ref, 0)

    @pl.when(i + 1 < pl.num_programs(0))
    def _():
        issue(posn_ref, 1 - slot)

    def drain(t, carry):
        for k in range(TOP_K):
            _row_copy(ys_ref, 0, ybuf.at[slot, k], 0, sem.at[slot]).wait()
        return carry

    lax.fori_loop(0, tm, drain, 0, unroll=8)

    gate = gate_ref[...]
    y = gate[:, 0:1] * ybuf[slot, 0] + gate[:, 1:2] * ybuf[slot, 1]
    o_ref[...] = _ln(ALPHA * x1_ref[...] + g2_ref[...] * y) * lg_ref[...] + lb_ref[...]


def _tail(l, pos, x1, gate_t, mods, ln_g, ln_b, ys):
    tm = TM_TAIL
    n = T_ALL // tm
    row = pl.BlockSpec((tm, D_MODEL), lambda i: (i, 0))
    vec = pl.BlockSpec((None, 1, D_MODEL), lambda i: (l, 0, 0))
    return pl.pallas_call(
        _tail_kernel,
        grid=(n,),
        in_specs=[
            pl.BlockSpec((TOP_K, tm), lambda i: (0, i), memory_space=pltpu.SMEM),
            pl.BlockSpec((TOP_K, tm), lambda i: (0, jnp.minimum(i + 1, n - 1)), memory_space=pltpu.SMEM),
            row,
            pl.BlockSpec((tm, TOP_K), lambda i: (i, 0)),
            pl.BlockSpec((None, None, None, 1, D_MODEL), lambda i: (l, _cond_row(i, tm), 5, 0, 0)),
            vec, vec,
            pl.BlockSpec(memory_space=pl.ANY),
        ],
        out_specs=row,
        out_shape=jax.ShapeDtypeStruct((T_ALL, D_MODEL), F32),
        scratch_shapes=[pltpu.VMEM((2, TOP_K, tm, D_MODEL), F32), pltpu.SemaphoreType.DMA((2,))],
        compiler_params=_cparams(("arbitrary",)),
        name="tail",
    )(pos, pos, x1, gate_t, mods, ln_g.reshape(DEPTH, 1, D_MODEL), ln_b.reshape(DEPTH, 1, D_MODEL), ys)


def _swap_perm():
    half = ROPE_DIM // 2
    nf = half // 2
    base = np.concatenate([np.arange(nf, half), np.arange(0, nf)])
    return np.concatenate([base, half + base])


def _prep_w_in(w_in):
    kr = w_in[..., C_KR:C_KR + ROPE_DIM]
    return jnp.concatenate([w_in[..., :C_KR + ROPE_DIM], kr[..., _swap_perm()], w_in[..., C_KR + ROPE_DIM:]],
                           axis=-1).astype(BF16)


def _prep_w_uq(w_uq):
    w = w_uq.reshape(DEPTH, Q_LORA, MLA_HEADS, NOPE_DIM + ROPE_DIM)
    rope = w[..., NOPE_DIM:]
    return jnp.concatenate([w, rope[..., _swap_perm()]], axis=-1).reshape(DEPTH, Q_LORA, MLA_HEADS * MLA_QK).astype(BF16)


def _prep_w_ukv(w_ukv):
    w = w_ukv.reshape(DEPTH, KV_LORA, MLA_HEADS, NOPE_DIM + V_DIM)
    return jnp.concatenate([w[..., :NOPE_DIM].reshape(DEPTH, KV_LORA, MLA_W),
                            w[..., NOPE_DIM:].reshape(DEPTH, KV_LORA, MLA_W)], axis=-1).astype(BF16)


def _rope_table(tm):
    half = ROPE_DIM // 2
    nf = half // 2
    t = jnp.arange(DEC_SEQ)
    inv = ROPE_BASE ** (-jnp.arange(nf, dtype=F32) * 2.0 / half)
    ar = (t // GRID_W).astype(F32)[:, None] * inv[None, :]
    ac = (t % GRID_W).astype(F32)[:, None] * inv[None, :]
    cos = jnp.concatenate([jnp.cos(ar), jnp.cos(ar), jnp.cos(ac), jnp.cos(ac)], axis=-1)
    sin = jnp.concatenate([-jnp.sin(ar), jnp.sin(ar), -jnp.sin(ac), jnp.sin(ac)], axis=-1)
    ident = jnp.concatenate([jnp.ones((tm, ROPE_DIM), F32), jnp.zeros((tm, ROPE_DIM), F32)], axis=-1)
    return jnp.concatenate([ident, jnp.concatenate([cos, sin], axis=-1)], axis=0)


def _na_bias_table(na_rpb):
    reps = list(range(NA_ROWS // 2)) + [NA_ROWS // 2] + list(range(GRID_R - NA_ROWS // 2 + 1, GRID_R))
    rows = np.array(reps)
    row_start = np.clip(rows - NA_ROWS // 2, 0, GRID_R - NA_ROWS)
    d_row = row_start[:, None] + np.arange(NA_ROWS)[None, :] - rows[:, None] + NA_ROWS - 1
    cols = np.arange(GRID_W)
    col_start = np.clip(cols - NA_COLS // 2, 0, GRID_W - NA_COLS)
    col_ok = (cols[None, :] >= col_start[:, None]) & (cols[None, :] < col_start[:, None] + NA_COLS)
    d_col = np.clip(cols[None, :] - cols[:, None] + NA_COLS - 1, 0, 2 * NA_COLS - 2)
    bias = na_rpb[:, :, d_row[:, None, :, None], d_col[None, :, None, :]]
    bias = jnp.where(col_ok[None, None, None, :, None, :], bias, NEG_BIG)
    return bias.reshape(DEPTH, NA_HEADS, NA_ROWS, GRID_W, NA_ROWS * GRID_W)


def kernel(x_prompt, x_sample, cache_na_k, cache_na_v, cache_mla_ckv, cache_mla_krope, c, c_ctx, w_ada, b_ada,
           w_in, g_q, g_kv, w_uq, w_ukv, na_rpb, sgu_ln_g, sgu_ln_b, sgu_w, sgu_b, w_out, ln1_g, ln1_b, ln2_g,
           ln2_b, w_router, b_router, w_gate, w_up, w_down):
    max_tiles = (T_ALL * TOP_K) // TM_EXP + N_EXPERTS

    cond = jnp.zeros((N_COND, D_MODEL), F32).at[0].set(c_ctx).at[1:1 + DEC_BATCH].set(c)
    mods = _ada(cond, w_ada, b_ada).reshape(DEPTH, N_COND, 6, 1, D_MODEL)

    w_in_p = _prep_w_in(w_in)
    w_uq_p = _prep_w_uq(w_uq)
    w_ukv_p = _prep_w_ukv(w_ukv)
    w_out_b = w_out.astype(BF16)
    w_router_t = w_router.T.astype(BF16)
    b_router_col = b_router.reshape(N_EXPERTS, 1).astype(F32)
    cs_tab = _rope_table(TM_PROJ)
    bias_tab = _na_bias_table(na_rpb)
    cache_k = cache_na_k.reshape(DEC_BATCH, DEPTH, PAST_LEN, NA_W)
    cache_v = cache_na_v.reshape(DEC_BATCH, DEPTH, PAST_LEN, NA_W)
    kmc, vvc = _ctxkv(cache_mla_ckv, cache_mla_krope, w_ukv_p)

    x = jnp.concatenate([x_prompt.reshape(T_CTX, D_MODEL), x_sample.reshape(T_LAT, D_MODEL)], axis=0)
    ks, vs, ckvs, krs = [], [], [], []
    for l in range(DEPTH):
        q, k, v, k32, v32, qm, km, vv, ckv32, kr32, z = _inproj(l, x, mods, w_in_p, g_q, g_kv, w_uq_p, w_ukv_p, cs_tab)
        ks.append(k32.reshape(BATCH, SEQ, NA_HEADS, NA_HEAD_DIM))
        vs.append(v32.reshape(BATCH, SEQ, NA_HEADS, NA_HEAD_DIM))
        ckvs.append(ckv32.reshape(BATCH, SEQ, KV_LORA))
        krs.append(kr32.reshape(BATCH, SEQ, ROPE_DIM))

        oa_c, ob_c = _ctx_attn(q, k, v, qm, km, vv)
        oa_l = _lat_na(l, q, k, v, cache_k, cache_v, bias_tab)
        ob_l = _lat_mla(l, qm, km, vv, kmc, vvc)
        oc = _sgu(l, z, sgu_ln_g, sgu_ln_b, sgu_w, sgu_b)

        x1, hp, e, rank, gate, cnt = _outproj(l, oa_c, oa_l, ob_c, ob_l, oc, w_out_b, x, mods, ln1_g, ln1_b, w_router_t, b_router_col)
        pos, tile_expert, n_tiles = _dispatch_plan(e, rank, cnt[:, 0].astype(jnp.int32), max_tiles)
        xs = _dispatch(pos, hp, jnp.zeros((max_tiles * TM_EXP, D_MODEL // 2), jnp.uint32))
        ys = _experts(l, tile_expert, n_tiles, xs, w_gate, w_up, w_down, max_tiles)
        x = _tail(l, pos, x1, gate.T, mods, ln2_g, ln2_b, ys)

    xp = x[:T_CTX].reshape(BATCH, SEQ, D_MODEL)
    xsamp = x[T_CTX:].reshape(DEC_BATCH, DEC_SEQ, D_MODEL)
    return (xp, xsamp, jnp.stack(ks, axis=1), jnp.stack(vs, axis=1), jnp.stack(ckvs, axis=1), jnp.stack(krs, axis=1))
```

```python
import functools

import numpy as np
import jax
import jax.numpy as jnp
from jax import lax
from jax.experimental import pallas as pl
from jax.experimental.pallas import tpu as pltpu

F32 = jnp.float32
BF16 = jnp.bfloat16

D_MODEL = 2048
BATCH = 32
SEQ = 256
DEPTH = 2
DEC_BATCH = 2
DEC_SEQ = 2048
PAST_LEN = 256
GRID_W = 64
GRID_R = DEC_SEQ // GRID_W
NA_HEADS = 6
NA_HEAD_DIM = 128
NA_ROWS = 8
NA_COLS = 16
MLA_HEADS = 6
Q_LORA = 512
KV_LORA = 512
NOPE_DIM = 128
ROPE_DIM = 64
V_DIM = 128
ROPE_BASE = 10000.0
MLA_SCALE = (NOPE_DIM + ROPE_DIM) ** -0.5
NA_SCALE = NA_HEAD_DIM ** -0.5
SGU_GROUPS = 4
SGU_DIM = 128
CHUNK = 128
NA_W = NA_HEADS * NA_HEAD_DIM
MLA_W = MLA_HEADS * V_DIM
SGU_W = SGU_GROUPS * SGU_DIM
D_MIX = NA_W + MLA_W + SGU_W
N_EXPERTS = 16
N_GROUPS = 4
EXPERTS_PER_GROUP = N_EXPERTS // N_GROUPS
TOP_K = 2
D_EXPERT = 512
ALPHA = (2 * DEPTH) ** 0.25
LN_EPS = 1e-6
RMS_EPS = 1e-6

T_CTX = BATCH * SEQ
T_LAT = DEC_BATCH * DEC_SEQ
T_ALL = T_CTX + T_LAT
N_COND = 8

C_Q, C_K, C_V = 0, NA_W, 2 * NA_W
C_CQ = 3 * NA_W
C_CKV = C_CQ + Q_LORA
C_KR = C_CKV + KV_LORA
MLA_QK = 256

TM_PROJ = 256
TM_OUT = 512
TM_SGU = 256
TM_TAIL = 256
TM_DISP = 512
TM_EXP = 256
TQ_MLA = 256
NEG_BIG = -1e30
VMEM_LIMIT = 56 * 1024 * 1024


def _cparams(sem, vmem=VMEM_LIMIT):
    return pltpu.CompilerParams(dimension_semantics=sem, vmem_limit_bytes=vmem)


def _cond_row(i, tm):
    t0 = i * tm
    return jnp.where(t0 < T_CTX, 0, 1 + (t0 - T_CTX) // DEC_SEQ)


def _ln(x):
    mu = jnp.mean(x, axis=-1, keepdims=True)
    xc = x - mu
    var = jnp.mean(xc * xc, axis=-1, keepdims=True)
    return xc * lax.rsqrt(var + LN_EPS)


def _rms(x):
    return x * lax.rsqrt(jnp.mean(x * x, axis=-1, keepdims=True) + RMS_EPS)


def _dot(a, b):
    return jnp.dot(a, b, preferred_element_type=F32)


def _dot_nt(a, b):
    return lax.dot_general(a, b, (((1,), (1,)), ((), ())), preferred_element_type=F32)


def _ada_kernel(c_ref, w_ref, b_ref, o_ref):
    c = c_ref[...]
    s = c * jax.nn.sigmoid(c)
    o_ref[...] = _dot(s.astype(BF16), w_ref[...].astype(BF16)) + b_ref[...]


def _ada(cond, w_ada, b_ada):
    tn = 1024
    n = 6 * D_MODEL
    return pl.pallas_call(
        _ada_kernel,
        grid=(DEPTH, n // tn),
        in_specs=[
            pl.BlockSpec((N_COND, D_MODEL), lambda l, j: (0, 0)),
            pl.BlockSpec((None, D_MODEL, tn), lambda l, j: (l, 0, j)),
            pl.BlockSpec((None, 1, tn), lambda l, j: (l, 0, j)),
        ],
        out_specs=pl.BlockSpec((None, N_COND, tn), lambda l, j: (l, 0, j)),
        out_shape=jax.ShapeDtypeStruct((DEPTH, N_COND, n), F32),
        compiler_params=_cparams(("parallel", "parallel")),
        name="ada_mod",
    )(cond, w_ada, b_ada.reshape(DEPTH, 1, n))


def _inproj_kernel(xc_ref, xl_ref, sh_ref, sc_ref, w_ref, wkr_ref, wz_ref, gq_ref, gkv_ref, wuq_ref, wukv_ref, cs_ref,
                   k32_in, v32_in, ckv32_in, kr32_in,
                   q_ref, k_ref, v_ref, k32_ref, v32_ref, qm_ref, km_ref, vv_ref,
                   ckv32_ref, kr32_ref, z_ref):
    del k32_in, v32_in, ckv32_in, kr32_in
    i = pl.program_id(0)
    is_ctx = i * TM_PROJ < T_CTX
    x = jnp.where(is_ctx, xc_ref[...], xl_ref[...])
    h = (_ln(x) * (1.0 + sc_ref[...]) + sh_ref[...]).astype(BF16)

    def proj(a, b):
        return _dot(h, w_ref[:, a:b])

    q_ref[...] = proj(C_Q, C_K).astype(BF16)
    k = proj(C_K, C_V)
    k_ref[...] = k.astype(BF16)
    v = proj(C_V, C_CQ)
    v_ref[...] = v.astype(BF16)

    cs = cs_ref[...]
    lane = lax.broadcasted_iota(jnp.int32, cs.shape, 1)

    def rotate(pair):
        t = pair * cs
        return jnp.where(lane < ROPE_DIM, t + pltpu.roll(t, ROPE_DIM, 1), 0.0)

    cqn = (_rms(proj(C_CQ, C_CKV)) * gq_ref[...]).astype(BF16)
    mq = _dot(cqn, wuq_ref[...])
    for hd in range(MLA_HEADS):
        c0 = hd * MLA_QK
        qm_ref[:, c0:c0 + NOPE_DIM] = mq[:, c0:c0 + NOPE_DIM].astype(BF16)
        qm_ref[:, c0 + NOPE_DIM:c0 + MLA_QK] = rotate(mq[:, c0 + NOPE_DIM:c0 + MLA_QK]).astype(BF16)

    ckvn = _rms(proj(C_CKV, C_KR)) * gkv_ref[...]
    kv = _dot(ckvn.astype(BF16), wukv_ref[...])
    kr2 = _dot(h, wkr_ref[...])
    krot = rotate(kr2).astype(BF16)
    for hd in range(MLA_HEADS):
        c0 = hd * MLA_QK
        km_ref[:, c0:c0 + NOPE_DIM] = kv[:, hd * NOPE_DIM:(hd + 1) * NOPE_DIM].astype(BF16)
        km_ref[:, c0 + NOPE_DIM:c0 + MLA_QK] = krot
    vv_ref[...] = kv[:, MLA_HEADS * NOPE_DIM:].astype(BF16)

    z_ref[...] = _dot(h, wz_ref[...])

    @pl.when(is_ctx)
    def _():
        k32_ref[...] = k
        v32_ref[...] = v
        ckv32_ref[...] = ckvn
        kr32_ref[...] = kr2[:, :ROPE_DIM]


def _inproj(l, x_ctx, x_lat, mods, w_main, w_kr, w_z, g_q, g_kv, w_uq_p, w_ukv_p, cs_tab, new_k, new_v, new_ckv, new_kr):
    tm = TM_PROJ
    assert tm == SEQ
    nct = T_CTX // tm
    per_b = DEC_SEQ // tm

    def row(i):
        return (i, 0)

    def ctx_row(i):
        return (jnp.minimum(i, nct - 1), 0)

    def lat_row(i):
        return (jnp.maximum(i - nct, 0), 0)

    def cs_row(i):
        return (jnp.where(i < nct, 0, 1 + (i - nct) % per_b), 0)

    def mod(kind):
        return pl.BlockSpec((None, None, None, 1, D_MODEL), lambda i: (l, _cond_row(i, tm), kind, 0, 0))

    def const2(shape):
        return pl.BlockSpec(shape, lambda i: (l, 0, 0), pipeline_mode=pl.Buffered(1))

    def cache(w):
        return pl.BlockSpec((None, None, SEQ, w), lambda i: (jnp.minimum(i, nct - 1), l, 0, 0))

    anyspec = pl.BlockSpec(memory_space=pl.ANY)
    out_shapes = [
        jax.ShapeDtypeStruct((T_ALL, NA_W), BF16),
        jax.ShapeDtypeStruct((T_ALL, NA_W), BF16),
        jax.ShapeDtypeStruct((T_ALL, NA_W), BF16),
        jax.ShapeDtypeStruct(new_k.shape, F32),
        jax.ShapeDtypeStruct(new_v.shape, F32),
        jax.ShapeDtypeStruct((T_ALL, MLA_HEADS * MLA_QK), BF16),
        jax.ShapeDtypeStruct((T_ALL, MLA_HEADS * MLA_QK), BF16),
        jax.ShapeDtypeStruct((T_ALL, MLA_W), BF16),
        jax.ShapeDtypeStruct(new_ckv.shape, F32),
        jax.ShapeDtypeStruct(new_kr.shape, F32),
        jax.ShapeDtypeStruct((T_ALL, 2 * SGU_W), F32),
    ]
    out_specs = [
        pl.BlockSpec((tm, NA_W), row), pl.BlockSpec((tm, NA_W), row), pl.BlockSpec((tm, NA_W), row),
        cache(NA_W), cache(NA_W),
        pl.BlockSpec((tm, MLA_HEADS * MLA_QK), row), pl.BlockSpec((tm, MLA_HEADS * MLA_QK), row),
        pl.BlockSpec((tm, MLA_W), row),
        cache(KV_LORA), cache(ROPE_DIM),
        pl.BlockSpec((tm, 2 * SGU_W), row),
    ]
    return pl.pallas_call(
        _inproj_kernel,
        grid=(T_ALL // tm,),
        in_specs=[
            pl.BlockSpec((tm, D_MODEL), ctx_row), pl.BlockSpec((tm, D_MODEL), lat_row),
            mod(0), mod(1),
            const2((None, D_MODEL, C_KR)), const2((None, D_MODEL, 2 * ROPE_DIM)), const2((None, D_MODEL, 2 * SGU_W)),
            pl.BlockSpec((None, 1, Q_LORA), lambda i: (l, 0, 0)),
            pl.BlockSpec((None, 1, KV_LORA), lambda i: (l, 0, 0)),
            const2((None, Q_LORA, MLA_HEADS * MLA_QK)),
            const2((None, KV_LORA, 2 * MLA_W)),
            pl.BlockSpec((tm, 2 * ROPE_DIM), cs_row),
            anyspec, anyspec, anyspec, anyspec,
        ],
        out_specs=out_specs,
        out_shape=out_shapes,
        input_output_aliases={12: 3, 13: 4, 14: 8, 15: 9},
        compiler_params=_cparams(("arbitrary",)),
        name="in_proj",
    )(x_ctx, x_lat, mods, mods, w_main, w_kr, w_z, g_q.reshape(DEPTH, 1, Q_LORA), g_kv.reshape(DEPTH, 1, KV_LORA),
      w_uq_p, w_ukv_p, cs_tab, new_k, new_v, new_ckv, new_kr)


def _ctxkv_kernel(ckv_ref, kr_ref, w_ref, km_ref, vv_ref):
    kv = _dot(ckv_ref[...].astype(BF16), w_ref[...])
    kr = jnp.concatenate([kr_ref[...], jnp.zeros((PAST_LEN, ROPE_DIM), F32)], axis=-1).astype(BF16)
    for hd in range(MLA_HEADS):
        c0 = hd * MLA_QK
        km_ref[:, c0:c0 + NOPE_DIM] = kv[:, hd * NOPE_DIM:(hd + 1) * NOPE_DIM].astype(BF16)
        km_ref[:, c0 + NOPE_DIM:c0 + MLA_QK] = kr
    vv_ref[...] = kv[:, MLA_HEADS * NOPE_DIM:].astype(BF16)


def _ctxkv(cache_ckv, cache_krope, w_ukv_p):
    return pl.pallas_call(
        _ctxkv_kernel,
        grid=(DEPTH, DEC_BATCH),
        in_specs=[
            pl.BlockSpec((None, None, PAST_LEN, KV_LORA), lambda l, b: (b, l, 0, 0)),
            pl.BlockSpec((None, None, PAST_LEN, ROPE_DIM), lambda l, b: (b, l, 0, 0)),
            pl.BlockSpec((None, KV_LORA, 2 * MLA_W), lambda l, b: (l, 0, 0)),
        ],
        out_specs=[
            pl.BlockSpec((None, None, PAST_LEN, MLA_HEADS * MLA_QK), lambda l, b: (l, b, 0, 0)),
            pl.BlockSpec((None, None, PAST_LEN, MLA_W), lambda l, b: (l, b, 0, 0)),
        ],
        out_shape=[
            jax.ShapeDtypeStruct((DEPTH, DEC_BATCH, PAST_LEN, MLA_HEADS * MLA_QK), BF16),
            jax.ShapeDtypeStruct((DEPTH, DEC_BATCH, PAST_LEN, MLA_W), BF16),
        ],
        compiler_params=_cparams(("parallel", "parallel")),
        name="ctx_cache_kv",
    )(cache_ckv, cache_krope, w_ukv_p)


def _softmax_pv(blocks):
    ss = [s for s, _ in blocks]
    m = ss[0].max(axis=-1, keepdims=True)
    for s in ss[1:]:
        m = jnp.maximum(m, s.max(axis=-1, keepdims=True))
    num = None
    den = None
    for s, (_, v) in zip(ss, blocks):
        e = jnp.exp(s - m)
        d = e.sum(axis=-1, keepdims=True)
        o = _dot(e.astype(BF16), v)
        num = o if num is None else num + o
        den = d if den is None else den + d
    return num / den


def _ctx_attn_kernel(q_ref, k_ref, v_ref, qm_ref, km_ref, vv_ref, oa_ref, ob_ref):
    for hd in range(NA_HEADS):
        sl = slice(hd * NA_HEAD_DIM, (hd + 1) * NA_HEAD_DIM)
        s = _dot_nt(q_ref[:, sl], k_ref[:, sl]) * NA_SCALE
        oa_ref[:, sl] = _softmax_pv([(s, v_ref[:, sl])]).astype(BF16)
    for hd in range(MLA_HEADS):
        sq = slice(hd * MLA_QK, (hd + 1) * MLA_QK)
        sv = slice(hd * V_DIM, (hd + 1) * V_DIM)
        s = _dot_nt(qm_ref[:, sq], km_ref[:, sq]) * MLA_SCALE
        ob_ref[:, sv] = _softmax_pv([(s, vv_ref[:, sv])]).astype(BF16)


def _ctx_attn(q, k, v, qm, km, vv):
    def spec(w):
        return pl.BlockSpec((SEQ, w), lambda b: (b, 0))

    return pl.pallas_call(
        _ctx_attn_kernel,
        grid=(BATCH,),
        in_specs=[spec(NA_W), spec(NA_W), spec(NA_W),
                  spec(MLA_HEADS * MLA_QK), spec(MLA_HEADS * MLA_QK), spec(MLA_W)],
        out_specs=[spec(NA_W), spec(MLA_W)],
        out_shape=[jax.ShapeDtypeStruct((T_CTX, NA_W), BF16), jax.ShapeDtypeStruct((T_CTX, MLA_W), BF16)],
        compiler_params=_cparams(("parallel",)),
        name="ctx_attn",
    )(q, k, v, qm, km, vv)


def _lat_mla_kernel(q_ref, k_ref, v_ref, kc_ref, vc_ref, o_ref):
    q = q_ref[...]
    s1 = _dot_nt(q, k_ref[...]) * MLA_SCALE
    s2 = _dot_nt(q, kc_ref[...]) * MLA_SCALE
    o_ref[...] = _softmax_pv([(s1, v_ref[...]), (s2, vc_ref[...])]).astype(BF16)


def _lat_mla(l, qm, km, vv, kmc, vvc):
    tq = TQ_MLA
    nq = DEC_SEQ // tq
    row0 = T_CTX // tq
    kb0 = T_CTX // DEC_SEQ
    return pl.pallas_call(
        _lat_mla_kernel,
        grid=(DEC_BATCH, MLA_HEADS, nq),
        in_specs=[
            pl.BlockSpec((tq, MLA_QK), lambda b, h, t: (row0 + b * nq + t, h)),
            pl.BlockSpec((DEC_SEQ, MLA_QK), lambda b, h, t: (kb0 + b, h)),
            pl.BlockSpec((DEC_SEQ, V_DIM), lambda b, h, t: (kb0 + b, h)),
            pl.BlockSpec((None, None, PAST_LEN, MLA_QK), lambda b, h, t: (l, b, 0, h)),
            pl.BlockSpec((None, None, PAST_LEN, V_DIM), lambda b, h, t: (l, b, 0, h)),
        ],
        out_specs=pl.BlockSpec((tq, V_DIM), lambda b, h, t: (b * nq + t, h)),
        out_shape=jax.ShapeDtypeStruct((T_LAT, MLA_W), BF16),
        compiler_params=_cparams(("parallel", "parallel", "arbitrary")),
        name="lat_mla",
    )(qm, km, vv, kmc, vvc)


def _lat_na_kernel(q_ref, k_ref, v_ref, ck_ref, cv_ref, bias_ref, o_ref):
    ck = ck_ref[...].astype(BF16)
    cv = cv_ref[...].astype(BF16)
    win = NA_ROWS * GRID_W

    def body(r, carry):
        rs = jnp.clip(r - NA_ROWS // 2, 0, GRID_R - NA_ROWS)
        tb = jnp.where(r < NA_ROWS // 2, r, jnp.where(r > GRID_R - NA_ROWS // 2, r - (GRID_R - NA_ROWS), NA_ROWS // 2))
        q0 = pl.multiple_of(r * GRID_W, GRID_W)
        k0 = pl.multiple_of(rs * GRID_W, GRID_W)
        q = q_ref[pl.ds(q0, GRID_W), :]
        kw = k_ref[pl.ds(k0, win), :]
        vw = v_ref[pl.ds(k0, win), :]
        s1 = _dot_nt(q, kw) * NA_SCALE + bias_ref[tb]
        s2 = _dot_nt(q, ck) * NA_SCALE
        o_ref[pl.ds(q0, GRID_W), :] = _softmax_pv([(s1, vw), (s2, cv)]).astype(BF16)
        return carry

    lax.fori_loop(0, GRID_R, body, 0)


def _lat_na(l, q, k, v, cache_k, cache_v, bias_tab):
    kb0 = T_CTX // DEC_SEQ

    def lat(b, h):
        return (kb0 + b, h)

    return pl.pallas_call(
        _lat_na_kernel,
        grid=(DEC_BATCH, NA_HEADS),
        in_specs=[
            pl.BlockSpec((DEC_SEQ, NA_HEAD_DIM), lat),
            pl.BlockSpec((DEC_SEQ, NA_HEAD_DIM), lat),
            pl.BlockSpec((DEC_SEQ, NA_HEAD_DIM), lat),
            pl.BlockSpec((None, None, PAST_LEN, NA_HEAD_DIM), lambda b, h: (b, l, 0, h)),
            pl.BlockSpec((None, None, PAST_LEN, NA_HEAD_DIM), lambda b, h: (b, l, 0, h)),
            pl.BlockSpec((None, None, NA_ROWS, GRID_W, NA_ROWS * GRID_W), lambda b, h: (l, h, 0, 0, 0)),
        ],
        out_specs=pl.BlockSpec((DEC_SEQ, NA_HEAD_DIM), lambda b, h: (b, h)),
        out_shape=jax.ShapeDtypeStruct((T_LAT, NA_W), BF16),
        compiler_params=_cparams(("parallel", "parallel")),
        name="lat_na",
    )(q, k, v, cache_k, cache_v, bias_tab)


def _sgu_kernel(z_ref, g_ref, b_ref, w_ref, bs_ref, o_ref):
    z = z_ref[...]
    a = 0.5 * z * (1.0 + lax.erf(z * (0.5 ** 0.5)))
    for c in range(TM_SGU // CHUNK):
        rows = slice(c * CHUNK, (c + 1) * CHUNK)
        for g in range(SGU_GROUPS):
            u = a[rows, g * SGU_DIM:(g + 1) * SGU_DIM]
            v = a[rows, SGU_W + g * SGU_DIM:SGU_W + (g + 1) * SGU_DIM]
            vn = _ln(v) * g_ref[g:g + 1, :] + b_ref[g:g + 1, :]
            t = _dot(w_ref[g].astype(BF16), vn.astype(BF16)) + bs_ref[g]
            o_ref[rows, g * SGU_DIM:(g + 1) * SGU_DIM] = (u * t).astype(BF16)


def _sgu(l, z, ln_g, ln_b, w_s, b_s):
    tm = TM_SGU
    return pl.pallas_call(
        _sgu_kernel,
        grid=(T_ALL // tm,),
        in_specs=[
            pl.BlockSpec((tm, 2 * SGU_W), lambda i: (i, 0)),
            pl.BlockSpec((None, SGU_GROUPS, SGU_DIM), lambda i: (l, 0, 0)),
            pl.BlockSpec((None, SGU_GROUPS, SGU_DIM), lambda i: (l, 0, 0)),
            pl.BlockSpec((None, SGU_GROUPS, CHUNK, CHUNK), lambda i: (l, 0, 0, 0)),
            pl.BlockSpec((None, SGU_GROUPS, CHUNK, 1), lambda i: (l, 0, 0, 0)),
        ],
        out_specs=pl.BlockSpec((tm, SGU_W), lambda i: (i, 0)),
        out_shape=jax.ShapeDtypeStruct((T_ALL, SGU_W), BF16),
        compiler_params=_cparams(("parallel",)),
        name="sgu",
    )(z, ln_g, ln_b, w_s, b_s.reshape(DEPTH, SGU_GROUPS, CHUNK, 1))


def _route_rows(logits_t, bias_col):
    s = jax.nn.sigmoid(logits_t)
    sb = s + bias_col
    rows = [sb[r:r + 1, :] for r in range(N_EXPERTS)]
    srows = [s[r:r + 1, :] for r in range(N_EXPERTS)]
    best, gsel = None, None
    for g in range(N_GROUPS):
        m = rows[g * EXPERTS_PER_GROUP:(g + 1) * EXPERTS_PER_GROUP]
        score = None
        for a in range(EXPERTS_PER_GROUP):
            for b in range(a + 1, EXPERTS_PER_GROUP):
                pair = m[a] + m[b]
                score = pair if score is None else jnp.maximum(score, pair)
        if best is None:
            best, gsel = score, jnp.zeros(score.shape, jnp.int32)
        else:
            upd = score > best
            best = jnp.where(upd, score, best)
            gsel = jnp.where(upd, g, gsel)

    def pick(table, j):
        out = table[j]
        for g in range(1, N_GROUPS):
            out = jnp.where(gsel == g, table[g * EXPERTS_PER_GROUP + j], out)
        return out

    v = [pick(rows, j) for j in range(EXPERTS_PER_GROUP)]
    sv = [pick(srows, j) for j in range(EXPERTS_PER_GROUP)]
    m1, i1, w1 = v[0], jnp.zeros(gsel.shape, jnp.int32), sv[0]
    for j in range(1, EXPERTS_PER_GROUP):
        upd = v[j] > m1
        m1 = jnp.where(upd, v[j], m1)
        i1 = jnp.where(upd, j, i1)
        w1 = jnp.where(upd, sv[j], w1)
    m2 = jnp.full(gsel.shape, -jnp.inf, F32)
    i2 = jnp.zeros(gsel.shape, jnp.int32)
    w2 = jnp.zeros(gsel.shape, F32)
    for j in range(EXPERTS_PER_GROUP):
        upd = jnp.logical_and(i1 != j, v[j] > m2)
        m2 = jnp.where(upd, v[j], m2)
        i2 = jnp.where(upd, j, i2)
        w2 = jnp.where(upd, sv[j], w2)
    den = w1 + w2
    return gsel * EXPERTS_PER_GROUP + i1, gsel * EXPERTS_PER_GROUP + i2, w1 / den, w2 / den


def _outproj_kernel(ac_ref, al_ref, bc_ref, bl_ref, c_ref, wa_ref, wb_ref, wc_ref, xc_ref, xl_ref, g1_ref, sh2_ref, sc2_ref,
                    lg_ref, lb_ref, wrt_ref, br_ref, x1_ref, hp_ref, e_ref, rank_ref, gate_ref, cnt_ref, base_ref):
    tm = TM_OUT
    is_ctx = pl.program_id(0) * tm < T_CTX
    a = jnp.where(is_ctx, ac_ref[...], al_ref[...])
    b = jnp.where(is_ctx, bc_ref[...], bl_ref[...])
    x = jnp.where(is_ctx, xc_ref[...], xl_ref[...])

    @pl.when(pl.program_id(0) == 0)
    def _():
        base_ref[...] = jnp.zeros_like(base_ref)

    mixed = _dot(a, wa_ref[...]) + _dot(b, wb_ref[...]) + _dot(c_ref[...], wc_ref[...])
    x1 = _ln(ALPHA * x + g1_ref[...] * mixed) * lg_ref[...] + lb_ref[...]
    x1_ref[...] = x1
    h2 = (_ln(x1) * (1.0 + sc2_ref[...]) + sh2_ref[...]).astype(BF16)

    hf = h2.astype(F32)
    hi = pltpu.bitcast(hf[:, :D_MODEL // 2], jnp.uint32)
    lo = pltpu.bitcast(hf[:, D_MODEL // 2:], jnp.uint32)
    hp_ref[...] = hi | (lo >> 16)

    e1, e2, w1, w2 = _route_rows(_dot_nt(wrt_ref[...], h2), br_ref[...])
    e_ref[0:1, :] = e1
    e_ref[1:2, :] = e2
    gate_ref[0:1, :] = w1
    gate_ref[1:2, :] = w2

    sub = lax.broadcasted_iota(jnp.int32, (N_EXPERTS, tm), 0)
    o1 = (sub == e1).astype(F32)
    o2 = (sub == e2).astype(F32)
    before = (lax.broadcasted_iota(jnp.int32, (tm, tm), 0) < lax.broadcasted_iota(jnp.int32, (tm, tm), 1)).astype(BF16)
    p1 = _dot(o1.astype(BF16), before)
    p2 = _dot(o2.astype(BF16), before)
    c1 = o1.sum(axis=1, keepdims=True)
    c2 = o2.sum(axis=1, keepdims=True)
    base = base_ref[:, 0:1]
    rank_ref[0:1, :] = jnp.sum(o1 * (base + p1), axis=0, keepdims=True).astype(jnp.int32)
    rank_ref[1:2, :] = jnp.sum(o2 * (base + c1 + p2), axis=0, keepdims=True).astype(jnp.int32)
    base_ref[...] = base_ref[...] + (c1 + c2)
    cnt_ref[...] = base_ref[...]


def _outproj(l, oa_c, oa_l, ob_c, ob_l, oc, w_out_b, x_ctx, x_lat, mods, ln_g, ln_b, w_router_t, b_router_col):
    tm = TM_OUT
    nct = T_CTX // tm

    def row(i):
        return (i, 0)

    def ctx_row(i):
        return (jnp.minimum(i, nct - 1), 0)

    def lat_row(i):
        return (jnp.maximum(i - nct, 0), 0)

    def tok(i):
        return (0, i)

    def mod(kind):
        return pl.BlockSpec((None, None, None, 1, D_MODEL), lambda i: (l, _cond_row(i, tm), kind, 0, 0))

    def wblk(rows, blk):
        return pl.BlockSpec((None, rows, D_MODEL), lambda i: (l, blk, 0), pipeline_mode=pl.Buffered(1))

    vec = pl.BlockSpec((None, 1, D_MODEL), lambda i: (l, 0, 0))
    return pl.pallas_call(
        _outproj_kernel,
        grid=(T_ALL // tm,),
        in_specs=[
            pl.BlockSpec((tm, NA_W), ctx_row), pl.BlockSpec((tm, NA_W), lat_row),
            pl.BlockSpec((tm, MLA_W), ctx_row), pl.BlockSpec((tm, MLA_W), lat_row),
            pl.BlockSpec((tm, SGU_W), row),
            wblk(NA_W, 0), wblk(MLA_W, 1), wblk(SGU_W, (NA_W + MLA_W) // SGU_W),
            pl.BlockSpec((tm, D_MODEL), ctx_row), pl.BlockSpec((tm, D_MODEL), lat_row),
            mod(2), mod(3), mod(4), vec, vec,
            pl.BlockSpec((N_EXPERTS, D_MODEL), lambda i: (0, 0)),
            pl.BlockSpec((N_EXPERTS, 1), lambda i: (0, 0)),
        ],
        out_specs=[pl.BlockSpec((tm, D_MODEL), row), pl.BlockSpec((tm, D_MODEL // 2), row),
                   pl.BlockSpec((TOP_K, tm), tok), pl.BlockSpec((TOP_K, tm), tok), pl.BlockSpec((TOP_K, tm), tok),
                   pl.BlockSpec((N_EXPERTS, 128), lambda i: (0, 0))],
        out_shape=[jax.ShapeDtypeStruct((T_ALL, D_MODEL), F32),
                   jax.ShapeDtypeStruct((T_ALL, D_MODEL // 2), jnp.uint32),
                   jax.ShapeDtypeStruct((TOP_K, T_ALL), jnp.int32),
                   jax.ShapeDtypeStruct((TOP_K, T_ALL), jnp.int32),
                   jax.ShapeDtypeStruct((TOP_K, T_ALL), F32),
                   jax.ShapeDtypeStruct((N_EXPERTS, 128), F32)],
        scratch_shapes=[pltpu.VMEM((N_EXPERTS, 128), F32)],
        compiler_params=_cparams(("arbitrary",)),
        name="out_proj",
    )(oa_c, oa_l, ob_c, ob_l, oc, w_out_b, w_out_b, w_out_b, x_ctx, x_lat, mods, mods, mods,
      ln_g.reshape(DEPTH, 1, D_MODEL), ln_b.reshape(DEPTH, 1, D_MODEL), w_router_t, b_router_col)


def _dispatch_plan(e, rank, counts, max_tiles):
    tiles = (counts + TM_EXP - 1) // TM_EXP
    tile_end = jnp.cumsum(tiles)
    off = (tile_end - tiles) * TM_EXP
    n_tiles = tile_end[-1]
    ids = jnp.arange(N_EXPERTS, dtype=jnp.int32)
    pos = rank + jnp.sum(jnp.where(e[..., None] == ids, off, 0), axis=-1)
    ti = jnp.arange(max_tiles, dtype=jnp.int32)
    owner = jnp.sum((tile_end[None, :] <= ti[:, None]).astype(jnp.int32), axis=1)
    last = jnp.sum((tile_end <= n_tiles - 1).astype(jnp.int32))
    tile_expert = jnp.where(ti < n_tiles, owner, last).astype(jnp.int32)
    return pos.astype(jnp.int32), tile_expert, n_tiles.reshape(1).astype(jnp.int32)


def _row_copy(src_ref, src_row, dst_ref, dst_row, sem):
    return pltpu.make_async_copy(src_ref.at[pl.ds(src_row, 1)], dst_ref.at[pl.ds(dst_row, 1)], sem)


def _dispatch_kernel(pos_ref, hp_ref, zeros_ref, xs_ref, sem):
    del zeros_ref

    def issue(t, carry):
        for k in range(TOP_K):
            _row_copy(hp_ref, t, xs_ref, pos_ref[k, t], sem).start()
        return carry

    lax.fori_loop(0, TM_DISP, issue, 0, unroll=8)

    def drain(t, carry):
        for k in range(TOP_K):
            _row_copy(hp_ref, 0, xs_ref, 0, sem).wait()
        return carry

    lax.fori_loop(0, TM_DISP, drain, 0, unroll=8)


def _dispatch(pos, hp, xs_zero):
    return pl.pallas_call(
        _dispatch_kernel,
        grid=(T_ALL // TM_DISP,),
        in_specs=[
            pl.BlockSpec((TOP_K, TM_DISP), lambda i: (0, i), memory_space=pltpu.SMEM),
            pl.BlockSpec((TM_DISP, D_MODEL // 2), lambda i: (i, 0)),
            pl.BlockSpec(memory_space=pl.ANY),
        ],
        out_specs=pl.BlockSpec(memory_space=pl.ANY),
        out_shape=jax.ShapeDtypeStruct(xs_zero.shape, xs_zero.dtype),
        input_output_aliases={2: 0},
        scratch_shapes=[pltpu.SemaphoreType.DMA(())],
        compiler_params=_cparams(("arbitrary",)),
        name="dispatch",
    )(pos, hp, xs_zero)


def _expert_kernel(te_ref, nt_ref, x_ref, wg_ref, wu_ref, wd_ref, y_ref, wg_s, wu_s, wd_s):
    i = pl.program_id(0)
    prev = te_ref[jnp.maximum(i - 1, 0)]
    fresh = jnp.logical_or(i == 0, te_ref[i] != prev)

    @pl.when(fresh)
    def _():
        wg_s[...] = wg_ref[...].astype(BF16)
        wu_s[...] = wu_ref[...].astype(BF16)
        wd_s[...] = wd_ref[...].astype(BF16)

    @pl.when(i < nt_ref[0])
    def _():
        xp = x_ref[...]
        hi = pltpu.bitcast(xp & jnp.uint32(0xFFFF0000), F32)
        lo = pltpu.bitcast(xp << 16, F32)
        x = jnp.concatenate([hi, lo], axis=-1).astype(BF16)
        g = _dot(x, wg_s[...])
        u = _dot(x, wu_s[...])
        a = (g * jax.nn.sigmoid(g) * u).astype(BF16)
        y_ref[...] = _dot(a, wd_s[...])

    @pl.when(i >= nt_ref[0])
    def _():
        y_ref[...] = jnp.zeros_like(y_ref)


def _experts(l, tile_expert, n_tiles, xs, w_gate, w_up, w_down, max_tiles):
    tm = TM_EXP
    grid_spec = pltpu.PrefetchScalarGridSpec(
        num_scalar_prefetch=2,
        grid=(max_tiles,),
        in_specs=[
            pl.BlockSpec((tm, D_MODEL // 2), lambda i, te, nt: (i, 0)),
            pl.BlockSpec((None, None, D_MODEL, D_EXPERT), lambda i, te, nt: (l, te[i], 0, 0)),
            pl.BlockSpec((None, None, D_MODEL, D_EXPERT), lambda i, te, nt: (l, te[i], 0, 0)),
            pl.BlockSpec((None, None, D_EXPERT, D_MODEL), lambda i, te, nt: (l, te[i], 0, 0)),
        ],
        out_specs=pl.BlockSpec((tm, D_MODEL), lambda i, te, nt: (i, 0)),
        scratch_shapes=[pltpu.VMEM((D_MODEL, D_EXPERT), BF16), pltpu.VMEM((D_MODEL, D_EXPERT), BF16),
                        pltpu.VMEM((D_EXPERT, D_MODEL), BF16)],
    )
    return pl.pallas_call(
        _expert_kernel,
        grid_spec=grid_spec,
        out_shape=jax.ShapeDtypeStruct((max_tiles * tm, D_MODEL), F32),
        compiler_params=_cparams(("arbitrary",)),
        name="experts",
    )(tile_expert, n_tiles, xs, w_gate, w_up, w_down)


def _tail_kernel(pos_ref, posn_ref, x1_ref, gate_ref, g2_ref, lg_ref, lb_ref, ys_ref, o_ref, ybuf, sem):
    tm = TM_TAIL
    i = pl.program_id(0)
    slot = i % 2

    def issue(p_ref, s):
        def body(t, carry):
            for k in range(TOP_K):
                _row_copy(ys_ref, p_ref[k, t], ybuf.at[s, k], t, sem.at[s]).start()
            return carry

        lax.fori_loop(0, tm, body, 0, unroll=8)

    @pl.when(i == 0)
    def _():
        issue(pos_ref, 0)

    @pl.when(i + 1 < pl.num_programs(0))
    def _():
        issue(posn_ref, 1 - slot)

    def drain(t, carry):
        for k in range(TOP_K):
            _row_copy(ys_ref, 0, ybuf.at[slot, k], 0, sem.at[slot]).wait()
        return carry

    lax.fori_loop(0, tm, drain, 0, unroll=8)

    gate = gate_ref[...]
    y = gate[:, 0:1] * ybuf[slot, 0] + gate[:, 1:2] * ybuf[slot, 1]
    o_ref[...] = _ln(ALPHA * x1_ref[...] + g2_ref[...] * y) * lg_ref[...] + lb_ref[...]


def _tail(l, t_start, t_count, pos, x1, gate_t, mods, ln_g, ln_b, ys):
    tm = TM_TAIL
    n = t_count // tm
    i0 = t_start // tm
    vec = pl.BlockSpec((None, 1, D_MODEL), lambda i: (l, 0, 0))
    return pl.pallas_call(
        _tail_kernel,
        grid=(n,),
        in_specs=[
            pl.BlockSpec((TOP_K, tm), lambda i: (0, i0 + i), memory_space=pltpu.SMEM),
            pl.BlockSpec((TOP_K, tm), lambda i: (0, i0 + jnp.minimum(i + 1, n - 1)), memory_space=pltpu.SMEM),
            pl.BlockSpec((tm, D_MODEL), lambda i: (i0 + i, 0)),
            pl.BlockSpec((tm, TOP_K), lambda i: (i0 + i, 0)),
            pl.BlockSpec((None, None, None, 1, D_MODEL), lambda i: (l, _cond_row(i0 + i, tm), 5, 0, 0)),
            vec, vec,
            pl.BlockSpec(memory_space=pl.ANY),
        ],
        out_specs=pl.BlockSpec((tm, D_MODEL), lambda i: (i, 0)),
        out_shape=jax.ShapeDtypeStruct((t_count, D_MODEL), F32),
        scratch_shapes=[pltpu.VMEM((2, TOP_K, tm, D_MODEL), F32), pltpu.SemaphoreType.DMA((2,))],
        compiler_params=_cparams(("arbitrary",)),
        name="tail",
    )(pos, pos, x1, gate_t, mods, ln_g.reshape(DEPTH, 1, D_MODEL), ln_b.reshape(DEPTH, 1, D_MODEL), ys)


def _swap_partners(w):
    nf = ROPE_DIM // 4
    return jnp.concatenate([w[..., nf:2 * nf], w[..., :nf], w[..., 3 * nf:], w[..., 2 * nf:3 * nf]], axis=-1)


def _prep_w_in(w_in):
    kr = w_in[..., C_KR:C_KR + ROPE_DIM]
    w_kr = jnp.concatenate([kr, _swap_partners(kr)], axis=-1).astype(BF16)
    return w_in[..., :C_KR].astype(BF16), w_kr, w_in[..., C_KR + ROPE_DIM:].astype(BF16)


def _prep_w_uq(w_uq):
    w = w_uq.reshape(DEPTH, Q_LORA, MLA_HEADS, NOPE_DIM + ROPE_DIM)
    return jnp.concatenate([w, _swap_partners(w[..., NOPE_DIM:])], axis=-1).reshape(
        DEPTH, Q_LORA, MLA_HEADS * MLA_QK).astype(BF16)


def _prep_w_ukv(w_ukv):
    w = w_ukv.reshape(DEPTH, KV_LORA, MLA_HEADS, NOPE_DIM + V_DIM)
    return jnp.concatenate([w[..., :NOPE_DIM].reshape(DEPTH, KV_LORA, MLA_W),
                            w[..., NOPE_DIM:].reshape(DEPTH, KV_LORA, MLA_W)], axis=-1).astype(BF16)


def _rope_table(tm):
    half = ROPE_DIM // 2
    nf = half // 2
    t = jnp.arange(DEC_SEQ)
    inv = ROPE_BASE ** (-jnp.arange(nf, dtype=F32) * 2.0 / half)
    ar = (t // GRID_W).astype(F32)[:, None] * inv[None, :]
    ac = (t % GRID_W).astype(F32)[:, None] * inv[None, :]
    cos = jnp.concatenate([jnp.cos(ar), jnp.cos(ar), jnp.cos(ac), jnp.cos(ac)], axis=-1)
    sin = jnp.concatenate([-jnp.sin(ar), jnp.sin(ar), -jnp.sin(ac), jnp.sin(ac)], axis=-1)
    ident = jnp.concatenate([jnp.ones((tm, ROPE_DIM), F32), jnp.zeros((tm, ROPE_DIM), F32)], axis=-1)
    return jnp.concatenate([ident, jnp.concatenate([cos, sin], axis=-1)], axis=0)


def _na_bias_table(na_rpb):
    reps = list(range(NA_ROWS // 2)) + [NA_ROWS // 2] + list(range(GRID_R - NA_ROWS // 2 + 1, GRID_R))
    rows = np.array(reps)
    row_start = np.clip(rows - NA_ROWS // 2, 0, GRID_R - NA_ROWS)
    d_row = row_start[:, None] + np.arange(NA_ROWS)[None, :] - rows[:, None] + NA_ROWS - 1
    cols = np.arange(GRID_W)
    col_start = np.clip(cols - NA_COLS // 2, 0, GRID_W - NA_COLS)
    col_ok = (cols[None, :] >= col_start[:, None]) & (cols[None, :] < col_start[:, None] + NA_COLS)
    d_col = np.clip(cols[None, :] - cols[:, None] + NA_COLS - 1, 0, 2 * NA_COLS - 2)
    pick_row = (d_row[:, :, None] == np.arange(2 * NA_ROWS - 1)[None, None, :]).astype(np.float32)
    pick_col = (d_col[:, :, None] == np.arange(2 * NA_COLS - 1)[None, None, :]).astype(np.float32)
    by_row = jnp.einsum("lhrc,ajr->lhajc", na_rpb, pick_row, precision=lax.Precision.HIGHEST)
    bias = jnp.einsum("lhajc,qwc->lhaqjw", by_row, pick_col, precision=lax.Precision.HIGHEST)
    bias = jnp.where(col_ok[None, None, None, :, None, :], bias, NEG_BIG)
    return bias.reshape(DEPTH, NA_HEADS, NA_ROWS, GRID_W, NA_ROWS * GRID_W)


def kernel(x_prompt, x_sample, cache_na_k, cache_na_v, cache_mla_ckv, cache_mla_krope, c, c_ctx, w_ada, b_ada,
           w_in, g_q, g_kv, w_uq, w_ukv, na_rpb, sgu_ln_g, sgu_ln_b, sgu_w, sgu_b, w_out, ln1_g, ln1_b, ln2_g,
           ln2_b, w_router, b_router, w_gate, w_up, w_down):
    max_tiles = (T_ALL * TOP_K) // TM_EXP + N_EXPERTS

    cond = jnp.zeros((N_COND, D_MODEL), F32).at[0].set(c_ctx).at[1:1 + DEC_BATCH].set(c)
    mods = _ada(cond, w_ada, b_ada).reshape(DEPTH, N_COND, 6, 1, D_MODEL)

    w_main, w_kr, w_z = _prep_w_in(w_in)
    w_uq_p = _prep_w_uq(w_uq)
    w_ukv_p = _prep_w_ukv(w_ukv)
    w_out_b = w_out.astype(BF16)
    w_router_t = w_router.T.astype(BF16)
    b_router_col = b_router.reshape(N_EXPERTS, 1).astype(F32)
    cs_tab = _rope_table(TM_PROJ)
    bias_tab = _na_bias_table(na_rpb)
    cache_k = cache_na_k.reshape(DEC_BATCH, DEPTH, PAST_LEN, NA_W)
    cache_v = cache_na_v.reshape(DEC_BATCH, DEPTH, PAST_LEN, NA_W)
    kmc, vvc = _ctxkv(cache_mla_ckv, cache_mla_krope, w_ukv_p)

    new_k = jnp.zeros((BATCH, DEPTH, SEQ, NA_W), F32)
    new_v = jnp.zeros((BATCH, DEPTH, SEQ, NA_W), F32)
    new_ckv = jnp.zeros((BATCH, DEPTH, SEQ, KV_LORA), F32)
    new_kr = jnp.zeros((BATCH, DEPTH, SEQ, ROPE_DIM), F32)

    x_ctx = x_prompt.reshape(T_CTX, D_MODEL)
    x_lat = x_sample.reshape(T_LAT, D_MODEL)
    for l in range(DEPTH):
        q, k, v, new_k, new_v, qm, km, vv, new_ckv, new_kr, z = _inproj(
            l, x_ctx, x_lat, mods, w_main, w_kr, w_z, g_q, g_kv, w_uq_p, w_ukv_p, cs_tab, new_k, new_v, new_ckv, new_kr)

        oa_c, ob_c = _ctx_attn(q, k, v, qm, km, vv)
        oa_l = _lat_na(l, q, k, v, cache_k, cache_v, bias_tab)
        ob_l = _lat_mla(l, qm, km, vv, kmc, vvc)
        oc = _sgu(l, z, sgu_ln_g, sgu_ln_b, sgu_w, sgu_b)

        x1, hp, e, rank, gate, cnt = _outproj(l, oa_c, oa_l, ob_c, ob_l, oc, w_out_b, x_ctx, x_lat, mods, ln1_g, ln1_b,
                                              w_router_t, b_router_col)
        pos, tile_expert, n_tiles = _dispatch_plan(e, rank, cnt[:, 0].astype(jnp.int32), max_tiles)
        xs = _dispatch(pos, hp, jnp.zeros((max_tiles * TM_EXP, D_MODEL // 2), jnp.uint32))
        ys = _experts(l, tile_expert, n_tiles, xs, w_gate, w_up, w_down, max_tiles)
        gate_t = gate.T
        x_ctx = _tail(l, 0, T_CTX, pos, x1, gate_t, mods, ln2_g, ln2_b, ys)
        x_lat = _tail(l, T_CTX, T_LAT, pos, x1, gate_t, mods, ln2_g, ln2_b, ys)

    return (x_ctx.reshape(BATCH, SEQ, D_MODEL), x_lat.reshape(DEC_BATCH, DEC_SEQ, D_MODEL),
            new_k.reshape(BATCH, DEPTH, SEQ, NA_HEADS, NA_HEAD_DIM), new_v.reshape(BATCH, DEPTH, SEQ, NA_HEADS, NA_HEAD_DIM),
            new_ckv, new_kr)
```

```python
import functools

import numpy as np
import jax
import jax.numpy as jnp
from jax import lax
from jax.experimental import pallas as pl
from jax.experimental.pallas import tpu as pltpu

F32 = jnp.float32
BF16 = jnp.bfloat16

D_MODEL = 2048
BATCH = 32
SEQ = 256
DEPTH = 2
DEC_BATCH = 2
DEC_SEQ = 2048
PAST_LEN = 256
GRID_W = 64
GRID_R = DEC_SEQ // GRID_W
NA_HEADS = 6
NA_HEAD_DIM = 128
NA_ROWS = 8
NA_COLS = 16
MLA_HEADS = 6
Q_LORA = 512
KV_LORA = 512
NOPE_DIM = 128
ROPE_DIM = 64
V_DIM = 128
ROPE_BASE = 10000.0
MLA_SCALE = (NOPE_DIM + ROPE_DIM) ** -0.5
NA_SCALE = NA_HEAD_DIM ** -0.5
SGU_GROUPS = 4
SGU_DIM = 128
CHUNK = 128
NA_W = NA_HEADS * NA_HEAD_DIM
MLA_W = MLA_HEADS * V_DIM
SGU_W = SGU_GROUPS * SGU_DIM
D_MIX = NA_W + MLA_W + SGU_W
N_EXPERTS = 16
N_GROUPS = 4
EXPERTS_PER_GROUP = N_EXPERTS // N_GROUPS
TOP_K = 2
D_EXPERT = 512
ALPHA = (2 * DEPTH) ** 0.25
LN_EPS = 1e-6
RMS_EPS = 1e-6

T_CTX = BATCH * SEQ
T_LAT = DEC_BATCH * DEC_SEQ
T_ALL = T_CTX + T_LAT
N_COND = 8

C_Q, C_K, C_V = 0, NA_W, 2 * NA_W
C_CQ = 3 * NA_W
C_CKV = C_CQ + Q_LORA
C_KR = C_CKV + KV_LORA
MLA_QK = 256

TM_PROJ = 256
TM_OUT = 512
TM_SGU = 256
TM_TAIL = 256
TM_DISP = 512
TM_EXP = 256
TQ_MLA = 512
NA_QROWS = 4
NA_KROWS = NA_QROWS + NA_ROWS
NEG_BIG = -1e30
VMEM_LIMIT = 56 * 1024 * 1024


def _cparams(sem, vmem=VMEM_LIMIT):
    return pltpu.CompilerParams(dimension_semantics=sem, vmem_limit_bytes=vmem)


def _cond_row(i, tm):
    t0 = i * tm
    return jnp.where(t0 < T_CTX, 0, 1 + (t0 - T_CTX) // DEC_SEQ)


def _ln(x):
    mu = jnp.mean(x, axis=-1, keepdims=True)
    xc = x - mu
    var = jnp.mean(xc * xc, axis=-1, keepdims=True)
    return xc * lax.rsqrt(var + LN_EPS)


def _rms(x):
    return x * lax.rsqrt(jnp.mean(x * x, axis=-1, keepdims=True) + RMS_EPS)


def _dot(a, b):
    return jnp.dot(a, b, preferred_element_type=F32)


def _dot_nt(a, b):
    return lax.dot_general(a, b, (((1,), (1,)), ((), ())), preferred_element_type=F32)


def _ada_kernel(c_ref, w_ref, b_ref, o_ref):
    c = c_ref[...]
    s = c * jax.nn.sigmoid(c)
    o_ref[...] = _dot(s.astype(BF16), w_ref[...].astype(BF16)) + b_ref[...]


def _ada(cond, w_ada, b_ada):
    tn = 1024
    n = 6 * D_MODEL
    return pl.pallas_call(
        _ada_kernel,
        grid=(DEPTH, n // tn),
        in_specs=[
            pl.BlockSpec((N_COND, D_MODEL), lambda l, j: (0, 0)),
            pl.BlockSpec((None, D_MODEL, tn), lambda l, j: (l, 0, j)),
            pl.BlockSpec((None, 1, tn), lambda l, j: (l, 0, j)),
        ],
        out_specs=pl.BlockSpec((None, N_COND, tn), lambda l, j: (l, 0, j)),
        out_shape=jax.ShapeDtypeStruct((DEPTH, N_COND, n), F32),
        compiler_params=_cparams(("parallel", "parallel")),
        name="ada_mod",
    )(cond, w_ada, b_ada.reshape(DEPTH, 1, n))


def _inproj_kernel(xc_ref, xl_ref, sh_ref, sc_ref, w_ref, wkr_ref, wz_ref, gq_ref, gkv_ref, wuq_ref, wukv_ref, cs_ref,
                   k32_in, v32_in, ckv32_in, kr32_in,
                   q_ref, k_ref, v_ref, k32_ref, v32_ref, qm_ref, km_ref, vv_ref,
                   ckv32_ref, kr32_ref, z_ref):
    del k32_in, v32_in, ckv32_in, kr32_in
    i = pl.program_id(0)
    is_ctx = i * TM_PROJ < T_CTX
    x = jnp.where(is_ctx, xc_ref[...], xl_ref[...])
    h = (_ln(x) * (1.0 + sc_ref[...]) + sh_ref[...]).astype(BF16)

    def proj(a, b):
        return _dot(h, w_ref[:, a:b])

    q_ref[...] = proj(C_Q, C_K).astype(BF16)
    k = proj(C_K, C_V)
    k_ref[...] = k.astype(BF16)
    v = proj(C_V, C_CQ)
    v_ref[...] = v.astype(BF16)

    cs = cs_ref[...]
    lane = lax.broadcasted_iota(jnp.int32, cs.shape, 1)

    def rotate(pair):
        t = pair * cs
        return jnp.where(lane < ROPE_DIM, t + pltpu.roll(t, ROPE_DIM, 1), 0.0)

    cqn = (_rms(proj(C_CQ, C_CKV)) * gq_ref[...]).astype(BF16)
    mq = _dot(cqn, wuq_ref[...])
    for hd in range(MLA_HEADS):
        c0 = hd * MLA_QK
        qm_ref[:, c0:c0 + NOPE_DIM] = mq[:, c0:c0 + NOPE_DIM].astype(BF16)
        qm_ref[:, c0 + NOPE_DIM:c0 + MLA_QK] = rotate(mq[:, c0 + NOPE_DIM:c0 + MLA_QK]).astype(BF16)

    ckvn = _rms(proj(C_CKV, C_KR)) * gkv_ref[...]
    kv = _dot(ckvn.astype(BF16), wukv_ref[...])
    kr2 = _dot(h, wkr_ref[...])
    krot = rotate(kr2).astype(BF16)
    for hd in range(MLA_HEADS):
        c0 = hd * MLA_QK
        km_ref[:, c0:c0 + NOPE_DIM] = kv[:, hd * NOPE_DIM:(hd + 1) * NOPE_DIM].astype(BF16)
        km_ref[:, c0 + NOPE_DIM:c0 + MLA_QK] = krot
    vv_ref[...] = kv[:, MLA_HEADS * NOPE_DIM:].astype(BF16)

    z_ref[...] = _dot(h, wz_ref[...])

    @pl.when(is_ctx)
    def _():
        k32_ref[...] = k
        v32_ref[...] = v
        ckv32_ref[...] = ckvn
        kr32_ref[...] = kr2[:, :ROPE_DIM]


def _inproj(l, x_ctx, x_lat, mods, w_main, w_kr, w_z, g_q, g_kv, w_uq_p, w_ukv_p, cs_tab, new_k, new_v, new_ckv, new_kr):
    tm = TM_PROJ
    assert tm == SEQ
    nct = T_CTX // tm
    per_b = DEC_SEQ // tm

    def row(i):
        return (i, 0)

    def ctx_row(i):
        return (jnp.minimum(i, nct - 1), 0)

    def lat_row(i):
        return (jnp.maximum(i - nct, 0), 0)

    def cs_row(i):
        return (jnp.where(i < nct, 0, 1 + (i - nct) % per_b), 0)

    def mod(kind):
        return pl.BlockSpec((None, None, None, 1, D_MODEL), lambda i: (l, _cond_row(i, tm), kind, 0, 0))

    def const2(shape):
        return pl.BlockSpec(shape, lambda i: (l, 0, 0), pipeline_mode=pl.Buffered(1))

    def cache(w):
        return pl.BlockSpec((None, None, SEQ, w), lambda i: (jnp.minimum(i, nct - 1), l, 0, 0))

    anyspec = pl.BlockSpec(memory_space=pl.ANY)
    out_shapes = [
        jax.ShapeDtypeStruct((T_ALL, NA_W), BF16),
        jax.ShapeDtypeStruct((T_ALL, NA_W), BF16),
        jax.ShapeDtypeStruct((T_ALL, NA_W), BF16),
        jax.ShapeDtypeStruct(new_k.shape, F32),
        jax.ShapeDtypeStruct(new_v.shape, F32),
        jax.ShapeDtypeStruct((T_ALL, MLA_HEADS * MLA_QK), BF16),
        jax.ShapeDtypeStruct((T_ALL, MLA_HEADS * MLA_QK), BF16),
        jax.ShapeDtypeStruct((T_ALL, MLA_W), BF16),
        jax.ShapeDtypeStruct(new_ckv.shape, F32),
        jax.ShapeDtypeStruct(new_kr.shape, F32),
        jax.ShapeDtypeStruct((T_ALL, 2 * SGU_W), F32),
    ]
    out_specs = [
        pl.BlockSpec((tm, NA_W), row), pl.BlockSpec((tm, NA_W), row), pl.BlockSpec((tm, NA_W), row),
        cache(NA_W), cache(NA_W),
        pl.BlockSpec((tm, MLA_HEADS * MLA_QK), row), pl.BlockSpec((tm, MLA_HEADS * MLA_QK), row),
        pl.BlockSpec((tm, MLA_W), row),
        cache(KV_LORA), cache(ROPE_DIM),
        pl.BlockSpec((tm, 2 * SGU_W), row),
    ]
    return pl.pallas_call(
        _inproj_kernel,
        grid=(T_ALL // tm,),
        in_specs=[
            pl.BlockSpec((tm, D_MODEL), ctx_row), pl.BlockSpec((tm, D_MODEL), lat_row),
            mod(0), mod(1),
            const2((None, D_MODEL, C_KR)), const2((None, D_MODEL, 2 * ROPE_DIM)), const2((None, D_MODEL, 2 * SGU_W)),
            pl.BlockSpec((None, 1, Q_LORA), lambda i: (l, 0, 0)),
            pl.BlockSpec((None, 1, KV_LORA), lambda i: (l, 0, 0)),
            const2((None, Q_LORA, MLA_HEADS * MLA_QK)),
            const2((None, KV_LORA, 2 * MLA_W)),
            pl.BlockSpec((tm, 2 * ROPE_DIM), cs_row),
            anyspec, anyspec, anyspec, anyspec,
        ],
        out_specs=out_specs,
        out_shape=out_shapes,
        input_output_aliases={12: 3, 13: 4, 14: 8, 15: 9},
        compiler_params=_cparams(("arbitrary",)),
        name="in_proj",
    )(x_ctx, x_lat, mods, mods, w_main, w_kr, w_z, g_q.reshape(DEPTH, 1, Q_LORA), g_kv.reshape(DEPTH, 1, KV_LORA),
      w_uq_p, w_ukv_p, cs_tab, new_k, new_v, new_ckv, new_kr)


def _ctxkv_kernel(ckv_ref, kr_ref, w_ref, km_ref, vv_ref):
    kv = _dot(ckv_ref[...].astype(BF16), w_ref[...])
    kr = jnp.concatenate([kr_ref[...], jnp.zeros((PAST_LEN, ROPE_DIM), F32)], axis=-1).astype(BF16)
    for hd in range(MLA_HEADS):
        c0 = hd * MLA_QK
        km_ref[:, c0:c0 + NOPE_DIM] = kv[:, hd * NOPE_DIM:(hd + 1) * NOPE_DIM].astype(BF16)
        km_ref[:, c0 + NOPE_DIM:c0 + MLA_QK] = kr
    vv_ref[...] = kv[:, MLA_HEADS * NOPE_DIM:].astype(BF16)


def _ctxkv(cache_ckv, cache_krope, w_ukv_p):
    return pl.pallas_call(
        _ctxkv_kernel,
        grid=(DEPTH, DEC_BATCH),
        in_specs=[
            pl.BlockSpec((None, None, PAST_LEN, KV_LORA), lambda l, b: (b, l, 0, 0)),
            pl.BlockSpec((None, None, PAST_LEN, ROPE_DIM), lambda l, b: (b, l, 0, 0)),
            pl.BlockSpec((None, KV_LORA, 2 * MLA_W), lambda l, b: (l, 0, 0)),
        ],
        out_specs=[
            pl.BlockSpec((None, None, PAST_LEN, MLA_HEADS * MLA_QK), lambda l, b: (l, b, 0, 0)),
            pl.BlockSpec((None, None, PAST_LEN, MLA_W), lambda l, b: (l, b, 0, 0)),
        ],
        out_shape=[
            jax.ShapeDtypeStruct((DEPTH, DEC_BATCH, PAST_LEN, MLA_HEADS * MLA_QK), BF16),
            jax.ShapeDtypeStruct((DEPTH, DEC_BATCH, PAST_LEN, MLA_W), BF16),
        ],
        compiler_params=_cparams(("parallel", "parallel")),
        name="ctx_cache_kv",
    )(cache_ckv, cache_krope, w_ukv_p)


def _softmax_pv(blocks):
    ss = [s for s, _ in blocks]
    m = ss[0].max(axis=-1, keepdims=True)
    for s in ss[1:]:
        m = jnp.maximum(m, s.max(axis=-1, keepdims=True))
    num = None
    den = None
    for s, (_, v) in zip(ss, blocks):
        e = jnp.exp(s - m)
        d = e.sum(axis=-1, keepdims=True)
        o = _dot(e.astype(BF16), v)
        num = o if num is None else num + o
        den = d if den is None else den + d
    return num / den


def _ctx_attn_kernel(q_ref, k_ref, v_ref, qm_ref, km_ref, vv_ref, oa_ref, ob_ref):
    for hd in range(NA_HEADS):
        sl = slice(hd * NA_HEAD_DIM, (hd + 1) * NA_HEAD_DIM)
        s = _dot_nt(q_ref[:, sl], k_ref[:, sl]) * NA_SCALE
        oa_ref[:, sl] = _softmax_pv([(s, v_ref[:, sl])]).astype(BF16)
    for hd in range(MLA_HEADS):
        sq = slice(hd * MLA_QK, (hd + 1) * MLA_QK)
        sv = slice(hd * V_DIM, (hd + 1) * V_DIM)
        s = _dot_nt(qm_ref[:, sq], km_ref[:, sq]) * MLA_SCALE
        ob_ref[:, sv] = _softmax_pv([(s, vv_ref[:, sv])]).astype(BF16)


def _ctx_attn(q, k, v, qm, km, vv):
    def spec(w):
        return pl.BlockSpec((SEQ, w), lambda b: (b, 0))

    return pl.pallas_call(
        _ctx_attn_kernel,
        grid=(BATCH,),
        in_specs=[spec(NA_W), spec(NA_W), spec(NA_W),
                  spec(MLA_HEADS * MLA_QK), spec(MLA_HEADS * MLA_QK), spec(MLA_W)],
        out_specs=[spec(NA_W), spec(MLA_W)],
        out_shape=[jax.ShapeDtypeStruct((T_CTX, NA_W), BF16), jax.ShapeDtypeStruct((T_CTX, MLA_W), BF16)],
        compiler_params=_cparams(("parallel",)),
        name="ctx_attn",
    )(q, k, v, qm, km, vv)


def _lat_mla_kernel(q_ref, k_ref, v_ref, kc_ref, vc_ref, o_ref):
    q = q_ref[...]
    s1 = _dot_nt(q, k_ref[...]) * MLA_SCALE
    s2 = _dot_nt(q, kc_ref[...]) * MLA_SCALE
    o_ref[...] = _softmax_pv([(s1, v_ref[...]), (s2, vc_ref[...])]).astype(BF16)


def _lat_mla(l, qm, km, vv, kmc, vvc):
    tq = TQ_MLA
    nq = DEC_SEQ // tq
    row0 = T_CTX // tq
    kb0 = T_CTX // DEC_SEQ
    return pl.pallas_call(
        _lat_mla_kernel,
        grid=(DEC_BATCH, MLA_HEADS, nq),
        in_specs=[
            pl.BlockSpec((tq, MLA_QK), lambda b, h, t: (row0 + b * nq + t, h)),
            pl.BlockSpec((DEC_SEQ, MLA_QK), lambda b, h, t: (kb0 + b, h)),
            pl.BlockSpec((DEC_SEQ, V_DIM), lambda b, h, t: (kb0 + b, h)),
            pl.BlockSpec((None, None, PAST_LEN, MLA_QK), lambda b, h, t: (l, b, 0, h)),
            pl.BlockSpec((None, None, PAST_LEN, V_DIM), lambda b, h, t: (l, b, 0, h)),
        ],
        out_specs=pl.BlockSpec((tq, V_DIM), lambda b, h, t: (b * nq + t, h)),
        out_shape=jax.ShapeDtypeStruct((T_LAT, MLA_W), BF16),
        compiler_params=_cparams(("parallel", "parallel", "arbitrary")),
        name="lat_mla",
    )(qm, km, vv, kmc, vvc)


def _na_block_plan():
    plan = []
    for r0 in range(0, GRID_R, NA_QROWS):
        ws = min(max(r0 - NA_ROWS // 2, 0), GRID_R - NA_KROWS)
        rows = []
        for r in range(r0, r0 + NA_QROWS):
            rs = min(max(r - NA_ROWS // 2, 0), GRID_R - NA_ROWS)
            rel = [kr - r + NA_ROWS - 1 if rs <= kr < rs + NA_ROWS else None for kr in range(ws, ws + NA_KROWS)]
            rows.append([(rel[j], rel[j + 1]) for j in range(0, NA_KROWS, 2)])
        plan.append((ws, rows))
    return plan


def _lat_na_kernel(q_ref, k_ref, v_ref, ck_ref, cv_ref, tab_ref, o_ref):
    ck = ck_ref[...].astype(BF16)
    cv = cv_ref[...].astype(BF16)
    nq = NA_QROWS * GRID_W
    nk = NA_KROWS * GRID_W
    lane = lax.broadcasted_iota(jnp.int32, (GRID_W, 2 * GRID_W), 1)
    masked = jnp.full((GRID_W, 2 * GRID_W), NEG_BIG, F32)

    def pair_bias(d0, d1):
        if d0 is None and d1 is None:
            return masked
        if d1 is None:
            return jnp.where(lane < GRID_W, tab_ref[d0], NEG_BIG)
        if d0 is None:
            return jnp.where(lane >= GRID_W, tab_ref[d1 - 1], NEG_BIG)
        return tab_ref[d0]

    for blk, (ws, rows) in enumerate(_na_block_plan()):
        q = q_ref[blk * nq:(blk + 1) * nq, :]
        kw = k_ref[ws * GRID_W:ws * GRID_W + nk, :]
        vw = v_ref[ws * GRID_W:ws * GRID_W + nk, :]
        bias = jnp.concatenate(
            [jnp.concatenate([pair_bias(d0, d1) for d0, d1 in pairs], axis=1) for pairs in rows], axis=0)
        s1 = _dot_nt(q, kw) * NA_SCALE + bias
        s2 = _dot_nt(q, ck) * NA_SCALE
        o_ref[blk * nq:(blk + 1) * nq, :] = _softmax_pv([(s1, vw), (s2, cv)]).astype(BF16)


def _lat_na(l, q, k, v, cache_k, cache_v, bias_tab):
    kb0 = T_CTX // DEC_SEQ

    def lat(b, h):
        return (kb0 + b, h)

    return pl.pallas_call(
        _lat_na_kernel,
        grid=(DEC_BATCH, NA_HEADS),
        in_specs=[
            pl.BlockSpec((DEC_SEQ, NA_HEAD_DIM), lat),
            pl.BlockSpec((DEC_SEQ, NA_HEAD_DIM), lat),
            pl.BlockSpec((DEC_SEQ, NA_HEAD_DIM), lat),
            pl.BlockSpec((None, None, PAST_LEN, NA_HEAD_DIM), lambda b, h: (b, l, 0, h)),
            pl.BlockSpec((None, None, PAST_LEN, NA_HEAD_DIM), lambda b, h: (b, l, 0, h)),
            pl.BlockSpec((None, None, 2 * NA_ROWS - 1, GRID_W, 2 * GRID_W), lambda b, h: (l, h, 0, 0, 0)),
        ],
        out_specs=pl.BlockSpec((DEC_SEQ, NA_HEAD_DIM), lambda b, h: (b, h)),
        out_shape=jax.ShapeDtypeStruct((T_LAT, NA_W), BF16),
        compiler_params=_cparams(("parallel", "parallel")),
        name="lat_na",
    )(q, k, v, cache_k, cache_v, bias_tab)


def _sgu_kernel(z_ref, g_ref, b_ref, w_ref, bs_ref, o_ref):
    z = z_ref[...]
    a = 0.5 * z * (1.0 + lax.erf(z * (0.5 ** 0.5)))
    for c in range(TM_SGU // CHUNK):
        rows = slice(c * CHUNK, (c + 1) * CHUNK)
        for g in range(SGU_GROUPS):
            u = a[rows, g * SGU_DIM:(g + 1) * SGU_DIM]
            v = a[rows, SGU_W + g * SGU_DIM:SGU_W + (g + 1) * SGU_DIM]
            vn = _ln(v) * g_ref[g:g + 1, :] + b_ref[g:g + 1, :]
            t = _dot(w_ref[g].astype(BF16), vn.astype(BF16)) + bs_ref[g]
            o_ref[rows, g * SGU_DIM:(g + 1) * SGU_DIM] = (u * t).astype(BF16)


def _sgu(l, z, ln_g, ln_b, w_s, b_s):
    tm = TM_SGU
    return pl.pallas_call(
        _sgu_kernel,
        grid=(T_ALL // tm,),
        in_specs=[
            pl.BlockSpec((tm, 2 * SGU_W), lambda i: (i, 0)),
            pl.BlockSpec((None, SGU_GROUPS, SGU_DIM), lambda i: (l, 0, 0)),
            pl.BlockSpec((None, SGU_GROUPS, SGU_DIM), lambda i: (l, 0, 0)),
            pl.BlockSpec((None, SGU_GROUPS, CHUNK, CHUNK), lambda i: (l, 0, 0, 0)),
            pl.BlockSpec((None, SGU_GROUPS, CHUNK, 1), lambda i: (l, 0, 0, 0)),
        ],
        out_specs=pl.BlockSpec((tm, SGU_W), lambda i: (i, 0)),
        out_shape=jax.ShapeDtypeStruct((T_ALL, SGU_W), BF16),
        compiler_params=_cparams(("parallel",)),
        name="sgu",
    )(z, ln_g, ln_b, w_s, b_s.reshape(DEPTH, SGU_GROUPS, CHUNK, 1))


def _route_rows(logits_t, bias_col):
    s = jax.nn.sigmoid(logits_t)
    sb = s + bias_col
    rows = [sb[r:r + 1, :] for r in range(N_EXPERTS)]
    srows = [s[r:r + 1, :] for r in range(N_EXPERTS)]
    best, gsel = None, None
    for g in range(N_GROUPS):
        m = rows[g * EXPERTS_PER_GROUP:(g + 1) * EXPERTS_PER_GROUP]
        score = None
        for a in range(EXPERTS_PER_GROUP):
            for b in range(a + 1, EXPERTS_PER_GROUP):
                pair = m[a] + m[b]
                score = pair if score is None else jnp.maximum(score, pair)
        if best is None:
            best, gsel = score, jnp.zeros(score.shape, jnp.int32)
        else:
            upd = score > best
            best = jnp.where(upd, score, best)
            gsel = jnp.where(upd, g, gsel)

    def pick(table, j):
        out = table[j]
        for g in range(1, N_GROUPS):
            out = jnp.where(gsel == g, table[g * EXPERTS_PER_GROUP + j], out)
        return out

    v = [pick(rows, j) for j in range(EXPERTS_PER_GROUP)]
    sv = [pick(srows, j) for j in range(EXPERTS_PER_GROUP)]
    m1, i1, w1 = v[0], jnp.zeros(gsel.shape, jnp.int32), sv[0]
    for j in range(1, EXPERTS_PER_GROUP):
        upd = v[j] > m1
        m1 = jnp.where(upd, v[j], m1)
        i1 = jnp.where(upd, j, i1)
        w1 = jnp.where(upd, sv[j], w1)
    m2 = jnp.full(gsel.shape, -jnp.inf, F32)
    i2 = jnp.zeros(gsel.shape, jnp.int32)
    w2 = jnp.zeros(gsel.shape, F32)
    for j in range(EXPERTS_PER_GROUP):
        upd = jnp.logical_and(i1 != j, v[j] > m2)
        m2 = jnp.where(upd, v[j], m2)
        i2 = jnp.where(upd, j, i2)
        w2 = jnp.where(upd, sv[j], w2)
    den = w1 + w2
    return gsel * EXPERTS_PER_GROUP + i1, gsel * EXPERTS_PER_GROUP + i2, w1 / den, w2 / den


def _outproj_kernel(ac_ref, al_ref, bc_ref, bl_ref, c_ref, wa_ref, wb_ref, wc_ref, xc_ref, xl_ref, g1_ref, sh2_ref, sc2_ref,
                    lg_ref, lb_ref, wrt_ref, br_ref, x1_ref, hp_ref, e_ref, rank_ref, gate_ref, cnt_ref, base_ref):
    tm = TM_OUT
    is_ctx = pl.program_id(0) * tm < T_CTX
    a = jnp.where(is_ctx, ac_ref[...], al_ref[...])
    b = jnp.where(is_ctx, bc_ref[...], bl_ref[...])
    x = jnp.where(is_ctx, xc_ref[...], xl_ref[...])

    @pl.when(pl.program_id(0) == 0)
    def _():
        base_ref[...] = jnp.zeros_like(base_ref)

    mixed = _dot(a, wa_ref[...]) + _dot(b, wb_ref[...]) + _dot(c_ref[...], wc_ref[...])
    x1 = _ln(ALPHA * x + g1_ref[...] * mixed) * lg_ref[...] + lb_ref[...]
    x1_ref[...] = x1
    h2 = (_ln(x1) * (1.0 + sc2_ref[...]) + sh2_ref[...]).astype(BF16)

    hf = h2.astype(F32)
    hi = pltpu.bitcast(hf[:, :D_MODEL // 2], jnp.uint32)
    lo = pltpu.bitcast(hf[:, D_MODEL // 2:], jnp.uint32)
    hp_ref[...] = hi | (lo >> 16)

    e1, e2, w1, w2 = _route_rows(_dot_nt(wrt_ref[...], h2), br_ref[...])
    e_ref[0:1, :] = e1
    e_ref[1:2, :] = e2
    gate_ref[0:1, :] = w1
    gate_ref[1:2, :] = w2

    sub = lax.broadcasted_iota(jnp.int32, (N_EXPERTS, tm), 0)
    o1 = (sub == e1).astype(F32)
    o2 = (sub == e2).astype(F32)
    before = (lax.broadcasted_iota(jnp.int32, (tm, tm), 0) < lax.broadcasted_iota(jnp.int32, (tm, tm), 1)).astype(BF16)
    p1 = _dot(o1.astype(BF16), before)
    p2 = _dot(o2.astype(BF16), before)
    c1 = o1.sum(axis=1, keepdims=True)
    c2 = o2.sum(axis=1, keepdims=True)
    base = base_ref[:, 0:1]
    rank_ref[0:1, :] = jnp.sum(o1 * (base + p1), axis=0, keepdims=True).astype(jnp.int32)
    rank_ref[1:2, :] = jnp.sum(o2 * (base + c1 + p2), axis=0, keepdims=True).astype(jnp.int32)
    base_ref[...] = base_ref[...] + (c1 + c2)
    cnt_ref[...] = base_ref[...]


def _outproj(l, oa_c, oa_l, ob_c, ob_l, oc, w_out_b, x_ctx, x_lat, mods, ln_g, ln_b, w_router_t, b_router_col):
    tm = TM_OUT
    nct = T_CTX // tm

    def row(i):
        return (i, 0)

    def ctx_row(i):
        return (jnp.minimum(i, nct - 1), 0)

    def lat_row(i):
        return (jnp.maximum(i - nct, 0), 0)

    def tok(i):
        return (0, i)

    def mod(kind):
        return pl.BlockSpec((None, None, None, 1, D_MODEL), lambda i: (l, _cond_row(i, tm), kind, 0, 0))

    def wblk(rows, blk):
        return pl.BlockSpec((None, rows, D_MODEL), lambda i: (l, blk, 0), pipeline_mode=pl.Buffered(1))

    vec = pl.BlockSpec((None, 1, D_MODEL), lambda i: (l, 0, 0))
    return pl.pallas_call(
        _outproj_kernel,
        grid=(T_ALL // tm,),
        in_specs=[
            pl.BlockSpec((tm, NA_W), ctx_row), pl.BlockSpec((tm, NA_W), lat_row),
            pl.BlockSpec((tm, MLA_W), ctx_row), pl.BlockSpec((tm, MLA_W), lat_row),
            pl.BlockSpec((tm, SGU_W), row),
            wblk(NA_W, 0), wblk(MLA_W, 1), wblk(SGU_W, (NA_W + MLA_W) // SGU_W),
            pl.BlockSpec((tm, D_MODEL), ctx_row), pl.BlockSpec((tm, D_MODEL), lat_row),
            mod(2), mod(3), mod(4), vec, vec,
            pl.BlockSpec((N_EXPERTS, D_MODEL), lambda i: (0, 0)),
            pl.BlockSpec((N_EXPERTS, 1), lambda i: (0, 0)),
        ],
        out_specs=[pl.BlockSpec((tm, D_MODEL), row), pl.BlockSpec((tm, D_MODEL // 2), row),
                   pl.BlockSpec((TOP_K, tm), tok), pl.BlockSpec((TOP_K, tm), tok), pl.BlockSpec((TOP_K, tm), tok),
                   pl.BlockSpec((N_EXPERTS, 128), lambda i: (0, 0))],
        out_shape=[jax.ShapeDtypeStruct((T_ALL, D_MODEL), F32),
                   jax.ShapeDtypeStruct((T_ALL, D_MODEL // 2), jnp.uint32),
                   jax.ShapeDtypeStruct((TOP_K, T_ALL), jnp.int32),
                   jax.ShapeDtypeStruct((TOP_K, T_ALL), jnp.int32),
                   jax.ShapeDtypeStruct((TOP_K, T_ALL), F32),
                   jax.ShapeDtypeStruct((N_EXPERTS, 128), F32)],
        scratch_shapes=[pltpu.VMEM((N_EXPERTS, 128), F32)],
        compiler_params=_cparams(("arbitrary",)),
        name="out_proj",
    )(oa_c, oa_l, ob_c, ob_l, oc, w_out_b, w_out_b, w_out_b, x_ctx, x_lat, mods, mods, mods,
      ln_g.reshape(DEPTH, 1, D_MODEL), ln_b.reshape(DEPTH, 1, D_MODEL), w_router_t, b_router_col)


def _dispatch_plan(e, rank, counts, max_tiles):
    tiles = (counts + TM_EXP - 1) // TM_EXP
    tile_end = jnp.cumsum(tiles)
    off = (tile_end - tiles) * TM_EXP
    n_tiles = tile_end[-1]
    ids = jnp.arange(N_EXPERTS, dtype=jnp.int32)
    pos = rank + jnp.sum(jnp.where(e[..., None] == ids, off, 0), axis=-1)
    ti = jnp.arange(max_tiles, dtype=jnp.int32)
    owner = jnp.sum((tile_end[None, :] <= ti[:, None]).astype(jnp.int32), axis=1)
    last = jnp.sum((tile_end <= n_tiles - 1).astype(jnp.int32))
    tile_expert = jnp.where(ti < n_tiles, owner, last).astype(jnp.int32)
    later_used = jnp.logical_and(ids[None, :] > ids[:, None], tiles[None, :] > 0)
    next_expert = jnp.min(jnp.where(later_used, ids[None, :], N_EXPERTS), axis=1)
    next_expert = jnp.where(next_expert < N_EXPERTS, next_expert, -1).astype(jnp.int32)
    return pos.astype(jnp.int32), tile_expert, next_expert, n_tiles.reshape(1).astype(jnp.int32)


def _row_copy(src_ref, src_row, dst_ref, dst_row, sem):
    return pltpu.make_async_copy(src_ref.at[pl.ds(src_row, 1)], dst_ref.at[pl.ds(dst_row, 1)], sem)


def _dispatch_kernel(p0_ref, p1_ref, hp_ref, zeros_ref, xs_ref, sem):
    del zeros_ref

    def issue(t, carry):
        for p_ref in (p0_ref, p1_ref):
            _row_copy(hp_ref, t, xs_ref, p_ref[t], sem).start()
        return carry

    lax.fori_loop(0, TM_DISP, issue, 0, unroll=8)

    def drain(t, carry):
        for k in range(TOP_K):
            _row_copy(hp_ref, 0, xs_ref, 0, sem).wait()
        return carry

    lax.fori_loop(0, TM_DISP, drain, 0, unroll=8)


def _dispatch(pos0, pos1, hp, xs_zero):
    return pl.pallas_call(
        _dispatch_kernel,
        grid=(T_ALL // TM_DISP,),
        in_specs=[
            pl.BlockSpec((TM_DISP,), lambda i: (i,), memory_space=pltpu.SMEM),
            pl.BlockSpec((TM_DISP,), lambda i: (i,), memory_space=pltpu.SMEM),
            pl.BlockSpec((TM_DISP, D_MODEL // 2), lambda i: (i, 0)),
            pl.BlockSpec(memory_space=pl.ANY),
        ],
        out_specs=pl.BlockSpec(memory_space=pl.ANY),
        out_shape=jax.ShapeDtypeStruct(xs_zero.shape, xs_zero.dtype),
        input_output_aliases={3: 0},
        scratch_shapes=[pltpu.SemaphoreType.DMA(())],
        compiler_params=_cparams(("arbitrary",)),
        name="dispatch",
    )(pos0, pos1, hp, xs_zero)


def _expert_kernel(te_ref, nxt_ref, nt_ref, x_ref, wg_hbm, wu_hbm, wd_hbm, y_ref,
                   wg_f, wu_f, wd_f, wg_s, wu_s, wd_s, switches, sem, *, layer):
    i = pl.program_id(0)
    e = te_ref[i]

    def weight_copies(expert, slot):
        return [pltpu.make_async_copy(w_hbm.at[layer, expert], buf.at[slot], sem.at[slot, j])
                for j, (w_hbm, buf) in enumerate(((wg_hbm, wg_f), (wu_hbm, wu_f), (wd_hbm, wd_f)))]

    @pl.when(i == 0)
    def _():
        for c in weight_copies(e, 0):
            c.start()

    fresh = jnp.logical_or(i == 0, e != te_ref[jnp.maximum(i - 1, 0)])

    @pl.when(fresh)
    def _():
        n_sw = jnp.where(i == 0, 0, switches[0] + 1)
        switches[0] = n_sw
        slot = n_sw % 2
        for c in weight_copies(e, slot):
            c.wait()
        wg_s[...] = wg_f[slot].astype(BF16)
        wu_s[...] = wu_f[slot].astype(BF16)
        wd_s[...] = wd_f[slot].astype(BF16)
        nx = nxt_ref[e]

        @pl.when(nx >= 0)
        def _():
            for c in weight_copies(nx, 1 - slot):
                c.start()

    @pl.when(i < nt_ref[0])
    def _():
        xp = x_ref[...]
        hi = pltpu.bitcast(xp & jnp.uint32(0xFFFF0000), F32)
        lo = pltpu.bitcast(xp << 16, F32)
        x = jnp.concatenate([hi, lo], axis=-1).astype(BF16)
        g = _dot(x, wg_s[...])
        u = _dot(x, wu_s[...])
        a = (g * jax.nn.sigmoid(g) * u).astype(BF16)
        y_ref[...] = _dot(a, wd_s[...])

    @pl.when(i >= nt_ref[0])
    def _():
        y_ref[...] = jnp.zeros_like(y_ref)


def _experts(l, tile_expert, next_expert, n_tiles, xs, w_gate, w_up, w_down, max_tiles):
    tm = TM_EXP
    anyspec = pl.BlockSpec(memory_space=pl.ANY)
    grid_spec = pltpu.PrefetchScalarGridSpec(
        num_scalar_prefetch=3,
        grid=(max_tiles,),
        in_specs=[pl.BlockSpec((tm, D_MODEL // 2), lambda i, te, nx, nt: (i, 0)), anyspec, anyspec, anyspec],
        out_specs=pl.BlockSpec((tm, D_MODEL), lambda i, te, nx, nt: (i, 0)),
        scratch_shapes=[pltpu.VMEM((2, D_MODEL, D_EXPERT), F32), pltpu.VMEM((2, D_MODEL, D_EXPERT), F32),
                        pltpu.VMEM((2, D_EXPERT, D_MODEL), F32),
                        pltpu.VMEM((D_MODEL, D_EXPERT), BF16), pltpu.VMEM((D_MODEL, D_EXPERT), BF16),
                        pltpu.VMEM((D_EXPERT, D_MODEL), BF16),
                        pltpu.SMEM((1,), jnp.int32), pltpu.SemaphoreType.DMA((2, 3))],
    )
    return pl.pallas_call(
        functools.partial(_expert_kernel, layer=l),
        grid_spec=grid_spec,
        out_shape=jax.ShapeDtypeStruct((max_tiles * tm, D_MODEL), F32),
        compiler_params=_cparams(("arbitrary",)),
        name="experts",
    )(tile_expert, next_expert, n_tiles, xs, w_gate, w_up, w_down)


def _tail_kernel(p0_ref, p1_ref, p0n_ref, p1n_ref, x1_ref, gate_ref, g2_ref, lg_ref, lb_ref, ys_ref, o_ref, ybuf, sem,
                 *, n_tiles):
    tm = TM_TAIL
    i = pl.program_id(0)
    slot = i % 2

    def issue(p_refs, s):
        def body(t, carry):
            for k, p_ref in enumerate(p_refs):
                _row_copy(ys_ref, p_ref[t], ybuf.at[s, k], t, sem.at[s]).start()
            return carry

        lax.fori_loop(0, tm, body, 0, unroll=8)

    @pl.when(i == 0)
    def _():
        issue((p0_ref, p1_ref), 0)

    @pl.when(i + 1 < n_tiles)
    def _():
        issue((p0n_ref, p1n_ref), 1 - slot)

    def drain(t, carry):
        for k in range(TOP_K):
            _row_copy(ys_ref, 0, ybuf.at[slot, k], 0, sem.at[slot]).wait()
        return carry

    lax.fori_loop(0, tm, drain, 0, unroll=8)

    gate = gate_ref[...]
    y = gate[:, 0:1] * ybuf[slot, 0] + gate[:, 1:2] * ybuf[slot, 1]
    o_ref[...] = _ln(ALPHA * x1_ref[...] + g2_ref[...] * y) * lg_ref[...] + lb_ref[...]


def _tail(l, t_start, t_count, pos0, pos1, x1, gate_t, mods, ln_g, ln_b, ys):
    tm = TM_TAIL
    n = t_count // tm
    i0 = t_start // tm
    vec = pl.BlockSpec((None, 1, D_MODEL), lambda i: (l, 0, 0))
    return pl.pallas_call(
        functools.partial(_tail_kernel, n_tiles=n),
        grid=(n,),
        in_specs=[
            pl.BlockSpec((tm,), lambda i: (i0 + i,), memory_space=pltpu.SMEM),
            pl.BlockSpec((tm,), lambda i: (i0 + i,), memory_space=pltpu.SMEM),
            pl.BlockSpec((tm,), lambda i: (i0 + jnp.minimum(i + 1, n - 1),), memory_space=pltpu.SMEM),
            pl.BlockSpec((tm,), lambda i: (i0 + jnp.minimum(i + 1, n - 1),), memory_space=pltpu.SMEM),
            pl.BlockSpec((tm, D_MODEL), lambda i: (i0 + i, 0)),
            pl.BlockSpec((tm, TOP_K), lambda i: (i0 + i, 0)),
            pl.BlockSpec((None, None, None, 1, D_MODEL), lambda i: (l, _cond_row(i0 + i, tm), 5, 0, 0)),
            vec, vec,
            pl.BlockSpec(memory_space=pl.ANY),
        ],
        out_specs=pl.BlockSpec((tm, D_MODEL), lambda i: (i, 0)),
        out_shape=jax.ShapeDtypeStruct((t_count, D_MODEL), F32),
        scratch_shapes=[pltpu.VMEM((2, TOP_K, tm, D_MODEL), F32), pltpu.SemaphoreType.DMA((2,))],
        compiler_params=_cparams(("arbitrary",)),
        name="tail",
    )(pos0, pos1, pos0, pos1, x1, gate_t, mods, ln_g.reshape(DEPTH, 1, D_MODEL), ln_b.reshape(DEPTH, 1, D_MODEL), ys)


def _swap_partners(w):
    nf = ROPE_DIM // 4
    return jnp.concatenate([w[..., nf:2 * nf], w[..., :nf], w[..., 3 * nf:], w[..., 2 * nf:3 * nf]], axis=-1)


def _prep_w_in(w_in):
    kr = w_in[..., C_KR:C_KR + ROPE_DIM]
    w_kr = jnp.concatenate([kr, _swap_partners(kr)], axis=-1).astype(BF16)
    return w_in[..., :C_KR].astype(BF16), w_kr, w_in[..., C_KR + ROPE_DIM:].astype(BF16)


def _prep_w_uq(w_uq):
    w = w_uq.reshape(DEPTH, Q_LORA, MLA_HEADS, NOPE_DIM + ROPE_DIM)
    return jnp.concatenate([w, _swap_partners(w[..., NOPE_DIM:])], axis=-1).reshape(
        DEPTH, Q_LORA, MLA_HEADS * MLA_QK).astype(BF16)


def _prep_w_ukv(w_ukv):
    w = w_ukv.reshape(DEPTH, KV_LORA, MLA_HEADS, NOPE_DIM + V_DIM)
    return jnp.concatenate([w[..., :NOPE_DIM].reshape(DEPTH, KV_LORA, MLA_W),
                            w[..., NOPE_DIM:].reshape(DEPTH, KV_LORA, MLA_W)], axis=-1).astype(BF16)


def _rope_table(tm):
    half = ROPE_DIM // 2
    nf = half // 2
    t = jnp.arange(DEC_SEQ)
    inv = ROPE_BASE ** (-jnp.arange(nf, dtype=F32) * 2.0 / half)
    ar = (t // GRID_W).astype(F32)[:, None] * inv[None, :]
    ac = (t % GRID_W).astype(F32)[:, None] * inv[None, :]
    cos = jnp.concatenate([jnp.cos(ar), jnp.cos(ar), jnp.cos(ac), jnp.cos(ac)], axis=-1)
    sin = jnp.concatenate([-jnp.sin(ar), jnp.sin(ar), -jnp.sin(ac), jnp.sin(ac)], axis=-1)
    ident = jnp.concatenate([jnp.ones((tm, ROPE_DIM), F32), jnp.zeros((tm, ROPE_DIM), F32)], axis=-1)
    return jnp.concatenate([ident, jnp.concatenate([cos, sin], axis=-1)], axis=0)


def _na_bias_table(na_rpb):
    cols = np.arange(GRID_W)
    col_start = np.clip(cols - NA_COLS // 2, 0, GRID_W - NA_COLS)
    col_ok = (cols[None, :] >= col_start[:, None]) & (cols[None, :] < col_start[:, None] + NA_COLS)
    d_col = np.clip(cols[None, :] - cols[:, None] + NA_COLS - 1, 0, 2 * NA_COLS - 2)
    pick_col = (d_col[:, :, None] == np.arange(2 * NA_COLS - 1)[None, None, :]).astype(np.float32)
    tab = jnp.einsum("lhrc,qwc->lhrqw", na_rpb, pick_col, precision=lax.Precision.HIGHEST)
    tab = jnp.where(col_ok[None, None, None], tab, NEG_BIG)
    nxt = jnp.concatenate([tab[:, :, 1:], jnp.full_like(tab[:, :, :1], NEG_BIG)], axis=2)
    return jnp.concatenate([tab, nxt], axis=-1)


def kernel(x_prompt, x_sample, cache_na_k, cache_na_v, cache_mla_ckv, cache_mla_krope, c, c_ctx, w_ada, b_ada,
           w_in, g_q, g_kv, w_uq, w_ukv, na_rpb, sgu_ln_g, sgu_ln_b, sgu_w, sgu_b, w_out, ln1_g, ln1_b, ln2_g,
           ln2_b, w_router, b_router, w_gate, w_up, w_down):
    max_tiles = (T_ALL * TOP_K) // TM_EXP + N_EXPERTS

    cond = jnp.zeros((N_COND, D_MODEL), F32).at[0].set(c_ctx).at[1:1 + DEC_BATCH].set(c)
    mods = _ada(cond, w_ada, b_ada).reshape(DEPTH, N_COND, 6, 1, D_MODEL)

    w_main, w_kr, w_z = _prep_w_in(w_in)
    w_uq_p = _prep_w_uq(w_uq)
    w_ukv_p = _prep_w_ukv(w_ukv)
    w_out_b = w_out.astype(BF16)
    w_router_t = w_router.T.astype(BF16)
    b_router_col = b_router.reshape(N_EXPERTS, 1).astype(F32)
    cs_tab = _rope_table(TM_PROJ)
    bias_tab = _na_bias_table(na_rpb)
    cache_k = cache_na_k.reshape(DEC_BATCH, DEPTH, PAST_LEN, NA_W)
    cache_v = cache_na_v.reshape(DEC_BATCH, DEPTH, PAST_LEN, NA_W)
    kmc, vvc = _ctxkv(cache_mla_ckv, cache_mla_krope, w_ukv_p)

    new_k = jnp.zeros((BATCH, DEPTH, SEQ, NA_W), F32)
    new_v = jnp.zeros((BATCH, DEPTH, SEQ, NA_W), F32)
    new_ckv = jnp.zeros((BATCH, DEPTH, SEQ, KV_LORA), F32)
    new_kr = jnp.zeros((BATCH, DEPTH, SEQ, ROPE_DIM), F32)

    x_ctx = x_prompt.reshape(T_CTX, D_MODEL)
    x_lat = x_sample.reshape(T_LAT, D_MODEL)
    for l in range(DEPTH):
        q, k, v, new_k, new_v, qm, km, vv, new_ckv, new_kr, z = _inproj(
            l, x_ctx, x_lat, mods, w_main, w_kr, w_z, g_q, g_kv, w_uq_p, w_ukv_p, cs_tab, new_k, new_v, new_ckv, new_kr)

        oa_c, ob_c = _ctx_attn(q, k, v, qm, km, vv)
        oa_l = _lat_na(l, q, k, v, cache_k, cache_v, bias_tab)
        ob_l = _lat_mla(l, qm, km, vv, kmc, vvc)
        oc = _sgu(l, z, sgu_ln_g, sgu_ln_b, sgu_w, sgu_b)

        x1, hp, e, rank, gate, cnt = _outproj(l, oa_c, oa_l, ob_c, ob_l, oc, w_out_b, x_ctx, x_lat, mods, ln1_g, ln1_b,
                                              w_router_t, b_router_col)
        pos, tile_expert, next_expert, n_tiles = _dispatch_plan(e, rank, cnt[:, 0].astype(jnp.int32), max_tiles)
        pos0, pos1 = pos[0], pos[1]
        xs = _dispatch(pos0, pos1, hp, jnp.zeros((max_tiles * TM_EXP, D_MODEL // 2), jnp.uint32))
        ys = _experts(l, tile_expert, next_expert, n_tiles, xs, w_gate, w_up, w_down, max_tiles)
        gate_t = gate.T
        x_ctx = _tail(l, 0, T_CTX, pos0, pos1, x1, gate_t, mods, ln2_g, ln2_b, ys)
        x_lat = _tail(l, T_CTX, T_LAT, pos0, pos1, x1, gate_t, mods, ln2_g, ln2_b, ys)

    return (x_ctx.reshape(BATCH, SEQ, D_MODEL), x_lat.reshape(DEC_BATCH, DEC_SEQ, D_MODEL),
            new_k.reshape(BATCH, DEPTH, SEQ, NA_HEADS, NA_HEAD_DIM), new_v.reshape(BATCH, DEPTH, SEQ, NA_HEADS, NA_HEAD_DIM),
            new_ckv, new_kr)
```

```python
import functools

import numpy as np
import jax
import jax.numpy as jnp
from jax import lax
from jax.experimental import pallas as pl
from jax.experimental.pallas import tpu as pltpu

F32 = jnp.float32
BF16 = jnp.bfloat16

D_MODEL = 2048
BATCH = 32
SEQ = 256
DEPTH = 2
DEC_BATCH = 2
DEC_SEQ = 2048
PAST_LEN = 256
GRID_W = 64
GRID_R = DEC_SEQ // GRID_W
NA_HEADS = 6
NA_HEAD_DIM = 128
NA_ROWS = 8
NA_COLS = 16
MLA_HEADS = 6
Q_LORA = 512
KV_LORA = 512
NOPE_DIM = 128
ROPE_DIM = 64
V_DIM = 128
ROPE_BASE = 10000.0
MLA_SCALE = (NOPE_DIM + ROPE_DIM) ** -0.5
NA_SCALE = NA_HEAD_DIM ** -0.5
SGU_GROUPS = 4
SGU_DIM = 128
CHUNK = 128
NA_W = NA_HEADS * NA_HEAD_DIM
MLA_W = MLA_HEADS * V_DIM
SGU_W = SGU_GROUPS * SGU_DIM
D_MIX = NA_W + MLA_W + SGU_W
N_EXPERTS = 16
N_GROUPS = 4
EXPERTS_PER_GROUP = N_EXPERTS // N_GROUPS
TOP_K = 2
D_EXPERT = 512
ALPHA = (2 * DEPTH) ** 0.25
LN_EPS = 1e-6
RMS_EPS = 1e-6

T_CTX = BATCH * SEQ
T_LAT = DEC_BATCH * DEC_SEQ
T_ALL = T_CTX + T_LAT
N_COND = 8

C_Q, C_K, C_V = 0, NA_W, 2 * NA_W
C_CQ = 3 * NA_W
C_CKV = C_CQ + Q_LORA
C_KR = C_CKV + KV_LORA
MLA_QK = 256

TM_PROJ = 256
TM_OUT = 512
TM_SGU = 256
TM_TAIL = 512
TM_DISP = 512
TM_EXP = 256
TQ_MLA = 256
MLA_KCHUNK = 512
OUT_SPLIT = 1
NA_QROWS = 4
NA_KROWS = NA_QROWS + NA_ROWS
NEG_BIG = -1e30
VMEM_LIMIT = 56 * 1024 * 1024


def _cparams(sem, vmem=VMEM_LIMIT):
    return pltpu.CompilerParams(dimension_semantics=sem, vmem_limit_bytes=vmem)


def _cond_row(i, tm):
    t0 = i * tm
    return jnp.where(t0 < T_CTX, 0, 1 + (t0 - T_CTX) // DEC_SEQ)


def _ln(x):
    mu = jnp.mean(x, axis=-1, keepdims=True)
    xc = x - mu
    var = jnp.mean(xc * xc, axis=-1, keepdims=True)
    return xc * lax.rsqrt(var + LN_EPS)


def _rms(x):
    return x * lax.rsqrt(jnp.mean(x * x, axis=-1, keepdims=True) + RMS_EPS)


def _dot(a, b):
    return jnp.dot(a, b, preferred_element_type=F32)


def _pack_rows(x):
    xf = x.astype(BF16).astype(F32)
    half = x.shape[-1] // 2
    return pltpu.bitcast(xf[:, :half], jnp.uint32) | (pltpu.bitcast(xf[:, half:], jnp.uint32) >> 16)


def _unpack_rows(w):
    return jnp.concatenate([pltpu.bitcast(w & jnp.uint32(0xFFFF0000), F32), pltpu.bitcast(w << 16, F32)], axis=-1)


def _dot_nt(a, b):
    return lax.dot_general(a, b, (((1,), (1,)), ((), ())), preferred_element_type=F32)


def _ada_kernel(c_ref, w_ref, b_ref, o_ref):
    c = c_ref[...]
    s = c * jax.nn.sigmoid(c)
    o_ref[...] = _dot(s.astype(BF16), w_ref[...].astype(BF16)) + b_ref[...]


def _ada(cond, w_ada, b_ada):
    tn = 1024
    n = 6 * D_MODEL
    return pl.pallas_call(
        _ada_kernel,
        grid=(DEPTH, n // tn),
        in_specs=[
            pl.BlockSpec((N_COND, D_MODEL), lambda l, j: (0, 0)),
            pl.BlockSpec((None, D_MODEL, tn), lambda l, j: (l, 0, j)),
            pl.BlockSpec((None, 1, tn), lambda l, j: (l, 0, j)),
        ],
        out_specs=pl.BlockSpec((None, N_COND, tn), lambda l, j: (l, 0, j)),
        out_shape=jax.ShapeDtypeStruct((DEPTH, N_COND, n), F32),
        compiler_params=_cparams(("parallel", "parallel")),
        name="ada_mod",
    )(cond, w_ada, b_ada.reshape(DEPTH, 1, n))


def _inproj_kernel(xc_ref, xl_ref, sh_ref, sc_ref, w_ref, wkr_ref, wz_ref, gq_ref, gkv_ref, wuq_ref, wukv_ref, cs_ref,
                   k32_in, v32_in, ckv32_in, kr32_in,
                   q_ref, k_ref, v_ref, k32_ref, v32_ref, qm_ref, km_ref, vv_ref,
                   ckv32_ref, kr32_ref, z_ref):
    del k32_in, v32_in, ckv32_in, kr32_in
    i = pl.program_id(0)
    is_ctx = i * TM_PROJ < T_CTX
    x = jnp.where(is_ctx, xc_ref[...], xl_ref[...])
    h = (_ln(x) * (1.0 + sc_ref[...]) + sh_ref[...]).astype(BF16)

    def proj(a, b):
        return _dot(h, w_ref[:, a:b])

    q_ref[...] = proj(C_Q, C_K).astype(BF16)
    k = proj(C_K, C_V)
    k_ref[...] = k.astype(BF16)
    v = proj(C_V, C_CQ)
    v_ref[...] = v.astype(BF16)

    cs = cs_ref[...]
    lane = lax.broadcasted_iota(jnp.int32, cs.shape, 1)

    def rotate(pair):
        t = pair * cs
        return jnp.where(lane < ROPE_DIM, t + pltpu.roll(t, ROPE_DIM, 1), 0.0)

    cqn = (_rms(proj(C_CQ, C_CKV)) * gq_ref[...]).astype(BF16)
    mq = _dot(cqn, wuq_ref[...])
    for hd in range(MLA_HEADS):
        c0 = hd * MLA_QK
        qm_ref[:, c0:c0 + NOPE_DIM] = mq[:, c0:c0 + NOPE_DIM].astype(BF16)
        qm_ref[:, c0 + NOPE_DIM:c0 + MLA_QK] = rotate(mq[:, c0 + NOPE_DIM:c0 + MLA_QK]).astype(BF16)

    ckvn = _rms(proj(C_CKV, C_KR)) * gkv_ref[...]
    kv = _dot(ckvn.astype(BF16), wukv_ref[...])
    kr2 = _dot(h, wkr_ref[...])
    krot = rotate(kr2).astype(BF16)
    for hd in range(MLA_HEADS):
        c0 = hd * MLA_QK
        km_ref[:, c0:c0 + NOPE_DIM] = kv[:, hd * NOPE_DIM:(hd + 1) * NOPE_DIM].astype(BF16)
        km_ref[:, c0 + NOPE_DIM:c0 + MLA_QK] = krot
    vv_ref[...] = kv[:, MLA_HEADS * NOPE_DIM:].astype(BF16)

    z_ref[...] = _dot(h, wz_ref[...])

    @pl.when(is_ctx)
    def _():
        k32_ref[...] = k
        v32_ref[...] = v
        ckv32_ref[...] = ckvn
        kr32_ref[...] = kr2[:, :ROPE_DIM]


def _inproj(l, x_ctx, x_lat, mods, w_main, w_kr, w_z, g_q, g_kv, w_uq_p, w_ukv_p, cs_tab, new_k, new_v, new_ckv, new_kr):
    tm = TM_PROJ
    assert tm == SEQ
    nct = T_CTX // tm
    per_b = DEC_SEQ // tm

    def row(i):
        return (i, 0)

    def ctx_row(i):
        return (jnp.minimum(i, nct - 1), 0)

    def lat_row(i):
        return (jnp.maximum(i - nct, 0), 0)

    def cs_row(i):
        return (jnp.where(i < nct, 0, 1 + (i - nct) % per_b), 0)

    def mod(kind):
        return pl.BlockSpec((None, None, None, 1, D_MODEL), lambda i: (l, _cond_row(i, tm), kind, 0, 0))

    def const2(shape):
        return pl.BlockSpec(shape, lambda i: (l, 0, 0), pipeline_mode=pl.Buffered(1))

    def cache(w):
        return pl.BlockSpec((None, None, SEQ, w), lambda i: (jnp.minimum(i, nct - 1), l, 0, 0))

    anyspec = pl.BlockSpec(memory_space=pl.ANY)
    out_shapes = [
        jax.ShapeDtypeStruct((T_ALL, NA_W), BF16),
        jax.ShapeDtypeStruct((T_ALL, NA_W), BF16),
        jax.ShapeDtypeStruct((T_ALL, NA_W), BF16),
        jax.ShapeDtypeStruct(new_k.shape, F32),
        jax.ShapeDtypeStruct(new_v.shape, F32),
        jax.ShapeDtypeStruct((T_ALL, MLA_HEADS * MLA_QK), BF16),
        jax.ShapeDtypeStruct((T_ALL, MLA_HEADS * MLA_QK), BF16),
        jax.ShapeDtypeStruct((T_ALL, MLA_W), BF16),
        jax.ShapeDtypeStruct(new_ckv.shape, F32),
        jax.ShapeDtypeStruct(new_kr.shape, F32),
        jax.ShapeDtypeStruct((T_ALL, 2 * SGU_W), F32),
    ]
    out_specs = [
        pl.BlockSpec((tm, NA_W), row), pl.BlockSpec((tm, NA_W), row), pl.BlockSpec((tm, NA_W), row),
        cache(NA_W), cache(NA_W),
        pl.BlockSpec((tm, MLA_HEADS * MLA_QK), row), pl.BlockSpec((tm, MLA_HEADS * MLA_QK), row),
        pl.BlockSpec((tm, MLA_W), row),
        cache(KV_LORA), cache(ROPE_DIM),
        pl.BlockSpec((tm, 2 * SGU_W), row),
    ]
    return pl.pallas_call(
        _inproj_kernel,
        grid=(T_ALL // tm,),
        in_specs=[
            pl.BlockSpec((tm, D_MODEL), ctx_row), pl.BlockSpec((tm, D_MODEL), lat_row),
            mod(0), mod(1),
            const2((None, D_MODEL, C_KR)), const2((None, D_MODEL, 2 * ROPE_DIM)), const2((None, D_MODEL, 2 * SGU_W)),
            pl.BlockSpec((None, 1, Q_LORA), lambda i: (l, 0, 0)),
            pl.BlockSpec((None, 1, KV_LORA), lambda i: (l, 0, 0)),
            const2((None, Q_LORA, MLA_HEADS * MLA_QK)),
            const2((None, KV_LORA, 2 * MLA_W)),
            pl.BlockSpec((tm, 2 * ROPE_DIM), cs_row),
            anyspec, anyspec, anyspec, anyspec,
        ],
        out_specs=out_specs,
        out_shape=out_shapes,
        input_output_aliases={12: 3, 13: 4, 14: 8, 15: 9},
        compiler_params=_cparams(("arbitrary",)),
        name="in_proj",
    )(x_ctx, x_lat, mods, mods, w_main, w_kr, w_z, g_q.reshape(DEPTH, 1, Q_LORA), g_kv.reshape(DEPTH, 1, KV_LORA),
      w_uq_p, w_ukv_p, cs_tab, new_k, new_v, new_ckv, new_kr)


def _ctxkv_kernel(ckv_ref, kr_ref, w_ref, km_ref, vv_ref):
    kv = _dot(ckv_ref[...].astype(BF16), w_ref[...])
    kr = jnp.concatenate([kr_ref[...], jnp.zeros((PAST_LEN, ROPE_DIM), F32)], axis=-1).astype(BF16)
    for hd in range(MLA_HEADS):
        c0 = hd * MLA_QK
        km_ref[:, c0:c0 + NOPE_DIM] = kv[:, hd * NOPE_DIM:(hd + 1) * NOPE_DIM].astype(BF16)
        km_ref[:, c0 + NOPE_DIM:c0 + MLA_QK] = kr
    vv_ref[...] = kv[:, MLA_HEADS * NOPE_DIM:].astype(BF16)


def _ctxkv(cache_ckv, cache_krope, w_ukv_p):
    return pl.pallas_call(
        _ctxkv_kernel,
        grid=(DEPTH, DEC_BATCH),
        in_specs=[
            pl.BlockSpec((None, None, PAST_LEN, KV_LORA), lambda l, b: (b, l, 0, 0)),
            pl.BlockSpec((None, None, PAST_LEN, ROPE_DIM), lambda l, b: (b, l, 0, 0)),
            pl.BlockSpec((None, KV_LORA, 2 * MLA_W), lambda l, b: (l, 0, 0)),
        ],
        out_specs=[
            pl.BlockSpec((None, None, PAST_LEN, MLA_HEADS * MLA_QK), lambda l, b: (l, b, 0, 0)),
            pl.BlockSpec((None, None, PAST_LEN, MLA_W), lambda l, b: (l, b, 0, 0)),
        ],
        out_shape=[
            jax.ShapeDtypeStruct((DEPTH, DEC_BATCH, PAST_LEN, MLA_HEADS * MLA_QK), BF16),
            jax.ShapeDtypeStruct((DEPTH, DEC_BATCH, PAST_LEN, MLA_W), BF16),
        ],
        compiler_params=_cparams(("parallel", "parallel")),
        name="ctx_cache_kv",
    )(cache_ckv, cache_krope, w_ukv_p)


def _softmax_pv(blocks):
    ss = [s for s, _ in blocks]
    m = ss[0].max(axis=-1, keepdims=True)
    for s in ss[1:]:
        m = jnp.maximum(m, s.max(axis=-1, keepdims=True))
    num = None
    den = None
    for s, (_, v) in zip(ss, blocks):
        e = jnp.exp(s - m)
        d = e.sum(axis=-1, keepdims=True)
        o = _dot(e.astype(BF16), v)
        num = o if num is None else num + o
        den = d if den is None else den + d
    return num / den


def _ctx_attn_kernel(q_ref, k_ref, v_ref, qm_ref, km_ref, vv_ref, oa_ref, ob_ref):
    for hd in range(NA_HEADS):
        sl = slice(hd * NA_HEAD_DIM, (hd + 1) * NA_HEAD_DIM)
        s = _dot_nt(q_ref[:, sl], k_ref[:, sl]) * NA_SCALE
        oa_ref[:, sl] = _softmax_pv([(s, v_ref[:, sl])]).astype(BF16)
    for hd in range(MLA_HEADS):
        sq = slice(hd * MLA_QK, (hd + 1) * MLA_QK)
        sv = slice(hd * V_DIM, (hd + 1) * V_DIM)
        s = _dot_nt(qm_ref[:, sq], km_ref[:, sq]) * MLA_SCALE
        ob_ref[:, sv] = _softmax_pv([(s, vv_ref[:, sv])]).astype(BF16)


def _ctx_attn(q, k, v, qm, km, vv):
    def spec(w):
        return pl.BlockSpec((SEQ, w), lambda b: (b, 0))

    return pl.pallas_call(
        _ctx_attn_kernel,
        grid=(BATCH,),
        in_specs=[spec(NA_W), spec(NA_W), spec(NA_W),
                  spec(MLA_HEADS * MLA_QK), spec(MLA_HEADS * MLA_QK), spec(MLA_W)],
        out_specs=[spec(NA_W), spec(MLA_W)],
        out_shape=[jax.ShapeDtypeStruct((T_CTX, NA_W), BF16), jax.ShapeDtypeStruct((T_CTX, MLA_W), BF16)],
        compiler_params=_cparams(("parallel",)),
        name="ctx_attn",
    )(q, k, v, qm, km, vv)


def _lat_mla_kernel(q_ref, k_ref, v_ref, kc_ref, vc_ref, o_ref):
    q = q_ref[...]
    blocks = []
    for c in range(DEC_SEQ // MLA_KCHUNK):
        ks = slice(c * MLA_KCHUNK, (c + 1) * MLA_KCHUNK)
        blocks.append((_dot_nt(q, k_ref[ks, :]) * MLA_SCALE, v_ref[ks, :]))
    blocks.append((_dot_nt(q, kc_ref[...]) * MLA_SCALE, vc_ref[...]))
    o_ref[...] = _softmax_pv(blocks).astype(BF16)


def _lat_mla(l, qm, km, vv, kmc, vvc):
    tq = TQ_MLA
    nq = DEC_SEQ // tq
    row0 = T_CTX // tq
    kb0 = T_CTX // DEC_SEQ
    return pl.pallas_call(
        _lat_mla_kernel,
        grid=(DEC_BATCH, MLA_HEADS, nq),
        in_specs=[
            pl.BlockSpec((tq, MLA_QK), lambda b, h, t: (row0 + b * nq + t, h)),
            pl.BlockSpec((DEC_SEQ, MLA_QK), lambda b, h, t: (kb0 + b, h)),
            pl.BlockSpec((DEC_SEQ, V_DIM), lambda b, h, t: (kb0 + b, h)),
            pl.BlockSpec((None, None, PAST_LEN, MLA_QK), lambda b, h, t: (l, b, 0, h)),
            pl.BlockSpec((None, None, PAST_LEN, V_DIM), lambda b, h, t: (l, b, 0, h)),
        ],
        out_specs=pl.BlockSpec((tq, V_DIM), lambda b, h, t: (b * nq + t, h)),
        out_shape=jax.ShapeDtypeStruct((T_LAT, MLA_W), BF16),
        compiler_params=_cparams(("parallel", "parallel", "arbitrary")),
        name="lat_mla",
    )(qm, km, vv, kmc, vvc)


def _na_block_plan():
    plan = []
    for r0 in range(0, GRID_R, NA_QROWS):
        ws = min(max(r0 - NA_ROWS // 2, 0), GRID_R - NA_KROWS)
        rows = []
        for r in range(r0, r0 + NA_QROWS):
            rs = min(max(r - NA_ROWS // 2, 0), GRID_R - NA_ROWS)
            rel = [kr - r + NA_ROWS - 1 if rs <= kr < rs + NA_ROWS else None for kr in range(ws, ws + NA_KROWS)]
            rows.append([(rel[j], rel[j + 1]) for j in range(0, NA_KROWS, 2)])
        plan.append((ws, rows))
    return plan


def _lat_na_kernel(q_ref, k_ref, v_ref, ck_ref, cv_ref, tab_ref, o_ref):
    ck = ck_ref[...].astype(BF16)
    cv = cv_ref[...].astype(BF16)
    nq = NA_QROWS * GRID_W
    nk = NA_KROWS * GRID_W
    lane = lax.broadcasted_iota(jnp.int32, (GRID_W, 2 * GRID_W), 1)
    masked = jnp.full((GRID_W, 2 * GRID_W), NEG_BIG, F32)

    def pair_bias(d0, d1):
        if d0 is None and d1 is None:
            return masked
        if d1 is None:
            return jnp.where(lane < GRID_W, tab_ref[d0], NEG_BIG)
        if d0 is None:
            return jnp.where(lane >= GRID_W, tab_ref[d1 - 1], NEG_BIG)
        return tab_ref[d0]

    for blk, (ws, rows) in enumerate(_na_block_plan()):
        q = q_ref[blk * nq:(blk + 1) * nq, :]
        kw = k_ref[ws * GRID_W:ws * GRID_W + nk, :]
        vw = v_ref[ws * GRID_W:ws * GRID_W + nk, :]
        bias = jnp.concatenate(
            [jnp.concatenate([pair_bias(d0, d1) for d0, d1 in pairs], axis=1) for pairs in rows], axis=0)
        s1 = _dot_nt(q, kw) * NA_SCALE + bias
        s2 = _dot_nt(q, ck) * NA_SCALE
        o_ref[blk * nq:(blk + 1) * nq, :] = _softmax_pv([(s1, vw), (s2, cv)]).astype(BF16)


def _lat_na(l, q, k, v, cache_k, cache_v, bias_tab):
    kb0 = T_CTX // DEC_SEQ

    def lat(b, h):
        return (kb0 + b, h)

    return pl.pallas_call(
        _lat_na_kernel,
        grid=(DEC_BATCH, NA_HEADS),
        in_specs=[
            pl.BlockSpec((DEC_SEQ, NA_HEAD_DIM), lat),
            pl.BlockSpec((DEC_SEQ, NA_HEAD_DIM), lat),
            pl.BlockSpec((DEC_SEQ, NA_HEAD_DIM), lat),
            pl.BlockSpec((None, None, PAST_LEN, NA_HEAD_DIM), lambda b, h: (b, l, 0, h)),
            pl.BlockSpec((None, None, PAST_LEN, NA_HEAD_DIM), lambda b, h: (b, l, 0, h)),
            pl.BlockSpec((None, None, 2 * NA_ROWS - 1, GRID_W, 2 * GRID_W), lambda b, h: (l, h, 0, 0, 0)),
        ],
        out_specs=pl.BlockSpec((DEC_SEQ, NA_HEAD_DIM), lambda b, h: (b, h)),
        out_shape=jax.ShapeDtypeStruct((T_LAT, NA_W), BF16),
        compiler_params=_cparams(("parallel", "parallel")),
        name="lat_na",
    )(q, k, v, cache_k, cache_v, bias_tab)


def _sgu_kernel(z_ref, g_ref, b_ref, w_ref, bs_ref, o_ref):
    z = z_ref[...]
    a = 0.5 * z * (1.0 + lax.erf(z * (0.5 ** 0.5)))
    for c in range(TM_SGU // CHUNK):
        rows = slice(c * CHUNK, (c + 1) * CHUNK)
        for g in range(SGU_GROUPS):
            u = a[rows, g * SGU_DIM:(g + 1) * SGU_DIM]
            v = a[rows, SGU_W + g * SGU_DIM:SGU_W + (g + 1) * SGU_DIM]
            vn = _ln(v) * g_ref[g:g + 1, :] + b_ref[g:g + 1, :]
            t = _dot(w_ref[g].astype(BF16), vn.astype(BF16)) + bs_ref[g]
            o_ref[rows, g * SGU_DIM:(g + 1) * SGU_DIM] = (u * t).astype(BF16)


def _sgu(l, z, ln_g, ln_b, w_s, b_s):
    tm = TM_SGU
    return pl.pallas_call(
        _sgu_kernel,
        grid=(T_ALL // tm,),
        in_specs=[
            pl.BlockSpec((tm, 2 * SGU_W), lambda i: (i, 0)),
            pl.BlockSpec((None, SGU_GROUPS, SGU_DIM), lambda i: (l, 0, 0)),
            pl.BlockSpec((None, SGU_GROUPS, SGU_DIM), lambda i: (l, 0, 0)),
            pl.BlockSpec((None, SGU_GROUPS, CHUNK, CHUNK), lambda i: (l, 0, 0, 0)),
            pl.BlockSpec((None, SGU_GROUPS, CHUNK, 1), lambda i: (l, 0, 0, 0)),
        ],
        out_specs=pl.BlockSpec((tm, SGU_W), lambda i: (i, 0)),
        out_shape=jax.ShapeDtypeStruct((T_ALL, SGU_W), BF16),
        compiler_params=_cparams(("parallel",)),
        name="sgu",
    )(z, ln_g, ln_b, w_s, b_s.reshape(DEPTH, SGU_GROUPS, CHUNK, 1))


def _route_rows(logits_t, bias_col):
    s = jax.nn.sigmoid(logits_t)
    sb = s + bias_col
    rows = [sb[r:r + 1, :] for r in range(N_EXPERTS)]
    srows = [s[r:r + 1, :] for r in range(N_EXPERTS)]
    best, gsel = None, None
    for g in range(N_GROUPS):
        m = rows[g * EXPERTS_PER_GROUP:(g + 1) * EXPERTS_PER_GROUP]
        score = None
        for a in range(EXPERTS_PER_GROUP):
            for b in range(a + 1, EXPERTS_PER_GROUP):
                pair = m[a] + m[b]
                score = pair if score is None else jnp.maximum(score, pair)
        if best is None:
            best, gsel = score, jnp.zeros(score.shape, jnp.int32)
        else:
            upd = score > best
            best = jnp.where(upd, score, best)
            gsel = jnp.where(upd, g, gsel)

    def pick(table, j):
        out = table[j]
        for g in range(1, N_GROUPS):
            out = jnp.where(gsel == g, table[g * EXPERTS_PER_GROUP + j], out)
        return out

    v = [pick(rows, j) for j in range(EXPERTS_PER_GROUP)]
    sv = [pick(srows, j) for j in range(EXPERTS_PER_GROUP)]
    m1, i1, w1 = v[0], jnp.zeros(gsel.shape, jnp.int32), sv[0]
    for j in range(1, EXPERTS_PER_GROUP):
        upd = v[j] > m1
        m1 = jnp.where(upd, v[j], m1)
        i1 = jnp.where(upd, j, i1)
        w1 = jnp.where(upd, sv[j], w1)
    m2 = jnp.full(gsel.shape, -jnp.inf, F32)
    i2 = jnp.zeros(gsel.shape, jnp.int32)
    w2 = jnp.zeros(gsel.shape, F32)
    for j in range(EXPERTS_PER_GROUP):
        upd = jnp.logical_and(i1 != j, v[j] > m2)
        m2 = jnp.where(upd, v[j], m2)
        i2 = jnp.where(upd, j, i2)
        w2 = jnp.where(upd, sv[j], w2)
    den = w1 + w2
    return gsel * EXPERTS_PER_GROUP + i1, gsel * EXPERTS_PER_GROUP + i2, w1 / den, w2 / den


def _outproj_kernel(ac_ref, al_ref, bc_ref, bl_ref, c_ref, wa_ref, wb_ref, wc_ref, xc_ref, xl_ref, g1_ref, sh2_ref, sc2_ref,
                    lg_ref, lb_ref, wrt_ref, br_ref, x1_ref, hp_ref, e_ref, rank_ref, gate_ref, cnt_ref, base_ref):
    is_ctx = pl.program_id(0) * TM_OUT < T_CTX

    @pl.when(pl.program_id(0) == 0)
    def _():
        base_ref[...] = jnp.zeros_like(base_ref)

    ts = TM_OUT // OUT_SPLIT
    sub = lax.broadcasted_iota(jnp.int32, (N_EXPERTS, ts), 0)
    before = (lax.broadcasted_iota(jnp.int32, (ts, ts), 0) < lax.broadcasted_iota(jnp.int32, (ts, ts), 1)).astype(BF16)
    base = base_ref[:, 0:1]
    for part in range(OUT_SPLIT):
        rows = slice(part * ts, (part + 1) * ts)
        a = jnp.where(is_ctx, ac_ref[rows, :], al_ref[rows, :])
        b = jnp.where(is_ctx, bc_ref[rows, :], bl_ref[rows, :])
        x = jnp.where(is_ctx, xc_ref[rows, :], xl_ref[rows, :])
        mixed = _dot(a, wa_ref[...]) + _dot(b, wb_ref[...]) + _dot(c_ref[rows, :], wc_ref[...])
        x1 = _ln(ALPHA * x + g1_ref[...] * mixed) * lg_ref[...] + lb_ref[...]
        x1_ref[rows, :] = x1
        h2 = (_ln(x1) * (1.0 + sc2_ref[...]) + sh2_ref[...]).astype(BF16)

        hp_ref[rows, :] = _pack_rows(h2)

        e1, e2, w1, w2 = _route_rows(_dot_nt(wrt_ref[...], h2), br_ref[...])
        e_ref[0:1, rows] = e1
        e_ref[1:2, rows] = e2
        gate_ref[0:1, rows] = w1
        gate_ref[1:2, rows] = w2

        o1 = (sub == e1).astype(F32)
        o2 = (sub == e2).astype(F32)
        p1 = _dot(o1.astype(BF16), before)
        p2 = _dot(o2.astype(BF16), before)
        c1 = o1.sum(axis=1, keepdims=True)
        c2 = o2.sum(axis=1, keepdims=True)
        rank_ref[0:1, rows] = jnp.sum(o1 * (base + p1), axis=0, keepdims=True).astype(jnp.int32)
        rank_ref[1:2, rows] = jnp.sum(o2 * (base + c1 + p2), axis=0, keepdims=True).astype(jnp.int32)
        base = base + (c1 + c2)
    base_ref[...] = jnp.broadcast_to(base, base_ref.shape)
    cnt_ref[...] = base_ref[...]


def _outproj(l, oa_c, oa_l, ob_c, ob_l, oc, w_out_b, x_ctx, x_lat, mods, ln_g, ln_b, w_router_t, b_router_col):
    tm = TM_OUT
    nct = T_CTX // tm

    def row(i):
        return (i, 0)

    def ctx_row(i):
        return (jnp.minimum(i, nct - 1), 0)

    def lat_row(i):
        return (jnp.maximum(i - nct, 0), 0)

    def tok(i):
        return (0, i)

    def mod(kind):
        return pl.BlockSpec((None, None, None, 1, D_MODEL), lambda i: (l, _cond_row(i, tm), kind, 0, 0))

    def wblk(rows, blk):
        return pl.BlockSpec((None, rows, D_MODEL), lambda i: (l, blk, 0), pipeline_mode=pl.Buffered(1))

    vec = pl.BlockSpec((None, 1, D_MODEL), lambda i: (l, 0, 0))
    return pl.pallas_call(
        _outproj_kernel,
        grid=(T_ALL // tm,),
        in_specs=[
            pl.BlockSpec((tm, NA_W), ctx_row), pl.BlockSpec((tm, NA_W), lat_row),
            pl.BlockSpec((tm, MLA_W), ctx_row), pl.BlockSpec((tm, MLA_W), lat_row),
            pl.BlockSpec((tm, SGU_W), row),
            wblk(NA_W, 0), wblk(MLA_W, 1), wblk(SGU_W, (NA_W + MLA_W) // SGU_W),
            pl.BlockSpec((tm, D_MODEL), ctx_row), pl.BlockSpec((tm, D_MODEL), lat_row),
            mod(2), mod(3), mod(4), vec, vec,
            pl.BlockSpec((N_EXPERTS, D_MODEL), lambda i: (0, 0)),
            pl.BlockSpec((N_EXPERTS, 1), lambda i: (0, 0)),
        ],
        out_specs=[pl.BlockSpec((tm, D_MODEL), row), pl.BlockSpec((tm, D_MODEL // 2), row),
                   pl.BlockSpec((TOP_K, tm), tok), pl.BlockSpec((TOP_K, tm), tok), pl.BlockSpec((TOP_K, tm), tok),
                   pl.BlockSpec((N_EXPERTS, 128), lambda i: (0, 0))],
        out_shape=[jax.ShapeDtypeStruct((T_ALL, D_MODEL), F32),
                   jax.ShapeDtypeStruct((T_ALL, D_MODEL // 2), jnp.uint32),
                   jax.ShapeDtypeStruct((TOP_K, T_ALL), jnp.int32),
                   jax.ShapeDtypeStruct((TOP_K, T_ALL), jnp.int32),
                   jax.ShapeDtypeStruct((TOP_K, T_ALL), F32),
                   jax.ShapeDtypeStruct((N_EXPERTS, 128), F32)],
        scratch_shapes=[pltpu.VMEM((N_EXPERTS, 128), F32)],
        compiler_params=_cparams(("arbitrary",)),
        name="out_proj",
    )(oa_c, oa_l, ob_c, ob_l, oc, w_out_b, w_out_b, w_out_b, x_ctx, x_lat, mods, mods, mods,
      ln_g.reshape(DEPTH, 1, D_MODEL), ln_b.reshape(DEPTH, 1, D_MODEL), w_router_t, b_router_col)


def _dispatch_plan(e, rank, counts, max_tiles):
    tiles = (counts + TM_EXP - 1) // TM_EXP
    tile_end = jnp.cumsum(tiles)
    off = (tile_end - tiles) * TM_EXP
    n_tiles = tile_end[-1]
    ids = jnp.arange(N_EXPERTS, dtype=jnp.int32)
    pos = rank + jnp.sum(jnp.where(e[..., None] == ids, off, 0), axis=-1)
    ti = jnp.arange(max_tiles, dtype=jnp.int32)
    owner = jnp.sum((tile_end[None, :] <= ti[:, None]).astype(jnp.int32), axis=1)
    last = jnp.sum((tile_end <= n_tiles - 1).astype(jnp.int32))
    tile_expert = jnp.where(ti < n_tiles, owner, last).astype(jnp.int32)
    later_used = jnp.logical_and(ids[None, :] > ids[:, None], tiles[None, :] > 0)
    next_expert = jnp.min(jnp.where(later_used, ids[None, :], N_EXPERTS), axis=1)
    next_expert = jnp.where(next_expert < N_EXPERTS, next_expert, -1).astype(jnp.int32)
    pad_lo = (off + counts).astype(jnp.int32)
    pad_hi = (off + tiles * TM_EXP).astype(jnp.int32)
    return pos.astype(jnp.int32), tile_expert, next_expert, n_tiles.reshape(1).astype(jnp.int32), pad_lo, pad_hi


def _row_copy(src_ref, src_row, dst_ref, dst_row, sem):
    return pltpu.make_async_copy(src_ref.at[pl.ds(src_row, 1)], dst_ref.at[pl.ds(dst_row, 1)], sem)


def _dispatch_kernel(lo_ref, hi_ref, nt_ref, p0_ref, p1_ref, hp_ref, xs_ref, zeros, sem, *, max_tiles):
    @pl.when(pl.program_id(0) == 0)
    def _():
        zeros[...] = jnp.zeros_like(zeros)
        for ex in range(N_EXPERTS):
            def fill(r, carry):
                _row_copy(zeros, 0, xs_ref, r, sem).start()
                return carry

            def fill_done(r, carry):
                _row_copy(zeros, 0, xs_ref, 0, sem).wait()
                return carry

            lax.fori_loop(lo_ref[ex], hi_ref[ex], fill, 0)
            lax.fori_loop(lo_ref[ex], hi_ref[ex], fill_done, 0)

        def tile_copy(t):
            return pltpu.make_async_copy(zeros, xs_ref.at[pl.ds(t * TM_EXP, TM_EXP)], sem)

        def fill_tile(t, carry):
            tile_copy(t).start()
            return carry

        def fill_tile_done(t, carry):
            tile_copy(0).wait()
            return carry

        lax.fori_loop(nt_ref[0], max_tiles, fill_tile, 0)
        lax.fori_loop(nt_ref[0], max_tiles, fill_tile_done, 0)

    def issue(t, carry):
        for p_ref in (p0_ref, p1_ref):
            _row_copy(hp_ref, t, xs_ref, p_ref[t], sem).start()
        return carry

    lax.fori_loop(0, TM_DISP, issue, 0, unroll=8)

    def drain(t, carry):
        for k in range(TOP_K):
            _row_copy(hp_ref, 0, xs_ref, 0, sem).wait()
        return carry

    lax.fori_loop(0, TM_DISP, drain, 0, unroll=8)


def _dispatch(pad_lo, pad_hi, n_tiles, pos0, pos1, hp, max_tiles):
    grid_spec = pltpu.PrefetchScalarGridSpec(
        num_scalar_prefetch=3,
        grid=(T_ALL // TM_DISP,),
        in_specs=[
            pl.BlockSpec((TM_DISP,), lambda i, lo, hi, nt: (i,), memory_space=pltpu.SMEM),
            pl.BlockSpec((TM_DISP,), lambda i, lo, hi, nt: (i,), memory_space=pltpu.SMEM),
            pl.BlockSpec((TM_DISP, D_MODEL // 2), lambda i, lo, hi, nt: (i, 0)),
        ],
        out_specs=pl.BlockSpec(memory_space=pl.ANY),
        scratch_shapes=[pltpu.VMEM((TM_EXP, D_MODEL // 2), jnp.uint32), pltpu.SemaphoreType.DMA(())],
    )
    return pl.pallas_call(
        functools.partial(_dispatch_kernel, max_tiles=max_tiles),
        grid_spec=grid_spec,
        out_shape=jax.ShapeDtypeStruct((max_tiles * TM_EXP, D_MODEL // 2), jnp.uint32),
        compiler_params=_cparams(("arbitrary",)),
        name="dispatch",
    )(pad_lo, pad_hi, n_tiles, pos0, pos1, hp)


def _expert_kernel(te_ref, nxt_ref, nt_ref, x_ref, wg_hbm, wu_hbm, wd_hbm, y_ref,
                   wg_f, wu_f, wd_f, wg_s, wu_s, wd_s, switches, sem, *, layer):
    i = pl.program_id(0)
    e = te_ref[i]

    def weight_copies(expert, slot):
        return [pltpu.make_async_copy(w_hbm.at[layer, expert], buf.at[slot], sem.at[slot, j])
                for j, (w_hbm, buf) in enumerate(((wg_hbm, wg_f), (wu_hbm, wu_f), (wd_hbm, wd_f)))]

    @pl.when(i == 0)
    def _():
        for c in weight_copies(e, 0):
            c.start()

    fresh = jnp.logical_or(i == 0, e != te_ref[jnp.maximum(i - 1, 0)])

    @pl.when(fresh)
    def _():
        n_sw = jnp.where(i == 0, 0, switches[0] + 1)
        switches[0] = n_sw
        slot = n_sw % 2
        for c in weight_copies(e, slot):
            c.wait()
        wg_s[...] = wg_f[slot].astype(BF16)
        wu_s[...] = wu_f[slot].astype(BF16)
        wd_s[...] = wd_f[slot].astype(BF16)
        nx = nxt_ref[e]

        @pl.when(nx >= 0)
        def _():
            for c in weight_copies(nx, 1 - slot):
                c.start()

    @pl.when(i < nt_ref[0])
    def _():
        x = _unpack_rows(x_ref[...]).astype(BF16)
        g = _dot(x, wg_s[...])
        u = _dot(x, wu_s[...])
        a = (g * jax.nn.sigmoid(g) * u).astype(BF16)
        y_ref[...] = _pack_rows(_dot(a, wd_s[...]))

    @pl.when(i >= nt_ref[0])
    def _():
        y_ref[...] = jnp.zeros_like(y_ref)


def _experts(l, tile_expert, next_expert, n_tiles, xs, w_gate, w_up, w_down, max_tiles):
    tm = TM_EXP
    anyspec = pl.BlockSpec(memory_space=pl.ANY)
    grid_spec = pltpu.PrefetchScalarGridSpec(
        num_scalar_prefetch=3,
        grid=(max_tiles,),
        in_specs=[pl.BlockSpec((tm, D_MODEL // 2), lambda i, te, nx, nt: (i, 0)), anyspec, anyspec, anyspec],
        out_specs=pl.BlockSpec((tm, D_MODEL // 2), lambda i, te, nx, nt: (i, 0)),
        scratch_shapes=[pltpu.VMEM((2, D_MODEL, D_EXPERT), F32), pltpu.VMEM((2, D_MODEL, D_EXPERT), F32),
                        pltpu.VMEM((2, D_EXPERT, D_MODEL), F32),
                        pltpu.VMEM((D_MODEL, D_EXPERT), BF16), pltpu.VMEM((D_MODEL, D_EXPERT), BF16),
                        pltpu.VMEM((D_EXPERT, D_MODEL), BF16),
                        pltpu.SMEM((1,), jnp.int32), pltpu.SemaphoreType.DMA((2, 3))],
    )
    return pl.pallas_call(
        functools.partial(_expert_kernel, layer=l),
        grid_spec=grid_spec,
        out_shape=jax.ShapeDtypeStruct((max_tiles * tm, D_MODEL // 2), jnp.uint32),
        compiler_params=_cparams(("arbitrary",)),
        name="experts",
    )(tile_expert, next_expert, n_tiles, xs, w_gate, w_up, w_down)


def _tail_kernel(p0_ref, p1_ref, p0n_ref, p1n_ref, x1_ref, gate_ref, g2_ref, lg_ref, lb_ref, ys_ref, o_ref, ybuf, sem,
                 *, n_tiles):
    tm = TM_TAIL
    i = pl.program_id(0)
    slot = i % 2

    def issue(p_refs, s):
        def body(t, carry):
            for k, p_ref in enumerate(p_refs):
                _row_copy(ys_ref, p_ref[t], ybuf.at[s, k], t, sem.at[s]).start()
            return carry

        lax.fori_loop(0, tm, body, 0, unroll=8)

    @pl.when(i == 0)
    def _():
        issue((p0_ref, p1_ref), 0)

    @pl.when(i + 1 < n_tiles)
    def _():
        issue((p0n_ref, p1n_ref), 1 - slot)

    def drain(t, carry):
        for k in range(TOP_K):
            _row_copy(ys_ref, 0, ybuf.at[slot, k], 0, sem.at[slot]).wait()
        return carry

    lax.fori_loop(0, tm, drain, 0, unroll=8)

    gate = gate_ref[...]
    y = gate[:, 0:1] * _unpack_rows(ybuf[slot, 0]) + gate[:, 1:2] * _unpack_rows(ybuf[slot, 1])
    o_ref[...] = _ln(ALPHA * x1_ref[...] + g2_ref[...] * y) * lg_ref[...] + lb_ref[...]


def _tail(l, t_start, t_count, pos0, pos1, x1, gate_t, mods, ln_g, ln_b, ys):
    tm = TM_TAIL
    n = t_count // tm
    i0 = t_start // tm
    vec = pl.BlockSpec((None, 1, D_MODEL), lambda i: (l, 0, 0))
    return pl.pallas_call(
        functools.partial(_tail_kernel, n_tiles=n),
        grid=(n,),
        in_specs=[
            pl.BlockSpec((tm,), lambda i: (i0 + i,), memory_space=pltpu.SMEM),
            pl.BlockSpec((tm,), lambda i: (i0 + i,), memory_space=pltpu.SMEM),
            pl.BlockSpec((tm,), lambda i: (i0 + jnp.minimum(i + 1, n - 1),), memory_space=pltpu.SMEM),
            pl.BlockSpec((tm,), lambda i: (i0 + jnp.minimum(i + 1, n - 1),), memory_space=pltpu.SMEM),
            pl.BlockSpec((tm, D_MODEL), lambda i: (i0 + i, 0)),
            pl.BlockSpec((tm, TOP_K), lambda i: (i0 + i, 0)),
            pl.BlockSpec((None, None, None, 1, D_MODEL), lambda i: (l, _cond_row(i0 + i, tm), 5, 0, 0)),
            vec, vec,
            pl.BlockSpec(memory_space=pl.ANY),
        ],
        out_specs=pl.BlockSpec((tm, D_MODEL), lambda i: (i, 0)),
        out_shape=jax.ShapeDtypeStruct((t_count, D_MODEL), F32),
        scratch_shapes=[pltpu.VMEM((2, TOP_K, tm, D_MODEL // 2), jnp.uint32), pltpu.SemaphoreType.DMA((2,))],
        compiler_params=_cparams(("arbitrary",)),
        name="tail",
    )(pos0, pos1, pos0, pos1, x1, gate_t, mods, ln_g.reshape(DEPTH, 1, D_MODEL), ln_b.reshape(DEPTH, 1, D_MODEL), ys)


def _swap_partners(w):
    nf = ROPE_DIM // 4
    return jnp.concatenate([w[..., nf:2 * nf], w[..., :nf], w[..., 3 * nf:], w[..., 2 * nf:3 * nf]], axis=-1)


def _prep_w_in(w_in):
    kr = w_in[..., C_KR:C_KR + ROPE_DIM]
    w_kr = jnp.concatenate([kr, _swap_partners(kr)], axis=-1).astype(BF16)
    return w_in[..., :C_KR].astype(BF16), w_kr, w_in[..., C_KR + ROPE_DIM:].astype(BF16)


def _prep_w_uq(w_uq):
    w = w_uq.reshape(DEPTH, Q_LORA, MLA_HEADS, NOPE_DIM + ROPE_DIM)
    return jnp.concatenate([w, _swap_partners(w[..., NOPE_DIM:])], axis=-1).reshape(
        DEPTH, Q_LORA, MLA_HEADS * MLA_QK).astype(BF16)


def _prep_w_ukv(w_ukv):
    w = w_ukv.reshape(DEPTH, KV_LORA, MLA_HEADS, NOPE_DIM + V_DIM)
    return jnp.concatenate([w[..., :NOPE_DIM].reshape(DEPTH, KV_LORA, MLA_W),
                            w[..., NOPE_DIM:].reshape(DEPTH, KV_LORA, MLA_W)], axis=-1).astype(BF16)


def _rope_table(tm):
    half = ROPE_DIM // 2
    nf = half // 2
    t = jnp.arange(DEC_SEQ)
    inv = ROPE_BASE ** (-jnp.arange(nf, dtype=F32) * 2.0 / half)
    ar = (t // GRID_W).astype(F32)[:, None] * inv[None, :]
    ac = (t % GRID_W).astype(F32)[:, None] * inv[None, :]
    cos = jnp.concatenate([jnp.cos(ar), jnp.cos(ar), jnp.cos(ac), jnp.cos(ac)], axis=-1)
    sin = jnp.concatenate([-jnp.sin(ar), jnp.sin(ar), -jnp.sin(ac), jnp.sin(ac)], axis=-1)
    ident = jnp.concatenate([jnp.ones((tm, ROPE_DIM), F32), jnp.zeros((tm, ROPE_DIM), F32)], axis=-1)
    return jnp.concatenate([ident, jnp.concatenate([cos, sin], axis=-1)], axis=0)


def _na_bias_table(na_rpb):
    cols = np.arange(GRID_W)
    col_start = np.clip(cols - NA_COLS // 2, 0, GRID_W - NA_COLS)
    col_ok = (cols[None, :] >= col_start[:, None]) & (cols[None, :] < col_start[:, None] + NA_COLS)
    d_col = np.clip(cols[None, :] - cols[:, None] + NA_COLS - 1, 0, 2 * NA_COLS - 2)
    pick_col = (d_col[:, :, None] == np.arange(2 * NA_COLS - 1)[None, None, :]).astype(np.float32)
    tab = jnp.einsum("lhrc,qwc->lhrqw", na_rpb, pick_col, precision=lax.Precision.HIGHEST)
    tab = jnp.where(col_ok[None, None, None], tab, NEG_BIG)
    nxt = jnp.concatenate([tab[:, :, 1:], jnp.full_like(tab[:, :, :1], NEG_BIG)], axis=2)
    return jnp.concatenate([tab, nxt], axis=-1)


def kernel(x_prompt, x_sample, cache_na_k, cache_na_v, cache_mla_ckv, cache_mla_krope, c, c_ctx, w_ada, b_ada,
           w_in, g_q, g_kv, w_uq, w_ukv, na_rpb, sgu_ln_g, sgu_ln_b, sgu_w, sgu_b, w_out, ln1_g, ln1_b, ln2_g,
           ln2_b, w_router, b_router, w_gate, w_up, w_down):
    max_tiles = (T_ALL * TOP_K) // TM_EXP + N_EXPERTS

    cond = jnp.zeros((N_COND, D_MODEL), F32).at[0].set(c_ctx).at[1:1 + DEC_BATCH].set(c)
    mods = _ada(cond, w_ada, b_ada).reshape(DEPTH, N_COND, 6, 1, D_MODEL)

    w_main, w_kr, w_z = _prep_w_in(w_in)
    w_uq_p = _prep_w_uq(w_uq)
    w_ukv_p = _prep_w_ukv(w_ukv)
    w_out_b = w_out.astype(BF16)
    w_router_t = w_router.T.astype(BF16)
    b_router_col = b_router.reshape(N_EXPERTS, 1).astype(F32)
    cs_tab = _rope_table(TM_PROJ)
    bias_tab = _na_bias_table(na_rpb)
    cache_k = cache_na_k.reshape(DEC_BATCH, DEPTH, PAST_LEN, NA_W)
    cache_v = cache_na_v.reshape(DEC_BATCH, DEPTH, PAST_LEN, NA_W)
    kmc, vvc = _ctxkv(cache_mla_ckv, cache_mla_krope, w_ukv_p)

    new_k = jnp.zeros((BATCH, DEPTH, SEQ, NA_W), F32)
    new_v = jnp.zeros((BATCH, DEPTH, SEQ, NA_W), F32)
    new_ckv = jnp.zeros((BATCH, DEPTH, SEQ, KV_LORA), F32)
    new_kr = jnp.zeros((BATCH, DEPTH, SEQ, ROPE_DIM), F32)

    x_ctx = x_prompt.reshape(T_CTX, D_MODEL)
    x_lat = x_sample.reshape(T_LAT, D_MODEL)
    for l in range(DEPTH):
        q, k, v, new_k, new_v, qm, km, vv, new_ckv, new_kr, z = _inproj(
            l, x_ctx, x_lat, mods, w_main, w_kr, w_z, g_q, g_kv, w_uq_p, w_ukv_p, cs_tab, new_k, new_v, new_ckv, new_kr)

        oa_c, ob_c = _ctx_attn(q, k, v, qm, km, vv)
        oa_l = _lat_na(l, q, k, v, cache_k, cache_v, bias_tab)
        ob_l = _lat_mla(l, qm, km, vv, kmc, vvc)
        oc = _sgu(l, z, sgu_ln_g, sgu_ln_b, sgu_w, sgu_b)

        x1, hp, e, rank, gate, cnt = _outproj(l, oa_c, oa_l, ob_c, ob_l, oc, w_out_b, x_ctx, x_lat, mods, ln1_g, ln1_b,
                                              w_router_t, b_router_col)
        pos, tile_expert, next_expert, n_tiles, pad_lo, pad_hi = _dispatch_plan(
            e, rank, cnt[:, 0].astype(jnp.int32), max_tiles)
        pos0, pos1 = pos[0], pos[1]
        xs = _dispatch(pad_lo, pad_hi, n_tiles, pos0, pos1, hp, max_tiles)
        ys = _experts(l, tile_expert, next_expert, n_tiles, xs, w_gate, w_up, w_down, max_tiles)
        gate_t = gate.T
        x_ctx = _tail(l, 0, T_CTX, pos0, pos1, x1, gate_t, mods, ln2_g, ln2_b, ys)
        x_lat = _tail(l, T_CTX, T_LAT, pos0, pos1, x1, gate_t, mods, ln2_g, ln2_b, ys)

    return (x_ctx.reshape(BATCH, SEQ, D_MODEL), x_lat.reshape(DEC_BATCH, DEC_SEQ, D_MODEL),
            new_k.reshape(BATCH, DEPTH, SEQ, NA_HEADS, NA_HEAD_DIM), new_v.reshape(BATCH, DEPTH, SEQ, NA_HEADS, NA_HEAD_DIM),
            new_ckv, new_kr)
```

```python
import functools

import numpy as np
import jax
import jax.numpy as jnp
from jax import lax
from jax.experimental import pallas as pl
from jax.experimental.pallas import tpu as pltpu

F32 = jnp.float32
BF16 = jnp.bfloat16

D_MODEL = 2048
BATCH = 32
SEQ = 256
DEPTH = 2
DEC_BATCH = 2
DEC_SEQ = 2048
PAST_LEN = 256
GRID_W = 64
GRID_R = DEC_SEQ // GRID_W
NA_HEADS = 6
NA_HEAD_DIM = 128
NA_ROWS = 8
NA_COLS = 16
MLA_HEADS = 6
Q_LORA = 512
KV_LORA = 512
NOPE_DIM = 128
ROPE_DIM = 64
V_DIM = 128
ROPE_BASE = 10000.0
MLA_SCALE = (NOPE_DIM + ROPE_DIM) ** -0.5
NA_SCALE = NA_HEAD_DIM ** -0.5
SGU_GROUPS = 4
SGU_DIM = 128
CHUNK = 128
NA_W = NA_HEADS * NA_HEAD_DIM
MLA_W = MLA_HEADS * V_DIM
SGU_W = SGU_GROUPS * SGU_DIM
D_MIX = NA_W + MLA_W + SGU_W
N_EXPERTS = 16
N_GROUPS = 4
EXPERTS_PER_GROUP = N_EXPERTS // N_GROUPS
TOP_K = 2
D_EXPERT = 512
ALPHA = (2 * DEPTH) ** 0.25
LN_EPS = 1e-6
RMS_EPS = 1e-6

T_CTX = BATCH * SEQ
T_LAT = DEC_BATCH * DEC_SEQ
T_ALL = T_CTX + T_LAT
N_COND = 8

C_Q, C_K, C_V = 0, NA_W, 2 * NA_W
C_CQ = 3 * NA_W
C_CKV = C_CQ + Q_LORA
C_KR = C_CKV + KV_LORA
MLA_QK = 256

TM_PROJ = 256
TM_OUT = 512
TM_SGU = 256
TM_TAIL = 256
TM_DISP = 512
TM_EXP = 256
TQ_MLA = 256
MLA_KCHUNK = 512
OUT_SPLIT = 1
NA_QROWS = 4
NA_KROWS = NA_QROWS + NA_ROWS
NEG_BIG = -1e30
VMEM_LIMIT = 56 * 1024 * 1024


def _cparams(sem, vmem=VMEM_LIMIT):
    return pltpu.CompilerParams(dimension_semantics=sem, vmem_limit_bytes=vmem)


def _cond_row(i, tm):
    t0 = i * tm
    return jnp.where(t0 < T_CTX, 0, 1 + (t0 - T_CTX) // DEC_SEQ)


def _ln(x):
    mu = jnp.mean(x, axis=-1, keepdims=True)
    xc = x - mu
    var = jnp.mean(xc * xc, axis=-1, keepdims=True)
    return xc * lax.rsqrt(var + LN_EPS)


def _rms(x):
    return x * lax.rsqrt(jnp.mean(x * x, axis=-1, keepdims=True) + RMS_EPS)


def _dot(a, b):
    return jnp.dot(a, b, preferred_element_type=F32)


def _pack_rows(x):
    xf = x.astype(BF16).astype(F32)
    half = x.shape[-1] // 2
    return pltpu.bitcast(xf[:, :half], jnp.uint32) | (pltpu.bitcast(xf[:, half:], jnp.uint32) >> 16)


def _unpack_rows(w):
    return jnp.concatenate([pltpu.bitcast(w & jnp.uint32(0xFFFF0000), F32), pltpu.bitcast(w << 16, F32)], axis=-1)


def _dot_nt(a, b):
    return lax.dot_general(a, b, (((1,), (1,)), ((), ())), preferred_element_type=F32)


def _ada_kernel(c_ref, w_ref, b_ref, o_ref):
    c = c_ref[...]
    s = c * jax.nn.sigmoid(c)
    o_ref[...] = _dot(s.astype(BF16), w_ref[...].astype(BF16)) + b_ref[...]


def _ada(cond, w_ada, b_ada):
    tn = 1024
    n = 6 * D_MODEL
    return pl.pallas_call(
        _ada_kernel,
        grid=(DEPTH, n // tn),
        in_specs=[
            pl.BlockSpec((N_COND, D_MODEL), lambda l, j: (0, 0)),
            pl.BlockSpec((None, D_MODEL, tn), lambda l, j: (l, 0, j)),
            pl.BlockSpec((None, 1, tn), lambda l, j: (l, 0, j)),
        ],
        out_specs=pl.BlockSpec((None, N_COND, tn), lambda l, j: (l, 0, j)),
        out_shape=jax.ShapeDtypeStruct((DEPTH, N_COND, n), F32),
        compiler_params=_cparams(("parallel", "parallel")),
        name="ada_mod",
    )(cond, w_ada, b_ada.reshape(DEPTH, 1, n))


def _inproj_kernel(xc_ref, xl_ref, sh_ref, sc_ref, w_ref, wkr_ref, wz_ref, gq_ref, gkv_ref, wuq_ref, wukv_ref, cs_ref,
                   k32_in, v32_in, ckv32_in, kr32_in,
                   q_ref, k_ref, v_ref, k32_ref, v32_ref, qm_ref, km_ref, vv_ref,
                   ckv32_ref, kr32_ref, z_ref):
    del k32_in, v32_in, ckv32_in, kr32_in
    i = pl.program_id(0)
    is_ctx = i * TM_PROJ < T_CTX
    x = jnp.where(is_ctx, xc_ref[...], xl_ref[...])
    h = (_ln(x) * (1.0 + sc_ref[...]) + sh_ref[...]).astype(BF16)

    def proj(a, b):
        return _dot(h, w_ref[:, a:b])

    q_ref[...] = proj(C_Q, C_K).astype(BF16)
    k = proj(C_K, C_V)
    k_ref[...] = k.astype(BF16)
    v = proj(C_V, C_CQ)
    v_ref[...] = v.astype(BF16)

    cs = cs_ref[...]
    lane = lax.broadcasted_iota(jnp.int32, cs.shape, 1)

    def rotate(pair):
        t = pair * cs
        return jnp.where(lane < ROPE_DIM, t + pltpu.roll(t, ROPE_DIM, 1), 0.0)

    cqn = (_rms(proj(C_CQ, C_CKV)) * gq_ref[...]).astype(BF16)
    mq = _dot(cqn, wuq_ref[...])
    for hd in range(MLA_HEADS):
        c0 = hd * MLA_QK
        qm_ref[:, c0:c0 + NOPE_DIM] = mq[:, c0:c0 + NOPE_DIM].astype(BF16)
        qm_ref[:, c0 + NOPE_DIM:c0 + MLA_QK] = rotate(mq[:, c0 + NOPE_DIM:c0 + MLA_QK]).astype(BF16)

    ckvn = _rms(proj(C_CKV, C_KR)) * gkv_ref[...]
    kv = _dot(ckvn.astype(BF16), wukv_ref[...])
    kr2 = _dot(h, wkr_ref[...])
    krot = rotate(kr2).astype(BF16)
    for hd in range(MLA_HEADS):
        c0 = hd * MLA_QK
        km_ref[:, c0:c0 + NOPE_DIM] = kv[:, hd * NOPE_DIM:(hd + 1) * NOPE_DIM].astype(BF16)
        km_ref[:, c0 + NOPE_DIM:c0 + MLA_QK] = krot
    vv_ref[...] = kv[:, MLA_HEADS * NOPE_DIM:].astype(BF16)

    z_ref[...] = _dot(h, wz_ref[...])

    @pl.when(is_ctx)
    def _():
        k32_ref[...] = k
        v32_ref[...] = v
        ckv32_ref[...] = ckvn
        kr32_ref[...] = kr2[:, :ROPE_DIM]


def _inproj(l, x_ctx, x_lat, mods, w_main, w_kr, w_z, g_q, g_kv, w_uq_p, w_ukv_p, cs_tab, new_k, new_v, new_ckv, new_kr):
    tm = TM_PROJ
    assert tm == SEQ
    nct = T_CTX // tm
    per_b = DEC_SEQ // tm

    def row(i):
        return (i, 0)

    def ctx_row(i):
        return (jnp.minimum(i, nct - 1), 0)

    def lat_row(i):
        return (jnp.maximum(i - nct, 0), 0)

    def cs_row(i):
        return (jnp.where(i < nct, 0, 1 + (i - nct) % per_b), 0)

    def mod(kind):
        return pl.BlockSpec((None, None, None, 1, D_MODEL), lambda i: (l, _cond_row(i, tm), kind, 0, 0))

    def const2(shape):
        return pl.BlockSpec(shape, lambda i: (l, 0, 0), pipeline_mode=pl.Buffered(1))

    def cache(w):
        return pl.BlockSpec((None, None, SEQ, w), lambda i: (jnp.minimum(i, nct - 1), l, 0, 0))

    anyspec = pl.BlockSpec(memory_space=pl.ANY)
    out_shapes = [
        jax.ShapeDtypeStruct((T_ALL, NA_W), BF16),
        jax.ShapeDtypeStruct((T_ALL, NA_W), BF16),
        jax.ShapeDtypeStruct((T_ALL, NA_W), BF16),
        jax.ShapeDtypeStruct(new_k.shape, F32),
        jax.ShapeDtypeStruct(new_v.shape, F32),
        jax.ShapeDtypeStruct((T_ALL, MLA_HEADS * MLA_QK), BF16),
        jax.ShapeDtypeStruct((T_ALL, MLA_HEADS * MLA_QK), BF16),
        jax.ShapeDtypeStruct((T_ALL, MLA_W), BF16),
        jax.ShapeDtypeStruct(new_ckv.shape, F32),
        jax.ShapeDtypeStruct(new_kr.shape, F32),
        jax.ShapeDtypeStruct((T_ALL, 2 * SGU_W), F32),
    ]
    out_specs = [
        pl.BlockSpec((tm, NA_W), row), pl.BlockSpec((tm, NA_W), row), pl.BlockSpec((tm, NA_W), row),
        cache(NA_W), cache(NA_W),
        pl.BlockSpec((tm, MLA_HEADS * MLA_QK), row), pl.BlockSpec((tm, MLA_HEADS * MLA_QK), row),
        pl.BlockSpec((tm, MLA_W), row),
        cache(KV_LORA), cache(ROPE_DIM),
        pl.BlockSpec((tm, 2 * SGU_W), row),
    ]
    return pl.pallas_call(
        _inproj_kernel,
        grid=(T_ALL // tm,),
        in_specs=[
            pl.BlockSpec((tm, D_MODEL), ctx_row), pl.BlockSpec((tm, D_MODEL), lat_row),
            mod(0), mod(1),
            const2((None, D_MODEL, C_KR)), const2((None, D_MODEL, 2 * ROPE_DIM)), const2((None, D_MODEL, 2 * SGU_W)),
            pl.BlockSpec((None, 1, Q_LORA), lambda i: (l, 0, 0)),
            pl.BlockSpec((None, 1, KV_LORA), lambda i: (l, 0, 0)),
            const2((None, Q_LORA, MLA_HEADS * MLA_QK)),
            const2((None, KV_LORA, 2 * MLA_W)),
            pl.BlockSpec((tm, 2 * ROPE_DIM), cs_row),
            anyspec, anyspec, anyspec, anyspec,
        ],
        out_specs=out_specs,
        out_shape=out_shapes,
        input_output_aliases={12: 3, 13: 4, 14: 8, 15: 9},
        compiler_params=_cparams(("arbitrary",)),
        name="in_proj",
    )(x_ctx, x_lat, mods, mods, w_main, w_kr, w_z, g_q.reshape(DEPTH, 1, Q_LORA), g_kv.reshape(DEPTH, 1, KV_LORA),
      w_uq_p, w_ukv_p, cs_tab, new_k, new_v, new_ckv, new_kr)


def _ctxkv_kernel(ckv_ref, kr_ref, w_ref, km_ref, vv_ref):
    kv = _dot(ckv_ref[...].astype(BF16), w_ref[...])
    kr = jnp.concatenate([kr_ref[...], jnp.zeros((PAST_LEN, ROPE_DIM), F32)], axis=-1).astype(BF16)
    for hd in range(MLA_HEADS):
        c0 = hd * MLA_QK
        km_ref[:, c0:c0 + NOPE_DIM] = kv[:, hd * NOPE_DIM:(hd + 1) * NOPE_DIM].astype(BF16)
        km_ref[:, c0 + NOPE_DIM:c0 + MLA_QK] = kr
    vv_ref[...] = kv[:, MLA_HEADS * NOPE_DIM:].astype(BF16)


def _ctxkv(cache_ckv, cache_krope, w_ukv_p):
    return pl.pallas_call(
        _ctxkv_kernel,
        grid=(DEPTH, DEC_BATCH),
        in_specs=[
            pl.BlockSpec((None, None, PAST_LEN, KV_LORA), lambda l, b: (b, l, 0, 0)),
            pl.BlockSpec((None, None, PAST_LEN, ROPE_DIM), lambda l, b: (b, l, 0, 0)),
            pl.BlockSpec((None, KV_LORA, 2 * MLA_W), lambda l, b: (l, 0, 0)),
        ],
        out_specs=[
            pl.BlockSpec((None, None, PAST_LEN, MLA_HEADS * MLA_QK), lambda l, b: (l, b, 0, 0)),
            pl.BlockSpec((None, None, PAST_LEN, MLA_W), lambda l, b: (l, b, 0, 0)),
        ],
        out_shape=[
            jax.ShapeDtypeStruct((DEPTH, DEC_BATCH, PAST_LEN, MLA_HEADS * MLA_QK), BF16),
            jax.ShapeDtypeStruct((DEPTH, DEC_BATCH, PAST_LEN, MLA_W), BF16),
        ],
        compiler_params=_cparams(("parallel", "parallel")),
        name="ctx_cache_kv",
    )(cache_ckv, cache_krope, w_ukv_p)


def _softmax_pv(blocks):
    ss = [s for s, _ in blocks]
    m = ss[0].max(axis=-1, keepdims=True)
    for s in ss[1:]:
        m = jnp.maximum(m, s.max(axis=-1, keepdims=True))
    num = None
    den = None
    for s, (_, v) in zip(ss, blocks):
        e = jnp.exp(s - m)
        d = e.sum(axis=-1, keepdims=True)
        o = _dot(e.astype(BF16), v)
        num = o if num is None else num + o
        den = d if den is None else den + d
    return num / den


def _ctx_attn_kernel(q_ref, k_ref, v_ref, qm_ref, km_ref, vv_ref, oa_ref, ob_ref):
    for hd in range(NA_HEADS):
        sl = slice(hd * NA_HEAD_DIM, (hd + 1) * NA_HEAD_DIM)
        s = _dot_nt(q_ref[:, sl], k_ref[:, sl]) * NA_SCALE
        oa_ref[:, sl] = _softmax_pv([(s, v_ref[:, sl])]).astype(BF16)
    for hd in range(MLA_HEADS):
        sq = slice(hd * MLA_QK, (hd + 1) * MLA_QK)
        sv = slice(hd * V_DIM, (hd + 1) * V_DIM)
        s = _dot_nt(qm_ref[:, sq], km_ref[:, sq]) * MLA_SCALE
        ob_ref[:, sv] = _softmax_pv([(s, vv_ref[:, sv])]).astype(BF16)


def _ctx_attn(q, k, v, qm, km, vv):
    def spec(w):
        return pl.BlockSpec((SEQ, w), lambda b: (b, 0))

    return pl.pallas_call(
        _ctx_attn_kernel,
        grid=(BATCH,),
        in_specs=[spec(NA_W), spec(NA_W), spec(NA_W),
                  spec(MLA_HEADS * MLA_QK), spec(MLA_HEADS * MLA_QK), spec(MLA_W)],
        out_specs=[spec(NA_W), spec(MLA_W)],
        out_shape=[jax.ShapeDtypeStruct((T_CTX, NA_W), BF16), jax.ShapeDtypeStruct((T_CTX, MLA_W), BF16)],
        compiler_params=_cparams(("parallel",)),
        name="ctx_attn",
    )(q, k, v, qm, km, vv)


def _lat_mla_kernel(q_ref, k_ref, v_ref, kc_ref, vc_ref, o_ref):
    q = q_ref[...]
    blocks = []
    for c in range(DEC_SEQ // MLA_KCHUNK):
        ks = slice(c * MLA_KCHUNK, (c + 1) * MLA_KCHUNK)
        blocks.append((_dot_nt(q, k_ref[ks, :]) * MLA_SCALE, v_ref[ks, :]))
    blocks.append((_dot_nt(q, kc_ref[...]) * MLA_SCALE, vc_ref[...]))
    o_ref[...] = _softmax_pv(blocks).astype(BF16)


def _lat_mla(l, qm, km, vv, kmc, vvc):
    tq = TQ_MLA
    nq = DEC_SEQ // tq
    row0 = T_CTX // tq
    kb0 = T_CTX // DEC_SEQ
    return pl.pallas_call(
        _lat_mla_kernel,
        grid=(DEC_BATCH, MLA_HEADS, nq),
        in_specs=[
            pl.BlockSpec((tq, MLA_QK), lambda b, h, t: (row0 + b * nq + t, h)),
            pl.BlockSpec((DEC_SEQ, MLA_QK), lambda b, h, t: (kb0 + b, h)),
            pl.BlockSpec((DEC_SEQ, V_DIM), lambda b, h, t: (kb0 + b, h)),
            pl.BlockSpec((None, None, PAST_LEN, MLA_QK), lambda b, h, t: (l, b, 0, h)),
            pl.BlockSpec((None, None, PAST_LEN, V_DIM), lambda b, h, t: (l, b, 0, h)),
        ],
        out_specs=pl.BlockSpec((tq, V_DIM), lambda b, h, t: (b * nq + t, h)),
        out_shape=jax.ShapeDtypeStruct((T_LAT, MLA_W), BF16),
        compiler_params=_cparams(("parallel", "parallel", "arbitrary")),
        name="lat_mla",
    )(qm, km, vv, kmc, vvc)


def _na_block_plan():
    plan = []
    for r0 in range(0, GRID_R, NA_QROWS):
        ws = min(max(r0 - NA_ROWS // 2, 0), GRID_R - NA_KROWS)
        rows = []
        for r in range(r0, r0 + NA_QROWS):
            rs = min(max(r - NA_ROWS // 2, 0), GRID_R - NA_ROWS)
            rel = [kr - r + NA_ROWS - 1 if rs <= kr < rs + NA_ROWS else None for kr in range(ws, ws + NA_KROWS)]
            rows.append([(rel[j], rel[j + 1]) for j in range(0, NA_KROWS, 2)])
        plan.append((ws, rows))
    return plan


def _lat_na_kernel(q_ref, k_ref, v_ref, ck_ref, cv_ref, tab_ref, o_ref):
    ck = ck_ref[...].astype(BF16)
    cv = cv_ref[...].astype(BF16)
    nq = NA_QROWS * GRID_W
    nk = NA_KROWS * GRID_W
    lane = lax.broadcasted_iota(jnp.int32, (GRID_W, 2 * GRID_W), 1)
    masked = jnp.full((GRID_W, 2 * GRID_W), NEG_BIG, F32)

    def pair_bias(d0, d1):
        if d0 is None and d1 is None:
            return masked
        if d1 is None:
            return jnp.where(lane < GRID_W, tab_ref[d0], NEG_BIG)
        if d0 is None:
            return jnp.where(lane >= GRID_W, tab_ref[d1 - 1], NEG_BIG)
        return tab_ref[d0]

    for blk, (ws, rows) in enumerate(_na_block_plan()):
        q = q_ref[blk * nq:(blk + 1) * nq, :]
        kw = k_ref[ws * GRID_W:ws * GRID_W + nk, :]
        vw = v_ref[ws * GRID_W:ws * GRID_W + nk, :]
        bias = jnp.concatenate(
            [jnp.concatenate([pair_bias(d0, d1) for d0, d1 in pairs], axis=1) for pairs in rows], axis=0)
        s1 = _dot_nt(q, kw) * NA_SCALE + bias
        s2 = _dot_nt(q, ck) * NA_SCALE
        o_ref[blk * nq:(blk + 1) * nq, :] = _softmax_pv([(s1, vw), (s2, cv)]).astype(BF16)


def _lat_na(l, q, k, v, cache_k, cache_v, bias_tab):
    kb0 = T_CTX // DEC_SEQ

    def lat(b, h):
        return (kb0 + b, h)

    return pl.pallas_call(
        _lat_na_kernel,
        grid=(DEC_BATCH, NA_HEADS),
        in_specs=[
            pl.BlockSpec((DEC_SEQ, NA_HEAD_DIM), lat),
            pl.BlockSpec((DEC_SEQ, NA_HEAD_DIM), lat),
            pl.BlockSpec((DEC_SEQ, NA_HEAD_DIM), lat),
            pl.BlockSpec((None, None, PAST_LEN, NA_HEAD_DIM), lambda b, h: (b, l, 0, h)),
            pl.BlockSpec((None, None, PAST_LEN, NA_HEAD_DIM), lambda b, h: (b, l, 0, h)),
            pl.BlockSpec((None, None, 2 * NA_ROWS - 1, GRID_W, 2 * GRID_W), lambda b, h: (l, h, 0, 0, 0)),
        ],
        out_specs=pl.BlockSpec((DEC_SEQ, NA_HEAD_DIM), lambda b, h: (b, h)),
        out_shape=jax.ShapeDtypeStruct((T_LAT, NA_W), BF16),
        compiler_params=_cparams(("parallel", "parallel")),
        name="lat_na",
    )(q, k, v, cache_k, cache_v, bias_tab)


def _sgu_kernel(z_ref, g_ref, b_ref, w_ref, bs_ref, o_ref):
    z = z_ref[...]
    a = 0.5 * z * (1.0 + lax.erf(z * (0.5 ** 0.5)))
    for c in range(TM_SGU // CHUNK):
        rows = slice(c * CHUNK, (c + 1) * CHUNK)
        for g in range(SGU_GROUPS):
            u = a[rows, g * SGU_DIM:(g + 1) * SGU_DIM]
            v = a[rows, SGU_W + g * SGU_DIM:SGU_W + (g + 1) * SGU_DIM]
            vn = _ln(v) * g_ref[g:g + 1, :] + b_ref[g:g + 1, :]
            t = _dot(w_ref[g].astype(BF16), vn.astype(BF16)) + bs_ref[g]
            o_ref[rows, g * SGU_DIM:(g + 1) * SGU_DIM] = (u * t).astype(BF16)


def _sgu(l, z, ln_g, ln_b, w_s, b_s):
    tm = TM_SGU
    return pl.pallas_call(
        _sgu_kernel,
        grid=(T_ALL // tm,),
        in_specs=[
            pl.BlockSpec((tm, 2 * SGU_W), lambda i: (i, 0)),
            pl.BlockSpec((None, SGU_GROUPS, SGU_DIM), lambda i: (l, 0, 0)),
            pl.BlockSpec((None, SGU_GROUPS, SGU_DIM), lambda i: (l, 0, 0)),
            pl.BlockSpec((None, SGU_GROUPS, CHUNK, CHUNK), lambda i: (l, 0, 0, 0)),
            pl.BlockSpec((None, SGU_GROUPS, CHUNK, 1), lambda i: (l, 0, 0, 0)),
        ],
        out_specs=pl.BlockSpec((tm, SGU_W), lambda i: (i, 0)),
        out_shape=jax.ShapeDtypeStruct((T_ALL, SGU_W), BF16),
        compiler_params=_cparams(("parallel",)),
        name="sgu",
    )(z, ln_g, ln_b, w_s, b_s.reshape(DEPTH, SGU_GROUPS, CHUNK, 1))


def _route_rows(logits_t, bias_col):
    s = jax.nn.sigmoid(logits_t)
    sb = s + bias_col
    rows = [sb[r:r + 1, :] for r in range(N_EXPERTS)]
    srows = [s[r:r + 1, :] for r in range(N_EXPERTS)]
    best, gsel = None, None
    for g in range(N_GROUPS):
        m = rows[g * EXPERTS_PER_GROUP:(g + 1) * EXPERTS_PER_GROUP]
        score = None
        for a in range(EXPERTS_PER_GROUP):
            for b in range(a + 1, EXPERTS_PER_GROUP):
                pair = m[a] + m[b]
                score = pair if score is None else jnp.maximum(score, pair)
        if best is None:
            best, gsel = score, jnp.zeros(score.shape, jnp.int32)
        else:
            upd = score > best
            best = jnp.where(upd, score, best)
            gsel = jnp.where(upd, g, gsel)

    def pick(table, j):
        out = table[j]
        for g in range(1, N_GROUPS):
            out = jnp.where(gsel == g, table[g * EXPERTS_PER_GROUP + j], out)
        return out

    v = [pick(rows, j) for j in range(EXPERTS_PER_GROUP)]
    sv = [pick(srows, j) for j in range(EXPERTS_PER_GROUP)]
    m1, i1, w1 = v[0], jnp.zeros(gsel.shape, jnp.int32), sv[0]
    for j in range(1, EXPERTS_PER_GROUP):
        upd = v[j] > m1
        m1 = jnp.where(upd, v[j], m1)
        i1 = jnp.where(upd, j, i1)
        w1 = jnp.where(upd, sv[j], w1)
    m2 = jnp.full(gsel.shape, -jnp.inf, F32)
    i2 = jnp.zeros(gsel.shape, jnp.int32)
    w2 = jnp.zeros(gsel.shape, F32)
    for j in range(EXPERTS_PER_GROUP):
        upd = jnp.logical_and(i1 != j, v[j] > m2)
        m2 = jnp.where(upd, v[j], m2)
        i2 = jnp.where(upd, j, i2)
        w2 = jnp.where(upd, sv[j], w2)
    den = w1 + w2
    return gsel * EXPERTS_PER_GROUP + i1, gsel * EXPERTS_PER_GROUP + i2, w1 / den, w2 / den


def _outproj_kernel(ac_ref, al_ref, bc_ref, bl_ref, c_ref, wa_ref, wb_ref, wc_ref, xc_ref, xl_ref, g1_ref, sh2_ref, sc2_ref,
                    lg_ref, lb_ref, wrt_ref, br_ref, x1_ref, hp_ref, e_ref, rank_ref, gate_ref, cnt_ref, base_ref):
    is_ctx = pl.program_id(0) * TM_OUT < T_CTX

    @pl.when(pl.program_id(0) == 0)
    def _():
        base_ref[...] = jnp.zeros_like(base_ref)

    ts = TM_OUT // OUT_SPLIT
    sub = lax.broadcasted_iota(jnp.int32, (N_EXPERTS, ts), 0)
    before = (lax.broadcasted_iota(jnp.int32, (ts, ts), 0) < lax.broadcasted_iota(jnp.int32, (ts, ts), 1)).astype(BF16)
    base = base_ref[:, 0:1]
    for part in range(OUT_SPLIT):
        rows = slice(part * ts, (part + 1) * ts)
        a = jnp.where(is_ctx, ac_ref[rows, :], al_ref[rows, :])
        b = jnp.where(is_ctx, bc_ref[rows, :], bl_ref[rows, :])
        x = jnp.where(is_ctx, xc_ref[rows, :], xl_ref[rows, :])
        mixed = _dot(a, wa_ref[...]) + _dot(b, wb_ref[...]) + _dot(c_ref[rows, :], wc_ref[...])
        x1 = _ln(ALPHA * x + g1_ref[...] * mixed) * lg_ref[...] + lb_ref[...]
        x1_ref[rows, :] = x1
        h2 = (_ln(x1) * (1.0 + sc2_ref[...]) + sh2_ref[...]).astype(BF16)

        hp_ref[rows, :] = _pack_rows(h2)

        e1, e2, w1, w2 = _route_rows(_dot_nt(wrt_ref[...], h2), br_ref[...])
        e_ref[0:1, rows] = e1
        e_ref[1:2, rows] = e2
        gate_ref[0:1, rows] = w1
        gate_ref[1:2, rows] = w2

        o1 = (sub == e1).astype(F32)
        o2 = (sub == e2).astype(F32)
        p1 = _dot(o1.astype(BF16), before)
        p2 = _dot(o2.astype(BF16), before)
        c1 = o1.sum(axis=1, keepdims=True)
        c2 = o2.sum(axis=1, keepdims=True)
        rank_ref[0:1, rows] = jnp.sum(o1 * (base + p1), axis=0, keepdims=True).astype(jnp.int32)
        rank_ref[1:2, rows] = jnp.sum(o2 * (base + c1 + p2), axis=0, keepdims=True).astype(jnp.int32)
        base = base + (c1 + c2)
    base_ref[...] = jnp.broadcast_to(base, base_ref.shape)
    cnt_ref[...] = base_ref[...]


def _outproj(l, oa_c, oa_l, ob_c, ob_l, oc, w_out_b, x_ctx, x_lat, mods, ln_g, ln_b, w_router_t, b_router_col):
    tm = TM_OUT
    nct = T_CTX // tm

    def row(i):
        return (i, 0)

    def ctx_row(i):
        return (jnp.minimum(i, nct - 1), 0)

    def lat_row(i):
        return (jnp.maximum(i - nct, 0), 0)

    def tok(i):
        return (0, i)

    def mod(kind):
        return pl.BlockSpec((None, None, None, 1, D_MODEL), lambda i: (l, _cond_row(i, tm), kind, 0, 0))

    def wblk(rows, blk):
        return pl.BlockSpec((None, rows, D_MODEL), lambda i: (l, blk, 0), pipeline_mode=pl.Buffered(1))

    vec = pl.BlockSpec((None, 1, D_MODEL), lambda i: (l, 0, 0))
    return pl.pallas_call(
        _outproj_kernel,
        grid=(T_ALL // tm,),
        in_specs=[
            pl.BlockSpec((tm, NA_W), ctx_row), pl.BlockSpec((tm, NA_W), lat_row),
            pl.BlockSpec((tm, MLA_W), ctx_row), pl.BlockSpec((tm, MLA_W), lat_row),
            pl.BlockSpec((tm, SGU_W), row),
            wblk(NA_W, 0), wblk(MLA_W, 1), wblk(SGU_W, (NA_W + MLA_W) // SGU_W),
            pl.BlockSpec((tm, D_MODEL), ctx_row), pl.BlockSpec((tm, D_MODEL), lat_row),
            mod(2), mod(3), mod(4), vec, vec,
            pl.BlockSpec((N_EXPERTS, D_MODEL), lambda i: (0, 0)),
            pl.BlockSpec((N_EXPERTS, 1), lambda i: (0, 0)),
        ],
        out_specs=[pl.BlockSpec((tm, D_MODEL), row), pl.BlockSpec((tm, D_MODEL // 2), row),
                   pl.BlockSpec((TOP_K, tm), tok), pl.BlockSpec((TOP_K, tm), tok), pl.BlockSpec((TOP_K, tm), tok),
                   pl.BlockSpec((N_EXPERTS, 128), lambda i: (0, 0))],
        out_shape=[jax.ShapeDtypeStruct((T_ALL, D_MODEL), F32),
                   jax.ShapeDtypeStruct((T_ALL, D_MODEL // 2), jnp.uint32),
                   jax.ShapeDtypeStruct((TOP_K, T_ALL), jnp.int32),
                   jax.ShapeDtypeStruct((TOP_K, T_ALL), jnp.int32),
                   jax.ShapeDtypeStruct((TOP_K, T_ALL), F32),
                   jax.ShapeDtypeStruct((N_EXPERTS, 128), F32)],
        scratch_shapes=[pltpu.VMEM((N_EXPERTS, 128), F32)],
        compiler_params=_cparams(("arbitrary",)),
        name="out_proj",
    )(oa_c, oa_l, ob_c, ob_l, oc, w_out_b, w_out_b, w_out_b, x_ctx, x_lat, mods, mods, mods,
      ln_g.reshape(DEPTH, 1, D_MODEL), ln_b.reshape(DEPTH, 1, D_MODEL), w_router_t, b_router_col)


def _dispatch_plan(e, rank, counts, max_tiles):
    tiles = (counts + TM_EXP - 1) // TM_EXP
    tile_end = jnp.cumsum(tiles)
    off = (tile_end - tiles) * TM_EXP
    n_tiles = tile_end[-1]
    ids = jnp.arange(N_EXPERTS, dtype=jnp.int32)
    pos = rank + jnp.sum(jnp.where(e[..., None] == ids, off, 0), axis=-1)
    ti = jnp.arange(max_tiles, dtype=jnp.int32)
    owner = jnp.sum((tile_end[None, :] <= ti[:, None]).astype(jnp.int32), axis=1)
    last = jnp.sum((tile_end <= n_tiles - 1).astype(jnp.int32))
    tile_expert = jnp.where(ti < n_tiles, owner, last).astype(jnp.int32)
    later_used = jnp.logical_and(ids[None, :] > ids[:, None], tiles[None, :] > 0)
    next_expert = jnp.min(jnp.where(later_used, ids[None, :], N_EXPERTS), axis=1)
    next_expert = jnp.where(next_expert < N_EXPERTS, next_expert, -1).astype(jnp.int32)
    pad_lo = (off + counts).astype(jnp.int32)
    pad_hi = (off + tiles * TM_EXP).astype(jnp.int32)
    return pos.astype(jnp.int32), tile_expert, next_expert, n_tiles.reshape(1).astype(jnp.int32), pad_lo, pad_hi


def _row_copy(src_ref, src_row, dst_ref, dst_row, sem):
    return pltpu.make_async_copy(src_ref.at[pl.ds(src_row, 1)], dst_ref.at[pl.ds(dst_row, 1)], sem)


def _dispatch_kernel(lo_ref, hi_ref, nt_ref, p0_ref, p1_ref, hp_ref, xs_ref, zeros, sem, *, max_tiles):
    @pl.when(pl.program_id(0) == 0)
    def _():
        zeros[...] = jnp.zeros_like(zeros)

        def fill(r, carry):
            _row_copy(zeros, 0, xs_ref, r, sem).start()
            return carry

        def fill_done(r, carry):
            _row_copy(zeros, 0, xs_ref, 0, sem).wait()
            return carry

        for ex in range(N_EXPERTS):
            lax.fori_loop(lo_ref[ex], hi_ref[ex], fill, 0)
        for ex in range(N_EXPERTS):
            lax.fori_loop(lo_ref[ex], hi_ref[ex], fill_done, 0)

        def tile_copy(t):
            return pltpu.make_async_copy(zeros, xs_ref.at[pl.ds(t * TM_EXP, TM_EXP)], sem)

        def fill_tile(t, carry):
            tile_copy(t).start()
            return carry

        def fill_tile_done(t, carry):
            tile_copy(0).wait()
            return carry

        lax.fori_loop(nt_ref[0], max_tiles, fill_tile, 0)
        lax.fori_loop(nt_ref[0], max_tiles, fill_tile_done, 0)

    def issue(t, carry):
        for k, p_ref in enumerate((p0_ref, p1_ref)):
            _row_copy(hp_ref, t, xs_ref, p_ref[t], sem).start(priority=k)
        return carry

    lax.fori_loop(0, TM_DISP, issue, 0, unroll=8)

    def drain(t, carry):
        for k in range(TOP_K):
            _row_copy(hp_ref, 0, xs_ref, 0, sem).wait()
        return carry

    lax.fori_loop(0, TM_DISP, drain, 0, unroll=8)


def _dispatch(pad_lo, pad_hi, n_tiles, pos0, pos1, hp, max_tiles):
    grid_spec = pltpu.PrefetchScalarGridSpec(
        num_scalar_prefetch=3,
        grid=(T_ALL // TM_DISP,),
        in_specs=[
            pl.BlockSpec((TM_DISP,), lambda i, lo, hi, nt: (i,), memory_space=pltpu.SMEM),
            pl.BlockSpec((TM_DISP,), lambda i, lo, hi, nt: (i,), memory_space=pltpu.SMEM),
            pl.BlockSpec((TM_DISP, D_MODEL // 2), lambda i, lo, hi, nt: (i, 0)),
        ],
        out_specs=pl.BlockSpec(memory_space=pl.ANY),
        scratch_shapes=[pltpu.VMEM((TM_EXP, D_MODEL // 2), jnp.uint32), pltpu.SemaphoreType.DMA(())],
    )
    return pl.pallas_call(
        functools.partial(_dispatch_kernel, max_tiles=max_tiles),
        grid_spec=grid_spec,
        out_shape=jax.ShapeDtypeStruct((max_tiles * TM_EXP, D_MODEL // 2), jnp.uint32),
        compiler_params=_cparams(("arbitrary",)),
        name="dispatch",
    )(pad_lo, pad_hi, n_tiles, pos0, pos1, hp)


def _expert_kernel(te_ref, nxt_ref, nt_ref, x_ref, wg_hbm, wu_hbm, wd_hbm, y_ref,
                   wg_f, wu_f, wd_f, wg_s, wu_s, wd_s, switches, sem, *, layer):
    i = pl.program_id(0)
    e = te_ref[i]

    def weight_copies(expert, slot):
        return [pltpu.make_async_copy(w_hbm.at[layer, expert], buf.at[slot], sem.at[slot, j])
                for j, (w_hbm, buf) in enumerate(((wg_hbm, wg_f), (wu_hbm, wu_f), (wd_hbm, wd_f)))]

    @pl.when(i == 0)
    def _():
        for c in weight_copies(e, 0):
            c.start()

    fresh = jnp.logical_or(i == 0, e != te_ref[jnp.maximum(i - 1, 0)])

    @pl.when(fresh)
    def _():
        n_sw = jnp.where(i == 0, 0, switches[0] + 1)
        switches[0] = n_sw
        slot = n_sw % 2
        for c in weight_copies(e, slot):
            c.wait()
        wg_s[...] = wg_f[slot].astype(BF16)
        wu_s[...] = wu_f[slot].astype(BF16)
        wd_s[...] = wd_f[slot].astype(BF16)
        nx = nxt_ref[e]

        @pl.when(nx >= 0)
        def _():
            for c in weight_copies(nx, 1 - slot):
                c.start()

    @pl.when(i < nt_ref[0])
    def _():
        x = _unpack_rows(x_ref[...]).astype(BF16)
        g = _dot(x, wg_s[...])
        u = _dot(x, wu_s[...])
        a = (g * jax.nn.sigmoid(g) * u).astype(BF16)
        y_ref[...] = _pack_rows(_dot(a, wd_s[...]))

    @pl.when(i >= nt_ref[0])
    def _():
        y_ref[...] = jnp.zeros_like(y_ref)


def _experts(l, tile_expert, next_expert, n_tiles, xs, w_gate, w_up, w_down, max_tiles):
    tm = TM_EXP
    anyspec = pl.BlockSpec(memory_space=pl.ANY)
    grid_spec = pltpu.PrefetchScalarGridSpec(
        num_scalar_prefetch=3,
        grid=(max_tiles,),
        in_specs=[pl.BlockSpec((tm, D_MODEL // 2), lambda i, te, nx, nt: (i, 0)), anyspec, anyspec, anyspec],
        out_specs=pl.BlockSpec((tm, D_MODEL // 2), lambda i, te, nx, nt: (i, 0)),
        scratch_shapes=[pltpu.VMEM((2, D_MODEL, D_EXPERT), F32), pltpu.VMEM((2, D_MODEL, D_EXPERT), F32),
                        pltpu.VMEM((2, D_EXPERT, D_MODEL), F32),
                        pltpu.VMEM((D_MODEL, D_EXPERT), BF16), pltpu.VMEM((D_MODEL, D_EXPERT), BF16),
                        pltpu.VMEM((D_EXPERT, D_MODEL), BF16),
                        pltpu.SMEM((1,), jnp.int32), pltpu.SemaphoreType.DMA((2, 3))],
    )
    return pl.pallas_call(
        functools.partial(_expert_kernel, layer=l),
        grid_spec=grid_spec,
        out_shape=jax.ShapeDtypeStruct((max_tiles * tm, D_MODEL // 2), jnp.uint32),
        compiler_params=_cparams(("arbitrary",)),
        name="experts",
    )(tile_expert, next_expert, n_tiles, xs, w_gate, w_up, w_down)


def _tail_kernel(p0_ref, p1_ref, p0n_ref, p1n_ref, x1_ref, gate_ref, g2_ref, lg_ref, lb_ref, ys_ref, o_ref, ybuf, sem,
                 *, n_tiles):
    tm = TM_TAIL
    i = pl.program_id(0)
    slot = i % 2

    def issue(p_refs, s):
        def body(t, carry):
            for k, p_ref in enumerate(p_refs):
                _row_copy(ys_ref, p_ref[t], ybuf.at[s, k], t, sem.at[s]).start(priority=k)
            return carry

        lax.fori_loop(0, tm, body, 0, unroll=8)

    @pl.when(i == 0)
    def _():
        issue((p0_ref, p1_ref), 0)

    @pl.when(i + 1 < n_tiles)
    def _():
        issue((p0n_ref, p1n_ref), 1 - slot)

    def drain(t, carry):
        for k in range(TOP_K):
            _row_copy(ys_ref, 0, ybuf.at[slot, k], 0, sem.at[slot]).wait()
        return carry

    lax.fori_loop(0, tm, drain, 0, unroll=8)

    gate = gate_ref[...]
    y = gate[:, 0:1] * _unpack_rows(ybuf[slot, 0]) + gate[:, 1:2] * _unpack_rows(ybuf[slot, 1])
    o_ref[...] = _ln(ALPHA * x1_ref[...] + g2_ref[...] * y) * lg_ref[...] + lb_ref[...]


def _tail(l, t_start, t_count, pos0, pos1, x1, gate_t, mods, ln_g, ln_b, ys):
    tm = TM_TAIL
    n = t_count // tm
    i0 = t_start // tm
    vec = pl.BlockSpec((None, 1, D_MODEL), lambda i: (l, 0, 0))
    return pl.pallas_call(
        functools.partial(_tail_kernel, n_tiles=n),
        grid=(n,),
        in_specs=[
            pl.BlockSpec((tm,), lambda i: (i0 + i,), memory_space=pltpu.SMEM),
            pl.BlockSpec((tm,), lambda i: (i0 + i,), memory_space=pltpu.SMEM),
            pl.BlockSpec((tm,), lambda i: (i0 + jnp.minimum(i + 1, n - 1),), memory_space=pltpu.SMEM),
            pl.BlockSpec((tm,), lambda i: (i0 + jnp.minimum(i + 1, n - 1),), memory_space=pltpu.SMEM),
            pl.BlockSpec((tm, D_MODEL), lambda i: (i0 + i, 0)),
            pl.BlockSpec((tm, TOP_K), lambda i: (i0 + i, 0)),
            pl.BlockSpec((None, None, None, 1, D_MODEL), lambda i: (l, _cond_row(i0 + i, tm), 5, 0, 0)),
            vec, vec,
            pl.BlockSpec(memory_space=pl.ANY),
        ],
        out_specs=pl.BlockSpec((tm, D_MODEL), lambda i: (i, 0)),
        out_shape=jax.ShapeDtypeStruct((t_count, D_MODEL), F32),
        scratch_shapes=[pltpu.VMEM((2, TOP_K, tm, D_MODEL // 2), jnp.uint32), pltpu.SemaphoreType.DMA((2,))],
        compiler_params=_cparams(("arbitrary",)),
        name="tail",
    )(pos0, pos1, pos0, pos1, x1, gate_t, mods, ln_g.reshape(DEPTH, 1, D_MODEL), ln_b.reshape(DEPTH, 1, D_MODEL), ys)


def _swap_partners(w):
    nf = ROPE_DIM // 4
    return jnp.concatenate([w[..., nf:2 * nf], w[..., :nf], w[..., 3 * nf:], w[..., 2 * nf:3 * nf]], axis=-1)


def _prep_w_in(w_in):
    kr = w_in[..., C_KR:C_KR + ROPE_DIM]
    w_kr = jnp.concatenate([kr, _swap_partners(kr)], axis=-1).astype(BF16)
    return w_in.astype(BF16), w_kr, w_in[..., C_KR + ROPE_DIM:].astype(BF16)


def _prep_w_uq(w_uq):
    w = w_uq.reshape(DEPTH, Q_LORA, MLA_HEADS, NOPE_DIM + ROPE_DIM)
    return jnp.concatenate([w, _swap_partners(w[..., NOPE_DIM:])], axis=-1).reshape(
        DEPTH, Q_LORA, MLA_HEADS * MLA_QK).astype(BF16)


def _prep_w_ukv(w_ukv):
    w = w_ukv.reshape(DEPTH, KV_LORA, MLA_HEADS, NOPE_DIM + V_DIM)
    return jnp.concatenate([w[..., :NOPE_DIM].reshape(DEPTH, KV_LORA, MLA_W),
                            w[..., NOPE_DIM:].reshape(DEPTH, KV_LORA, MLA_W)], axis=-1).astype(BF16)


def _rope_table(tm):
    half = ROPE_DIM // 2
    nf = half // 2
    t = jnp.arange(DEC_SEQ)
    inv = ROPE_BASE ** (-jnp.arange(nf, dtype=F32) * 2.0 / half)
    ar = (t // GRID_W).astype(F32)[:, None] * inv[None, :]
    ac = (t % GRID_W).astype(F32)[:, None] * inv[None, :]
    cos = jnp.concatenate([jnp.cos(ar), jnp.cos(ar), jnp.cos(ac), jnp.cos(ac)], axis=-1)
    sin = jnp.concatenate([-jnp.sin(ar), jnp.sin(ar), -jnp.sin(ac), jnp.sin(ac)], axis=-1)
    ident = jnp.concatenate([jnp.ones((tm, ROPE_DIM), F32), jnp.zeros((tm, ROPE_DIM), F32)], axis=-1)
    return jnp.concatenate([ident, jnp.concatenate([cos, sin], axis=-1)], axis=0)


def _na_bias_table(na_rpb):
    cols = np.arange(GRID_W)
    col_start = np.clip(cols - NA_COLS // 2, 0, GRID_W - NA_COLS)
    col_ok = (cols[None, :] >= col_start[:, None]) & (cols[None, :] < col_start[:, None] + NA_COLS)
    d_col = np.clip(cols[None, :] - cols[:, None] + NA_COLS - 1, 0, 2 * NA_COLS - 2)
    pick_col = (d_col[:, :, None] == np.arange(2 * NA_COLS - 1)[None, None, :]).astype(np.float32)
    tab = jnp.einsum("lhrc,qwc->lhrqw", na_rpb, pick_col, precision=lax.Precision.HIGHEST)
    tab = jnp.where(col_ok[None, None, None], tab, NEG_BIG)
    nxt = jnp.concatenate([tab[:, :, 1:], jnp.full_like(tab[:, :, :1], NEG_BIG)], axis=2)
    return jnp.concatenate([tab, nxt], axis=-1)


def kernel(x_prompt, x_sample, cache_na_k, cache_na_v, cache_mla_ckv, cache_mla_krope, c, c_ctx, w_ada, b_ada,
           w_in, g_q, g_kv, w_uq, w_ukv, na_rpb, sgu_ln_g, sgu_ln_b, sgu_w, sgu_b, w_out, ln1_g, ln1_b, ln2_g,
           ln2_b, w_router, b_router, w_gate, w_up, w_down):
    max_tiles = (T_ALL * TOP_K) // TM_EXP + N_EXPERTS

    cond = jnp.zeros((N_COND, D_MODEL), F32).at[0].set(c_ctx).at[1:1 + DEC_BATCH].set(c)
    mods = _ada(cond, w_ada, b_ada).reshape(DEPTH, N_COND, 6, 1, D_MODEL)

    w_main, w_kr, w_z = _prep_w_in(w_in)
    w_uq_p = _prep_w_uq(w_uq)
    w_ukv_p = _prep_w_ukv(w_ukv)
    w_out_b = w_out.astype(BF16)
    w_router_t = w_router.T.astype(BF16)
    b_router_col = b_router.reshape(N_EXPERTS, 1).astype(F32)
    cs_tab = _rope_table(TM_PROJ)
    bias_tab = _na_bias_table(na_rpb)
    cache_k = cache_na_k.reshape(DEC_BATCH, DEPTH, PAST_LEN, NA_W)
    cache_v = cache_na_v.reshape(DEC_BATCH, DEPTH, PAST_LEN, NA_W)
    kmc, vvc = _ctxkv(cache_mla_ckv, cache_mla_krope, w_ukv_p)

    new_k = jnp.zeros((BATCH, DEPTH, SEQ, NA_W), F32)
    new_v = jnp.zeros((BATCH, DEPTH, SEQ, NA_W), F32)
    new_ckv = jnp.zeros((BATCH, DEPTH, SEQ, KV_LORA), F32)
    new_kr = jnp.zeros((BATCH, DEPTH, SEQ, ROPE_DIM), F32)

    x_ctx = x_prompt.reshape(T_CTX, D_MODEL)
    x_lat = x_sample.reshape(T_LAT, D_MODEL)
    for l in range(DEPTH):
        q, k, v, new_k, new_v, qm, km, vv, new_ckv, new_kr, z = _inproj(
            l, x_ctx, x_lat, mods, w_main, w_kr, w_z, g_q, g_kv, w_uq_p, w_ukv_p, cs_tab, new_k, new_v, new_ckv, new_kr)

        oa_c, ob_c = _ctx_attn(q, k, v, qm, km, vv)
        oa_l = _lat_na(l, q, k, v, cache_k, cache_v, bias_tab)
        ob_l = _lat_mla(l, qm, km, vv, kmc, vvc)
        oc = _sgu(l, z, sgu_ln_g, sgu_ln_b, sgu_w, sgu_b)

        x1, hp, e, rank, gate, cnt = _outproj(l, oa_c, oa_l, ob_c, ob_l, oc, w_out_b, x_ctx, x_lat, mods, ln1_g, ln1_b,
                                              w_router_t, b_router_col)
        pos, tile_expert, next_expert, n_tiles, pad_lo, pad_hi = _dispatch_plan(
            e, rank, cnt[:, 0].astype(jnp.int32), max_tiles)
        pos0, pos1 = pos[0], pos[1]
        xs = _dispatch(pad_lo, pad_hi, n_tiles, pos0, pos1, hp, max_tiles)
        ys = _experts(l, tile_expert, next_expert, n_tiles, xs, w_gate, w_up, w_down, max_tiles)
        gate_t = gate.T
        x_ctx = _tail(l, 0, T_CTX, pos0, pos1, x1, gate_t, mods, ln2_g, ln2_b, ys)
        x_lat = _tail(l, T_CTX, T_LAT, pos0, pos1, x1, gate_t, mods, ln2_g, ln2_b, ys)

    return (x_ctx.reshape(BATCH, SEQ, D_MODEL), x_lat.reshape(DEC_BATCH, DEC_SEQ, D_MODEL),
            new_k.reshape(BATCH, DEPTH, SEQ, NA_HEADS, NA_HEAD_DIM), new_v.reshape(BATCH, DEPTH, SEQ, NA_HEADS, NA_HEAD_DIM),
            new_ckv, new_kr)
```

```python
import functools

import numpy as np
import jax
import jax.numpy as jnp
from jax import lax
from jax.experimental import pallas as pl
from jax.experimental.pallas import tpu as pltpu

F32 = jnp.float32
BF16 = jnp.bfloat16

D_MODEL = 2048
BATCH = 32
SEQ = 256
DEPTH = 2
DEC_BATCH = 2
DEC_SEQ = 2048
PAST_LEN = 256
GRID_W = 64
GRID_R = DEC_SEQ // GRID_W
NA_HEADS = 6
NA_HEAD_DIM = 128
NA_ROWS = 8
NA_COLS = 16
MLA_HEADS = 6
Q_LORA = 512
KV_LORA = 512
NOPE_DIM = 128
ROPE_DIM = 64
V_DIM = 128
ROPE_BASE = 10000.0
MLA_SCALE = (NOPE_DIM + ROPE_DIM) ** -0.5
NA_SCALE = NA_HEAD_DIM ** -0.5
SGU_GROUPS = 4
SGU_DIM = 128
CHUNK = 128
NA_W = NA_HEADS * NA_HEAD_DIM
MLA_W = MLA_HEADS * V_DIM
SGU_W = SGU_GROUPS * SGU_DIM
D_MIX = NA_W + MLA_W + SGU_W
N_EXPERTS = 16
N_GROUPS = 4
EXPERTS_PER_GROUP = N_EXPERTS // N_GROUPS
TOP_K = 2
D_EXPERT = 512
ALPHA = (2 * DEPTH) ** 0.25
LN_EPS = 1e-6
RMS_EPS = 1e-6

T_CTX = BATCH * SEQ
T_LAT = DEC_BATCH * DEC_SEQ
T_ALL = T_CTX + T_LAT
N_COND = 8

C_Q, C_K, C_V = 0, NA_W, 2 * NA_W
C_CQ = 3 * NA_W
C_CKV = C_CQ + Q_LORA
C_KR = C_CKV + KV_LORA
MLA_QK = 256
LANES = 128
ROW_TILE = (D_MODEL // 2 // LANES, LANES)

TM_PROJ = 256
TM_OUT = 512
TM_SGU = 256
TM_TAIL = 256
TM_DISP = 512
TM_EXP = 256
TQ_MLA = 256
MLA_KCHUNK = 512
OUT_SPLIT = 1
NA_QROWS = 4
NA_KROWS = NA_QROWS + NA_ROWS
NEG_BIG = -1e30
VMEM_LIMIT = 56 * 1024 * 1024


def _cparams(sem, vmem=VMEM_LIMIT):
    return pltpu.CompilerParams(dimension_semantics=sem, vmem_limit_bytes=vmem)


def _cond_row(i, tm):
    t0 = i * tm
    return jnp.where(t0 < T_CTX, 0, 1 + (t0 - T_CTX) // DEC_SEQ)


def _ln(x):
    mu = jnp.mean(x, axis=-1, keepdims=True)
    xc = x - mu
    var = jnp.mean(xc * xc, axis=-1, keepdims=True)
    return xc * lax.rsqrt(var + LN_EPS)


def _rms(x):
    return x * lax.rsqrt(jnp.mean(x * x, axis=-1, keepdims=True) + RMS_EPS)


def _dot(a, b):
    return jnp.dot(a, b, preferred_element_type=F32)


def _pack_rows(x):
    xf = x.astype(BF16).astype(F32)
    half = x.shape[-1] // 2
    w = pltpu.bitcast(xf[:, :half], jnp.uint32) | (pltpu.bitcast(xf[:, half:], jnp.uint32) >> 16)
    return w.reshape(x.shape[0], *ROW_TILE)


def _unpack_rows(w3):
    w = w3.reshape(w3.shape[0], D_MODEL // 2)
    return jnp.concatenate([pltpu.bitcast(w & jnp.uint32(0xFFFF0000), F32), pltpu.bitcast(w << 16, F32)], axis=-1)


def _dot_nt(a, b):
    return lax.dot_general(a, b, (((1,), (1,)), ((), ())), preferred_element_type=F32)


def _ada_kernel(c_ref, w_ref, b_ref, o_ref):
    c = c_ref[...]
    s = c * jax.nn.sigmoid(c)
    o_ref[...] = _dot(s.astype(BF16), w_ref[...].astype(BF16)) + b_ref[...]


def _ada(cond, w_ada, b_ada):
    tn = 1024
    n = 6 * D_MODEL
    return pl.pallas_call(
        _ada_kernel,
        grid=(DEPTH, n // tn),
        in_specs=[
            pl.BlockSpec((N_COND, D_MODEL), lambda l, j: (0, 0)),
            pl.BlockSpec((None, D_MODEL, tn), lambda l, j: (l, 0, j)),
            pl.BlockSpec((None, 1, tn), lambda l, j: (l, 0, j)),
        ],
        out_specs=pl.BlockSpec((None, N_COND, tn), lambda l, j: (l, 0, j)),
        out_shape=jax.ShapeDtypeStruct((DEPTH, N_COND, n), F32),
        compiler_params=_cparams(("parallel", "parallel")),
        name="ada_mod",
    )(cond, w_ada, b_ada.reshape(DEPTH, 1, n))


def _inproj_kernel(xc_ref, xl_ref, sh_ref, sc_ref, w_ref, wkr_ref, wz_ref, gq_ref, gkv_ref, wuq_ref, wukv_ref, cs_ref,
                   k32_in, v32_in, ckv32_in, kr32_in,
                   q_ref, k_ref, v_ref, k32_ref, v32_ref, qm_ref, km_ref, vv_ref,
                   ckv32_ref, kr32_ref, z_ref):
    del k32_in, v32_in, ckv32_in, kr32_in
    i = pl.program_id(0)
    is_ctx = i * TM_PROJ < T_CTX
    x = jnp.where(is_ctx, xc_ref[...], xl_ref[...])
    h = (_ln(x) * (1.0 + sc_ref[...]) + sh_ref[...]).astype(BF16)

    def proj(a, b):
        return _dot(h, w_ref[:, a:b])

    q_ref[...] = proj(C_Q, C_K).astype(BF16)
    k = proj(C_K, C_V)
    k_ref[...] = k.astype(BF16)
    v = proj(C_V, C_CQ)
    v_ref[...] = v.astype(BF16)

    cs = cs_ref[...]
    lane = lax.broadcasted_iota(jnp.int32, cs.shape, 1)

    def rotate(pair):
        t = pair * cs
        return jnp.where(lane < ROPE_DIM, t + pltpu.roll(t, ROPE_DIM, 1), 0.0)

    cqn = (_rms(proj(C_CQ, C_CKV)) * gq_ref[...]).astype(BF16)
    mq = _dot(cqn, wuq_ref[...])
    for hd in range(MLA_HEADS):
        c0 = hd * MLA_QK
        qm_ref[:, c0:c0 + NOPE_DIM] = mq[:, c0:c0 + NOPE_DIM].astype(BF16)
        qm_ref[:, c0 + NOPE_DIM:c0 + MLA_QK] = rotate(mq[:, c0 + NOPE_DIM:c0 + MLA_QK]).astype(BF16)

    ckvn = _rms(proj(C_CKV, C_KR)) * gkv_ref[...]
    kv = _dot(ckvn.astype(BF16), wukv_ref[...])
    kr2 = _dot(h, wkr_ref[...])
    krot = rotate(kr2).astype(BF16)
    for hd in range(MLA_HEADS):
        c0 = hd * MLA_QK
        km_ref[:, c0:c0 + NOPE_DIM] = kv[:, hd * NOPE_DIM:(hd + 1) * NOPE_DIM].astype(BF16)
        km_ref[:, c0 + NOPE_DIM:c0 + MLA_QK] = krot
    vv_ref[...] = kv[:, MLA_HEADS * NOPE_DIM:].astype(BF16)

    z_ref[...] = _dot(h, wz_ref[...])

    @pl.when(is_ctx)
    def _():
        k32_ref[...] = k
        v32_ref[...] = v
        ckv32_ref[...] = ckvn
        kr32_ref[...] = kr2[:, :ROPE_DIM]


def _inproj(l, x_ctx, x_lat, mods, w_main, w_kr, w_z, g_q, g_kv, w_uq_p, w_ukv_p, cs_tab, new_k, new_v, new_ckv, new_kr):
    tm = TM_PROJ
    assert tm == SEQ
    nct = T_CTX // tm
    per_b = DEC_SEQ // tm

    def row(i):
        return (i, 0)

    def ctx_row(i):
        return (jnp.minimum(i, nct - 1), 0)

    def lat_row(i):
        return (jnp.maximum(i - nct, 0), 0)

    def cs_row(i):
        return (jnp.where(i < nct, 0, 1 + (i - nct) % per_b), 0)

    def mod(kind):
        return pl.BlockSpec((None, None, None, 1, D_MODEL), lambda i: (l, _cond_row(i, tm), kind, 0, 0))

    def const2(shape):
        return pl.BlockSpec(shape, lambda i: (l, 0, 0), pipeline_mode=pl.Buffered(1))

    def cache(w):
        return pl.BlockSpec((None, None, SEQ, w), lambda i: (jnp.minimum(i, nct - 1), l, 0, 0))

    anyspec = pl.BlockSpec(memory_space=pl.ANY)
    out_shapes = [
        jax.ShapeDtypeStruct((T_ALL, NA_W), BF16),
        jax.ShapeDtypeStruct((T_ALL, NA_W), BF16),
        jax.ShapeDtypeStruct((T_ALL, NA_W), BF16),
        jax.ShapeDtypeStruct(new_k.shape, F32),
        jax.ShapeDtypeStruct(new_v.shape, F32),
        jax.ShapeDtypeStruct((T_ALL, MLA_HEADS * MLA_QK), BF16),
        jax.ShapeDtypeStruct((T_ALL, MLA_HEADS * MLA_QK), BF16),
        jax.ShapeDtypeStruct((T_ALL, MLA_W), BF16),
        jax.ShapeDtypeStruct(new_ckv.shape, F32),
        jax.ShapeDtypeStruct(new_kr.shape, F32),
        jax.ShapeDtypeStruct((T_ALL, 2 * SGU_W), F32),
    ]
    out_specs = [
        pl.BlockSpec((tm, NA_W), row), pl.BlockSpec((tm, NA_W), row), pl.BlockSpec((tm, NA_W), row),
        cache(NA_W), cache(NA_W),
        pl.BlockSpec((tm, MLA_HEADS * MLA_QK), row), pl.BlockSpec((tm, MLA_HEADS * MLA_QK), row),
        pl.BlockSpec((tm, MLA_W), row),
        cache(KV_LORA), cache(ROPE_DIM),
        pl.BlockSpec((tm, 2 * SGU_W), row),
    ]
    return pl.pallas_call(
        _inproj_kernel,
        grid=(T_ALL // tm,),
        in_specs=[
            pl.BlockSpec((tm, D_MODEL), ctx_row), pl.BlockSpec((tm, D_MODEL), lat_row),
            mod(0), mod(1),
            const2((None, D_MODEL, C_KR)), const2((None, D_MODEL, 2 * ROPE_DIM)), const2((None, D_MODEL, 2 * SGU_W)),
            pl.BlockSpec((None, 1, Q_LORA), lambda i: (l, 0, 0)),
            pl.BlockSpec((None, 1, KV_LORA), lambda i: (l, 0, 0)),
            const2((None, Q_LORA, MLA_HEADS * MLA_QK)),
            const2((None, KV_LORA, 2 * MLA_W)),
            pl.BlockSpec((tm, 2 * ROPE_DIM), cs_row),
            anyspec, anyspec, anyspec, anyspec,
        ],
        out_specs=out_specs,
        out_shape=out_shapes,
        input_output_aliases={12: 3, 13: 4, 14: 8, 15: 9},
        compiler_params=_cparams(("arbitrary",)),
        name="in_proj",
    )(x_ctx, x_lat, mods, mods, w_main, w_kr, w_z, g_q.reshape(DEPTH, 1, Q_LORA), g_kv.reshape(DEPTH, 1, KV_LORA),
      w_uq_p, w_ukv_p, cs_tab, new_k, new_v, new_ckv, new_kr)


def _ctxkv_kernel(ckv_ref, kr_ref, w_ref, km_ref, vv_ref):
    kv = _dot(ckv_ref[...].astype(BF16), w_ref[...])
    kr = jnp.concatenate([kr_ref[...], jnp.zeros((PAST_LEN, ROPE_DIM), F32)], axis=-1).astype(BF16)
    for hd in range(MLA_HEADS):
        c0 = hd * MLA_QK
        km_ref[:, c0:c0 + NOPE_DIM] = kv[:, hd * NOPE_DIM:(hd + 1) * NOPE_DIM].astype(BF16)
        km_ref[:, c0 + NOPE_DIM:c0 + MLA_QK] = kr
    vv_ref[...] = kv[:, MLA_HEADS * NOPE_DIM:].astype(BF16)


def _ctxkv(cache_ckv, cache_krope, w_ukv_p):
    return pl.pallas_call(
        _ctxkv_kernel,
        grid=(DEPTH, DEC_BATCH),
        in_specs=[
            pl.BlockSpec((None, None, PAST_LEN, KV_LORA), lambda l, b: (b, l, 0, 0)),
            pl.BlockSpec((None, None, PAST_LEN, ROPE_DIM), lambda l, b: (b, l, 0, 0)),
            pl.BlockSpec((None, KV_LORA, 2 * MLA_W), lambda l, b: (l, 0, 0)),
        ],
        out_specs=[
            pl.BlockSpec((None, None, PAST_LEN, MLA_HEADS * MLA_QK), lambda l, b: (l, b, 0, 0)),
            pl.BlockSpec((None, None, PAST_LEN, MLA_W), lambda l, b: (l, b, 0, 0)),
        ],
        out_shape=[
            jax.ShapeDtypeStruct((DEPTH, DEC_BATCH, PAST_LEN, MLA_HEADS * MLA_QK), BF16),
            jax.ShapeDtypeStruct((DEPTH, DEC_BATCH, PAST_LEN, MLA_W), BF16),
        ],
        compiler_params=_cparams(("parallel", "parallel")),
        name="ctx_cache_kv",
    )(cache_ckv, cache_krope, w_ukv_p)


def _softmax_pv(blocks):
    ss = [s for s, _ in blocks]
    m = ss[0].max(axis=-1, keepdims=True)
    for s in ss[1:]:
        m = jnp.maximum(m, s.max(axis=-1, keepdims=True))
    num = None
    den = None
    for s, (_, v) in zip(ss, blocks):
        e = jnp.exp(s - m)
        d = e.sum(axis=-1, keepdims=True)
        o = _dot(e.astype(BF16), v)
        num = o if num is None else num + o
        den = d if den is None else den + d
    return num / den


def _ctx_attn_kernel(q_ref, k_ref, v_ref, qm_ref, km_ref, vv_ref, oa_ref, ob_ref):
    for hd in range(NA_HEADS):
        sl = slice(hd * NA_HEAD_DIM, (hd + 1) * NA_HEAD_DIM)
        s = _dot_nt(q_ref[:, sl], k_ref[:, sl]) * NA_SCALE
        oa_ref[:, sl] = _softmax_pv([(s, v_ref[:, sl])]).astype(BF16)
    for hd in range(MLA_HEADS):
        sq = slice(hd * MLA_QK, (hd + 1) * MLA_QK)
        sv = slice(hd * V_DIM, (hd + 1) * V_DIM)
        s = _dot_nt(qm_ref[:, sq], km_ref[:, sq]) * MLA_SCALE
        ob_ref[:, sv] = _softmax_pv([(s, vv_ref[:, sv])]).astype(BF16)


def _ctx_attn(q, k, v, qm, km, vv):
    def spec(w):
        return pl.BlockSpec((SEQ, w), lambda b: (b, 0))

    return pl.pallas_call(
        _ctx_attn_kernel,
        grid=(BATCH,),
        in_specs=[spec(NA_W), spec(NA_W), spec(NA_W),
                  spec(MLA_HEADS * MLA_QK), spec(MLA_HEADS * MLA_QK), spec(MLA_W)],
        out_specs=[spec(NA_W), spec(MLA_W)],
        out_shape=[jax.ShapeDtypeStruct((T_CTX, NA_W), BF16), jax.ShapeDtypeStruct((T_CTX, MLA_W), BF16)],
        compiler_params=_cparams(("parallel",)),
        name="ctx_attn",
    )(q, k, v, qm, km, vv)


def _lat_mla_kernel(q_ref, k_ref, v_ref, kc_ref, vc_ref, o_ref):
    q = q_ref[...]
    blocks = []
    for c in range(DEC_SEQ // MLA_KCHUNK):
        ks = slice(c * MLA_KCHUNK, (c + 1) * MLA_KCHUNK)
        blocks.append((_dot_nt(q, k_ref[ks, :]) * MLA_SCALE, v_ref[ks, :]))
    blocks.append((_dot_nt(q, kc_ref[...]) * MLA_SCALE, vc_ref[...]))
    o_ref[...] = _softmax_pv(blocks).astype(BF16)


def _lat_mla(l, qm, km, vv, kmc, vvc):
    tq = TQ_MLA
    nq = DEC_SEQ // tq
    row0 = T_CTX // tq
    kb0 = T_CTX // DEC_SEQ
    return pl.pallas_call(
        _lat_mla_kernel,
        grid=(DEC_BATCH, MLA_HEADS, nq),
        in_specs=[
            pl.BlockSpec((tq, MLA_QK), lambda b, h, t: (row0 + b * nq + t, h)),
            pl.BlockSpec((DEC_SEQ, MLA_QK), lambda b, h, t: (kb0 + b, h)),
            pl.BlockSpec((DEC_SEQ, V_DIM), lambda b, h, t: (kb0 + b, h)),
            pl.BlockSpec((None, None, PAST_LEN, MLA_QK), lambda b, h, t: (l, b, 0, h)),
            pl.BlockSpec((None, None, PAST_LEN, V_DIM), lambda b, h, t: (l, b, 0, h)),
        ],
        out_specs=pl.BlockSpec((tq, V_DIM), lambda b, h, t: (b * nq + t, h)),
        out_shape=jax.ShapeDtypeStruct((T_LAT, MLA_W), BF16),
        compiler_params=_cparams(("parallel", "parallel", "arbitrary")),
        name="lat_mla",
    )(qm, km, vv, kmc, vvc)


def _na_block_plan():
    plan = []
    for r0 in range(0, GRID_R, NA_QROWS):
        ws = min(max(r0 - NA_ROWS // 2, 0), GRID_R - NA_KROWS)
        rows = []
        for r in range(r0, r0 + NA_QROWS):
            rs = min(max(r - NA_ROWS // 2, 0), GRID_R - NA_ROWS)
            rel = [kr - r + NA_ROWS - 1 if rs <= kr < rs + NA_ROWS else None for kr in range(ws, ws + NA_KROWS)]
            rows.append([(rel[j], rel[j + 1]) for j in range(0, NA_KROWS, 2)])
        plan.append((ws, rows))
    return plan


def _lat_na_kernel(q_ref, k_ref, v_ref, ck_ref, cv_ref, tab_ref, o_ref):
    ck = ck_ref[...].astype(BF16)
    cv = cv_ref[...].astype(BF16)
    nq = NA_QROWS * GRID_W
    nk = NA_KROWS * GRID_W
    lane = lax.broadcasted_iota(jnp.int32, (GRID_W, 2 * GRID_W), 1)
    masked = jnp.full((GRID_W, 2 * GRID_W), NEG_BIG, F32)

    def pair_bias(d0, d1):
        if d0 is None and d1 is None:
            return masked
        if d1 is None:
            return jnp.where(lane < GRID_W, tab_ref[d0], NEG_BIG)
        if d0 is None:
            return jnp.where(lane >= GRID_W, tab_ref[d1 - 1], NEG_BIG)
        return tab_ref[d0]

    for blk, (ws, rows) in enumerate(_na_block_plan()):
        q = q_ref[blk * nq:(blk + 1) * nq, :]
        kw = k_ref[ws * GRID_W:ws * GRID_W + nk, :]
        vw = v_ref[ws * GRID_W:ws * GRID_W + nk, :]
        bias = jnp.concatenate(
            [jnp.concatenate([pair_bias(d0, d1) for d0, d1 in pairs], axis=1) for pairs in rows], axis=0)
        s1 = _dot_nt(q, kw) * NA_SCALE + bias
        s2 = _dot_nt(q, ck) * NA_SCALE
        o_ref[blk * nq:(blk + 1) * nq, :] = _softmax_pv([(s1, vw), (s2, cv)]).astype(BF16)


def _lat_na(l, q, k, v, cache_k, cache_v, bias_tab):
    kb0 = T_CTX // DEC_SEQ

    def lat(b, h):
        return (kb0 + b, h)

    return pl.pallas_call(
        _lat_na_kernel,
        grid=(DEC_BATCH, NA_HEADS),
        in_specs=[
            pl.BlockSpec((DEC_SEQ, NA_HEAD_DIM), lat),
            pl.BlockSpec((DEC_SEQ, NA_HEAD_DIM), lat),
            pl.BlockSpec((DEC_SEQ, NA_HEAD_DIM), lat),
            pl.BlockSpec((None, None, PAST_LEN, NA_HEAD_DIM), lambda b, h: (b, l, 0, h)),
            pl.BlockSpec((None, None, PAST_LEN, NA_HEAD_DIM), lambda b, h: (b, l, 0, h)),
            pl.BlockSpec((None, None, 2 * NA_ROWS - 1, GRID_W, 2 * GRID_W), lambda b, h: (l, h, 0, 0, 0)),
        ],
        out_specs=pl.BlockSpec((DEC_SEQ, NA_HEAD_DIM), lambda b, h: (b, h)),
        out_shape=jax.ShapeDtypeStruct((T_LAT, NA_W), BF16),
        compiler_params=_cparams(("parallel", "parallel")),
        name="lat_na",
    )(q, k, v, cache_k, cache_v, bias_tab)


def _sgu_kernel(z_ref, g_ref, b_ref, w_ref, bs_ref, o_ref):
    z = z_ref[...]
    a = 0.5 * z * (1.0 + lax.erf(z * (0.5 ** 0.5)))
    for c in range(TM_SGU // CHUNK):
        rows = slice(c * CHUNK, (c + 1) * CHUNK)
        for g in range(SGU_GROUPS):
            u = a[rows, g * SGU_DIM:(g + 1) * SGU_DIM]
            v = a[rows, SGU_W + g * SGU_DIM:SGU_W + (g + 1) * SGU_DIM]
            vn = _ln(v) * g_ref[g:g + 1, :] + b_ref[g:g + 1, :]
            t = _dot(w_ref[g].astype(BF16), vn.astype(BF16)) + bs_ref[g]
            o_ref[rows, g * SGU_DIM:(g + 1) * SGU_DIM] = (u * t).astype(BF16)


def _sgu(l, z, ln_g, ln_b, w_s, b_s):
    tm = TM_SGU
    return pl.pallas_call(
        _sgu_kernel,
        grid=(T_ALL // tm,),
        in_specs=[
            pl.BlockSpec((tm, 2 * SGU_W), lambda i: (i, 0)),
            pl.BlockSpec((None, SGU_GROUPS, SGU_DIM), lambda i: (l, 0, 0)),
            pl.BlockSpec((None, SGU_GROUPS, SGU_DIM), lambda i: (l, 0, 0)),
            pl.BlockSpec((None, SGU_GROUPS, CHUNK, CHUNK), lambda i: (l, 0, 0, 0)),
            pl.BlockSpec((None, SGU_GROUPS, CHUNK, 1), lambda i: (l, 0, 0, 0)),
        ],
        out_specs=pl.BlockSpec((tm, SGU_W), lambda i: (i, 0)),
        out_shape=jax.ShapeDtypeStruct((T_ALL, SGU_W), BF16),
        compiler_params=_cparams(("parallel",)),
        name="sgu",
    )(z, ln_g, ln_b, w_s, b_s.reshape(DEPTH, SGU_GROUPS, CHUNK, 1))


def _route_rows(logits_t, bias_col):
    s = jax.nn.sigmoid(logits_t)
    sb = s + bias_col
    rows = [sb[r:r + 1, :] for r in range(N_EXPERTS)]
    srows = [s[r:r + 1, :] for r in range(N_EXPERTS)]
    best, gsel = None, None
    for g in range(N_GROUPS):
        m = rows[g * EXPERTS_PER_GROUP:(g + 1) * EXPERTS_PER_GROUP]
        score = None
        for a in range(EXPERTS_PER_GROUP):
            for b in range(a + 1, EXPERTS_PER_GROUP):
                pair = m[a] + m[b]
                score = pair if score is None else jnp.maximum(score, pair)
        if best is None:
            best, gsel = score, jnp.zeros(score.shape, jnp.int32)
        else:
            upd = score > best
            best = jnp.where(upd, score, best)
            gsel = jnp.where(upd, g, gsel)

    def pick(table, j):
        out = table[j]
        for g in range(1, N_GROUPS):
            out = jnp.where(gsel == g, table[g * EXPERTS_PER_GROUP + j], out)
        return out

    v = [pick(rows, j) for j in range(EXPERTS_PER_GROUP)]
    sv = [pick(srows, j) for j in range(EXPERTS_PER_GROUP)]
    m1, i1, w1 = v[0], jnp.zeros(gsel.shape, jnp.int32), sv[0]
    for j in range(1, EXPERTS_PER_GROUP):
        upd = v[j] > m1
        m1 = jnp.where(upd, v[j], m1)
        i1 = jnp.where(upd, j, i1)
        w1 = jnp.where(upd, sv[j], w1)
    m2 = jnp.full(gsel.shape, -jnp.inf, F32)
    i2 = jnp.zeros(gsel.shape, jnp.int32)
    w2 = jnp.zeros(gsel.shape, F32)
    for j in range(EXPERTS_PER_GROUP):
        upd = jnp.logical_and(i1 != j, v[j] > m2)
        m2 = jnp.where(upd, v[j], m2)
        i2 = jnp.where(upd, j, i2)
        w2 = jnp.where(upd, sv[j], w2)
    den = w1 + w2
    return gsel * EXPERTS_PER_GROUP + i1, gsel * EXPERTS_PER_GROUP + i2, w1 / den, w2 / den


def _outproj_kernel(ac_ref, al_ref, bc_ref, bl_ref, c_ref, wa_ref, wb_ref, wc_ref, xc_ref, xl_ref, g1_ref, sh2_ref, sc2_ref,
                    lg_ref, lb_ref, wrt_ref, br_ref, x1_ref, hp_ref, e_ref, rank_ref, gate_ref, cnt_ref, base_ref):
    is_ctx = pl.program_id(0) * TM_OUT < T_CTX

    @pl.when(pl.program_id(0) == 0)
    def _():
        base_ref[...] = jnp.zeros_like(base_ref)

    ts = TM_OUT // OUT_SPLIT
    sub = lax.broadcasted_iota(jnp.int32, (N_EXPERTS, ts), 0)
    before = (lax.broadcasted_iota(jnp.int32, (ts, ts), 0) < lax.broadcasted_iota(jnp.int32, (ts, ts), 1)).astype(BF16)
    base = base_ref[:, 0:1]
    for part in range(OUT_SPLIT):
        rows = slice(part * ts, (part + 1) * ts)
        a = jnp.where(is_ctx, ac_ref[rows, :], al_ref[rows, :])
        b = jnp.where(is_ctx, bc_ref[rows, :], bl_ref[rows, :])
        x = jnp.where(is_ctx, xc_ref[rows, :], xl_ref[rows, :])
        mixed = _dot(a, wa_ref[...]) + _dot(b, wb_ref[...]) + _dot(c_ref[rows, :], wc_ref[...])
        x1 = _ln(ALPHA * x + g1_ref[...] * mixed) * lg_ref[...] + lb_ref[...]
        x1_ref[rows, :] = x1
        h2 = (_ln(x1) * (1.0 + sc2_ref[...]) + sh2_ref[...]).astype(BF16)

        hp_ref[rows] = _pack_rows(h2)

        e1, e2, w1, w2 = _route_rows(_dot_nt(wrt_ref[...], h2), br_ref[...])
        e_ref[0:1, rows] = e1
        e_ref[1:2, rows] = e2
        gate_ref[0:1, rows] = w1
        gate_ref[1:2, rows] = w2

        o1 = (sub == e1).astype(F32)
        o2 = (sub == e2).astype(F32)
        p1 = _dot(o1.astype(BF16), before)
        p2 = _dot(o2.astype(BF16), before)
        c1 = o1.sum(axis=1, keepdims=True)
        c2 = o2.sum(axis=1, keepdims=True)
        rank_ref[0:1, rows] = jnp.sum(o1 * (base + p1), axis=0, keepdims=True).astype(jnp.int32)
        rank_ref[1:2, rows] = jnp.sum(o2 * (base + c1 + p2), axis=0, keepdims=True).astype(jnp.int32)
        base = base + (c1 + c2)
    base_ref[...] = jnp.broadcast_to(base, base_ref.shape)
    cnt_ref[...] = base_ref[...]


def _outproj(l, oa_c, oa_l, ob_c, ob_l, oc, w_out_b, x_ctx, x_lat, mods, ln_g, ln_b, w_router_t, b_router_col):
    tm = TM_OUT
    nct = T_CTX // tm

    def row(i):
        return (i, 0)

    def ctx_row(i):
        return (jnp.minimum(i, nct - 1), 0)

    def lat_row(i):
        return (jnp.maximum(i - nct, 0), 0)

    def tok(i):
        return (0, i)

    def mod(kind):
        return pl.BlockSpec((None, None, None, 1, D_MODEL), lambda i: (l, _cond_row(i, tm), kind, 0, 0))

    def wblk(rows, blk):
        return pl.BlockSpec((None, rows, D_MODEL), lambda i: (l, blk, 0), pipeline_mode=pl.Buffered(1))

    vec = pl.BlockSpec((None, 1, D_MODEL), lambda i: (l, 0, 0))
    return pl.pallas_call(
        _outproj_kernel,
        grid=(T_ALL // tm,),
        in_specs=[
            pl.BlockSpec((tm, NA_W), ctx_row), pl.BlockSpec((tm, NA_W), lat_row),
            pl.BlockSpec((tm, MLA_W), ctx_row), pl.BlockSpec((tm, MLA_W), lat_row),
            pl.BlockSpec((tm, SGU_W), row),
            wblk(NA_W, 0), wblk(MLA_W, 1), wblk(SGU_W, (NA_W + MLA_W) // SGU_W),
            pl.BlockSpec((tm, D_MODEL), ctx_row), pl.BlockSpec((tm, D_MODEL), lat_row),
            mod(2), mod(3), mod(4), vec, vec,
            pl.BlockSpec((N_EXPERTS, D_MODEL), lambda i: (0, 0)),
            pl.BlockSpec((N_EXPERTS, 1), lambda i: (0, 0)),
        ],
        out_specs=[pl.BlockSpec((tm, D_MODEL), row), pl.BlockSpec((tm, *ROW_TILE), lambda i: (i, 0, 0)),
                   pl.BlockSpec((TOP_K, tm), tok), pl.BlockSpec((TOP_K, tm), tok), pl.BlockSpec((TOP_K, tm), tok),
                   pl.BlockSpec((N_EXPERTS, 128), lambda i: (0, 0))],
        out_shape=[jax.ShapeDtypeStruct((T_ALL, D_MODEL), F32),
                   jax.ShapeDtypeStruct((T_ALL, *ROW_TILE), jnp.uint32),
                   jax.ShapeDtypeStruct((TOP_K, T_ALL), jnp.int32),
                   jax.ShapeDtypeStruct((TOP_K, T_ALL), jnp.int32),
                   jax.ShapeDtypeStruct((TOP_K, T_ALL), F32),
                   jax.ShapeDtypeStruct((N_EXPERTS, 128), F32)],
        scratch_shapes=[pltpu.VMEM((N_EXPERTS, 128), F32)],
        compiler_params=_cparams(("arbitrary",)),
        name="out_proj",
    )(oa_c, oa_l, ob_c, ob_l, oc, w_out_b, w_out_b, w_out_b, x_ctx, x_lat, mods, mods, mods,
      ln_g.reshape(DEPTH, 1, D_MODEL), ln_b.reshape(DEPTH, 1, D_MODEL), w_router_t, b_router_col)


def _dispatch_plan(e, rank, counts, max_tiles):
    tiles = (counts + TM_EXP - 1) // TM_EXP
    tile_end = jnp.cumsum(tiles)
    off = (tile_end - tiles) * TM_EXP
    n_tiles = tile_end[-1]
    ids = jnp.arange(N_EXPERTS, dtype=jnp.int32)
    pos = rank + jnp.sum(jnp.where(e[..., None] == ids, off, 0), axis=-1)
    ti = jnp.arange(max_tiles, dtype=jnp.int32)
    owner = jnp.sum((tile_end[None, :] <= ti[:, None]).astype(jnp.int32), axis=1)
    last = jnp.sum((tile_end <= n_tiles - 1).astype(jnp.int32))
    tile_expert = jnp.where(ti < n_tiles, owner, last).astype(jnp.int32)
    later_used = jnp.logical_and(ids[None, :] > ids[:, None], tiles[None, :] > 0)
    next_expert = jnp.min(jnp.where(later_used, ids[None, :], N_EXPERTS), axis=1)
    next_expert = jnp.where(next_expert < N_EXPERTS, next_expert, -1).astype(jnp.int32)
    pad_lo = (off + counts).astype(jnp.int32)
    pad_hi = (off + tiles * TM_EXP).astype(jnp.int32)
    return pos.astype(jnp.int32), tile_expert, next_expert, n_tiles.reshape(1).astype(jnp.int32), pad_lo, pad_hi


def _row_copy(src_ref, src_row, dst_ref, dst_row, sem):
    return pltpu.make_async_copy(src_ref.at[pl.ds(src_row, 1)], dst_ref.at[pl.ds(dst_row, 1)], sem)


def _dispatch_kernel(lo_ref, hi_ref, nt_ref, p0_ref, p1_ref, hp_ref, xs_ref, zeros, sem, *, max_tiles):
    @pl.when(pl.program_id(0) == 0)
    def _():
        zeros[...] = jnp.zeros_like(zeros)

        def fill(r, carry):
            _row_copy(zeros, 0, xs_ref, r, sem).start()
            return carry

        def fill_done(r, carry):
            _row_copy(zeros, 0, xs_ref, 0, sem).wait()
            return carry

        for ex in range(N_EXPERTS):
            lax.fori_loop(lo_ref[ex], hi_ref[ex], fill, 0)
        for ex in range(N_EXPERTS):
            lax.fori_loop(lo_ref[ex], hi_ref[ex], fill_done, 0)

        def tile_copy(t):
            return pltpu.make_async_copy(zeros, xs_ref.at[pl.ds(t * TM_EXP, TM_EXP)], sem)

        def fill_tile(t, carry):
            tile_copy(t).start()
            return carry

        def fill_tile_done(t, carry):
            tile_copy(0).wait()
            return carry

        lax.fori_loop(nt_ref[0], max_tiles, fill_tile, 0)
        lax.fori_loop(nt_ref[0], max_tiles, fill_tile_done, 0)

    def issue(t, carry):
        for k, p_ref in enumerate((p0_ref, p1_ref)):
            _row_copy(hp_ref, t, xs_ref, p_ref[t], sem).start(priority=k)
        return carry

    lax.fori_loop(0, TM_DISP, issue, 0, unroll=8)

    def drain(t, carry):
        for k in range(TOP_K):
            _row_copy(hp_ref, 0, xs_ref, 0, sem).wait()
        return carry

    lax.fori_loop(0, TM_DISP, drain, 0, unroll=8)


def _dispatch(pad_lo, pad_hi, n_tiles, pos0, pos1, hp, max_tiles):
    grid_spec = pltpu.PrefetchScalarGridSpec(
        num_scalar_prefetch=3,
        grid=(T_ALL // TM_DISP,),
        in_specs=[
            pl.BlockSpec((TM_DISP,), lambda i, lo, hi, nt: (i,), memory_space=pltpu.SMEM),
            pl.BlockSpec((TM_DISP,), lambda i, lo, hi, nt: (i,), memory_space=pltpu.SMEM),
            pl.BlockSpec((TM_DISP, *ROW_TILE), lambda i, lo, hi, nt: (i, 0, 0)),
        ],
        out_specs=pl.BlockSpec(memory_space=pl.ANY),
        scratch_shapes=[pltpu.VMEM((TM_EXP, *ROW_TILE), jnp.uint32), pltpu.SemaphoreType.DMA(())],
    )
    return pl.pallas_call(
        functools.partial(_dispatch_kernel, max_tiles=max_tiles),
        grid_spec=grid_spec,
        out_shape=jax.ShapeDtypeStruct((max_tiles * TM_EXP, *ROW_TILE), jnp.uint32),
        compiler_params=_cparams(("arbitrary",)),
        name="dispatch",
    )(pad_lo, pad_hi, n_tiles, pos0, pos1, hp)


def _expert_kernel(te_ref, nxt_ref, nt_ref, x_ref, wg_hbm, wu_hbm, wd_hbm, y_ref,
                   wg_f, wu_f, wd_f, wg_s, wu_s, wd_s, switches, sem, *, layer):
    i = pl.program_id(0)
    e = te_ref[i]

    def weight_copies(expert, slot):
        return [pltpu.make_async_copy(w_hbm.at[layer, expert], buf.at[slot], sem.at[slot, j])
                for j, (w_hbm, buf) in enumerate(((wg_hbm, wg_f), (wu_hbm, wu_f), (wd_hbm, wd_f)))]

    @pl.when(i == 0)
    def _():
        for c in weight_copies(e, 0):
            c.start()

    fresh = jnp.logical_or(i == 0, e != te_ref[jnp.maximum(i - 1, 0)])

    @pl.when(fresh)
    def _():
        n_sw = jnp.where(i == 0, 0, switches[0] + 1)
        switches[0] = n_sw
        slot = n_sw % 2
        for c in weight_copies(e, slot):
            c.wait()
        wg_s[...] = wg_f[slot].astype(BF16)
        wu_s[...] = wu_f[slot].astype(BF16)
        wd_s[...] = wd_f[slot].astype(BF16)
        nx = nxt_ref[e]

        @pl.when(nx >= 0)
        def _():
            for c in weight_copies(nx, 1 - slot):
                c.start()

    @pl.when(i < nt_ref[0])
    def _():
        x = _unpack_rows(x_ref[...]).astype(BF16)
        g = _dot(x, wg_s[...])
        u = _dot(x, wu_s[...])
        a = (g * jax.nn.sigmoid(g) * u).astype(BF16)
        y_ref[...] = _pack_rows(_dot(a, wd_s[...]))

    @pl.when(i >= nt_ref[0])
    def _():
        y_ref[...] = jnp.zeros_like(y_ref)


def _experts(l, tile_expert, next_expert, n_tiles, xs, w_gate, w_up, w_down, max_tiles):
    tm = TM_EXP
    anyspec = pl.BlockSpec(memory_space=pl.ANY)
    grid_spec = pltpu.PrefetchScalarGridSpec(
        num_scalar_prefetch=3,
        grid=(max_tiles,),
        in_specs=[pl.BlockSpec((tm, *ROW_TILE), lambda i, te, nx, nt: (i, 0, 0)), anyspec, anyspec, anyspec],
        out_specs=pl.BlockSpec((tm, *ROW_TILE), lambda i, te, nx, nt: (i, 0, 0)),
        scratch_shapes=[pltpu.VMEM((2, D_MODEL, D_EXPERT), F32), pltpu.VMEM((2, D_MODEL, D_EXPERT), F32),
                        pltpu.VMEM((2, D_EXPERT, D_MODEL), F32),
                        pltpu.VMEM((D_MODEL, D_EXPERT), BF16), pltpu.VMEM((D_MODEL, D_EXPERT), BF16),
                        pltpu.VMEM((D_EXPERT, D_MODEL), BF16),
                        pltpu.SMEM((1,), jnp.int32), pltpu.SemaphoreType.DMA((2, 3))],
    )
    return pl.pallas_call(
        functools.partial(_expert_kernel, layer=l),
        grid_spec=grid_spec,
        out_shape=jax.ShapeDtypeStruct((max_tiles * tm, *ROW_TILE), jnp.uint32),
        compiler_params=_cparams(("arbitrary",)),
        name="experts",
    )(tile_expert, next_expert, n_tiles, xs, w_gate, w_up, w_down)


def _tail_kernel(p0_ref, p1_ref, p0n_ref, p1n_ref, x1_ref, gate_ref, g2_ref, lg_ref, lb_ref, ys_ref, o_ref, ybuf, sem,
                 *, n_tiles):
    tm = TM_TAIL
    i = pl.program_id(0)
    slot = i % 2

    def issue(p_refs, s):
        def body(t, carry):
            for k, p_ref in enumerate(p_refs):
                _row_copy(ys_ref, p_ref[t], ybuf.at[s, k], t, sem.at[s]).start(priority=k)
            return carry

        lax.fori_loop(0, tm, body, 0, unroll=8)

    @pl.when(i == 0)
    def _():
        issue((p0_ref, p1_ref), 0)

    @pl.when(i + 1 < n_tiles)
    def _():
        issue((p0n_ref, p1n_ref), 1 - slot)

    def drain(t, carry):
        for k in range(TOP_K):
            _row_copy(ys_ref, 0, ybuf.at[slot, k], 0, sem.at[slot]).wait()
        return carry

    lax.fori_loop(0, tm, drain, 0, unroll=8)

    gate = gate_ref[...]
    y = gate[:, 0:1] * _unpack_rows(ybuf[slot, 0]) + gate[:, 1:2] * _unpack_rows(ybuf[slot, 1])
    o_ref[...] = _ln(ALPHA * x1_ref[...] + g2_ref[...] * y) * lg_ref[...] + lb_ref[...]


def _tail(l, t_start, t_count, pos0, pos1, x1, gate_t, mods, ln_g, ln_b, ys):
    tm = TM_TAIL
    n = t_count // tm
    i0 = t_start // tm
    vec = pl.BlockSpec((None, 1, D_MODEL), lambda i: (l, 0, 0))
    return pl.pallas_call(
        functools.partial(_tail_kernel, n_tiles=n),
        grid=(n,),
        in_specs=[
            pl.BlockSpec((tm,), lambda i: (i0 + i,), memory_space=pltpu.SMEM),
            pl.BlockSpec((tm,), lambda i: (i0 + i,), memory_space=pltpu.SMEM),
            pl.BlockSpec((tm,), lambda i: (i0 + jnp.minimum(i + 1, n - 1),), memory_space=pltpu.SMEM),
            pl.BlockSpec((tm,), lambda i: (i0 + jnp.minimum(i + 1, n - 1),), memory_space=pltpu.SMEM),
            pl.BlockSpec((tm, D_MODEL), lambda i: (i0 + i, 0)),
            pl.BlockSpec((tm, TOP_K), lambda i: (i0 + i, 0)),
            pl.BlockSpec((None, None, None, 1, D_MODEL), lambda i: (l, _cond_row(i0 + i, tm), 5, 0, 0)),
            vec, vec,
            pl.BlockSpec(memory_space=pl.ANY),
        ],
        out_specs=pl.BlockSpec((tm, D_MODEL), lambda i: (i, 0)),
        out_shape=jax.ShapeDtypeStruct((t_count, D_MODEL), F32),
        scratch_shapes=[pltpu.VMEM((2, TOP_K, tm, *ROW_TILE), jnp.uint32), pltpu.SemaphoreType.DMA((2,))],
        compiler_params=_cparams(("arbitrary",)),
        name="tail",
    )(pos0, pos1, pos0, pos1, x1, gate_t, mods, ln_g.reshape(DEPTH, 1, D_MODEL), ln_b.reshape(DEPTH, 1, D_MODEL), ys)


def _swap_partners(w):
    nf = ROPE_DIM // 4
    return jnp.concatenate([w[..., nf:2 * nf], w[..., :nf], w[..., 3 * nf:], w[..., 2 * nf:3 * nf]], axis=-1)


def _prep_w_in(w_in):
    kr = w_in[..., C_KR:C_KR + ROPE_DIM]
    w_kr = jnp.concatenate([kr, _swap_partners(kr)], axis=-1).astype(BF16)
    return w_in.astype(BF16), w_kr, w_in[..., C_KR + ROPE_DIM:].astype(BF16)


def _prep_w_uq(w_uq):
    w = w_uq.reshape(DEPTH, Q_LORA, MLA_HEADS, NOPE_DIM + ROPE_DIM)
    return jnp.concatenate([w, _swap_partners(w[..., NOPE_DIM:])], axis=-1).reshape(
        DEPTH, Q_LORA, MLA_HEADS * MLA_QK).astype(BF16)


def _prep_w_ukv(w_ukv):
    w = w_ukv.reshape(DEPTH, KV_LORA, MLA_HEADS, NOPE_DIM + V_DIM)
    return jnp.concatenate([w[..., :NOPE_DIM].reshape(DEPTH, KV_LORA, MLA_W),
                            w[..., NOPE_DIM:].reshape(DEPTH, KV_LORA, MLA_W)], axis=-1).astype(BF16)


def _rope_table(tm):
    half = ROPE_DIM // 2
    nf = half // 2
    t = jnp.arange(DEC_SEQ)
    inv = ROPE_BASE ** (-jnp.arange(nf, dtype=F32) * 2.0 / half)
    ar = (t // GRID_W).astype(F32)[:, None] * inv[None, :]
    ac = (t % GRID_W).astype(F32)[:, None] * inv[None, :]
    cos = jnp.concatenate([jnp.cos(ar), jnp.cos(ar), jnp.cos(ac), jnp.cos(ac)], axis=-1)
    sin = jnp.concatenate([-jnp.sin(ar), jnp.sin(ar), -jnp.sin(ac), jnp.sin(ac)], axis=-1)
    ident = jnp.concatenate([jnp.ones((tm, ROPE_DIM), F32), jnp.zeros((tm, ROPE_DIM), F32)], axis=-1)
    return jnp.concatenate([ident, jnp.concatenate([cos, sin], axis=-1)], axis=0)


def _na_bias_table(na_rpb):
    cols = np.arange(GRID_W)
    col_start = np.clip(cols - NA_COLS // 2, 0, GRID_W - NA_COLS)
    col_ok = (cols[None, :] >= col_start[:, None]) & (cols[None, :] < col_start[:, None] + NA_COLS)
    d_col = np.clip(cols[None, :] - cols[:, None] + NA_COLS - 1, 0, 2 * NA_COLS - 2)
    pick_col = (d_col[:, :, None] == np.arange(2 * NA_COLS - 1)[None, None, :]).astype(np.float32)
    tab = jnp.einsum("lhrc,qwc->lhrqw", na_rpb, pick_col, precision=lax.Precision.HIGHEST)
    tab = jnp.where(col_ok[None, None, None], tab, NEG_BIG)
    nxt = jnp.concatenate([tab[:, :, 1:], jnp.full_like(tab[:, :, :1], NEG_BIG)], axis=2)
    return jnp.concatenate([tab, nxt], axis=-1)


def kernel(x_prompt, x_sample, cache_na_k, cache_na_v, cache_mla_ckv, cache_mla_krope, c, c_ctx, w_ada, b_ada,
           w_in, g_q, g_kv, w_uq, w_ukv, na_rpb, sgu_ln_g, sgu_ln_b, sgu_w, sgu_b, w_out, ln1_g, ln1_b, ln2_g,
           ln2_b, w_router, b_router, w_gate, w_up, w_down):
    max_tiles = (T_ALL * TOP_K) // TM_EXP + N_EXPERTS

    cond = jnp.zeros((N_COND, D_MODEL), F32).at[0].set(c_ctx).at[1:1 + DEC_BATCH].set(c)
    mods = _ada(cond, w_ada, b_ada).reshape(DEPTH, N_COND, 6, 1, D_MODEL)

    w_main, w_kr, w_z = _prep_w_in(w_in)
    w_uq_p = _prep_w_uq(w_uq)
    w_ukv_p = _prep_w_ukv(w_ukv)
    w_out_b = w_out.astype(BF16)
    w_router_t = w_router.T.astype(BF16)
    b_router_col = b_router.reshape(N_EXPERTS, 1).astype(F32)
    cs_tab = _rope_table(TM_PROJ)
    bias_tab = _na_bias_table(na_rpb)
    cache_k = cache_na_k.reshape(DEC_BATCH, DEPTH, PAST_LEN, NA_W)
    cache_v = cache_na_v.reshape(DEC_BATCH, DEPTH, PAST_LEN, NA_W)
    kmc, vvc = _ctxkv(cache_mla_ckv, cache_mla_krope, w_ukv_p)

    new_k = jnp.zeros((BATCH, DEPTH, SEQ, NA_W), F32)
    new_v = jnp.zeros((BATCH, DEPTH, SEQ, NA_W), F32)
    new_ckv = jnp.zeros((BATCH, DEPTH, SEQ, KV_LORA), F32)
    new_kr = jnp.zeros((BATCH, DEPTH, SEQ, ROPE_DIM), F32)

    x_ctx = x_prompt.reshape(T_CTX, D_MODEL)
    x_lat = x_sample.reshape(T_LAT, D_MODEL)
    for l in range(DEPTH):
        q, k, v, new_k, new_v, qm, km, vv, new_ckv, new_kr, z = _inproj(
            l, x_ctx, x_lat, mods, w_main, w_kr, w_z, g_q, g_kv, w_uq_p, w_ukv_p, cs_tab, new_k, new_v, new_ckv, new_kr)

        oa_c, ob_c = _ctx_attn(q, k, v, qm, km, vv)
        oa_l = _lat_na(l, q, k, v, cache_k, cache_v, bias_tab)
        ob_l = _lat_mla(l, qm, km, vv, kmc, vvc)
        oc = _sgu(l, z, sgu_ln_g, sgu_ln_b, sgu_w, sgu_b)

        x1, hp, e, rank, gate, cnt = _outproj(l, oa_c, oa_l, ob_c, ob_l, oc, w_out_b, x_ctx, x_lat, mods, ln1_g, ln1_b,
                                              w_router_t, b_router_col)
        pos, tile_expert, next_expert, n_tiles, pad_lo, pad_hi = _dispatch_plan(
            e, rank, cnt[:, 0].astype(jnp.int32), max_tiles)
        pos0, pos1 = pos[0], pos[1]
        xs = _dispatch(pad_lo, pad_hi, n_tiles, pos0, pos1, hp, max_tiles)
        ys = _experts(l, tile_expert, next_expert, n_tiles, xs, w_gate, w_up, w_down, max_tiles)
        gate_t = gate.T
        x_ctx = _tail(l, 0, T_CTX, pos0, pos1, x1, gate_t, mods, ln2_g, ln2_b, ys)
        x_lat = _tail(l, T_CTX, T_LAT, pos0, pos1, x1, gate_t, mods, ln2_g, ln2_b, ys)

    return (x_ctx.reshape(BATCH, SEQ, D_MODEL), x_lat.reshape(DEC_BATCH, DEC_SEQ, D_MODEL),
            new_k.reshape(BATCH, DEPTH, SEQ, NA_HEADS, NA_HEAD_DIM), new_v.reshape(BATCH, DEPTH, SEQ, NA_HEADS, NA_HEAD_DIM),
            new_ckv, new_kr)
```

```python
import functools

import numpy as np
import jax
import jax.numpy as jnp
from jax import lax
from jax.experimental import pallas as pl
from jax.experimental.pallas import tpu as pltpu

F32 = jnp.float32
BF16 = jnp.bfloat16

D_MODEL = 2048
BATCH = 32
SEQ = 256
DEPTH = 2
DEC_BATCH = 2
DEC_SEQ = 2048
PAST_LEN = 256
GRID_W = 64
GRID_R = DEC_SEQ // GRID_W
NA_HEADS = 6
NA_HEAD_DIM = 128
NA_ROWS = 8
NA_COLS = 16
MLA_HEADS = 6
Q_LORA = 512
KV_LORA = 512
NOPE_DIM = 128
ROPE_DIM = 64
V_DIM = 128
ROPE_BASE = 10000.0
MLA_SCALE = (NOPE_DIM + ROPE_DIM) ** -0.5
NA_SCALE = NA_HEAD_DIM ** -0.5
SGU_GROUPS = 4
SGU_DIM = 128
CHUNK = 128
NA_W = NA_HEADS * NA_HEAD_DIM
MLA_W = MLA_HEADS * V_DIM
SGU_W = SGU_GROUPS * SGU_DIM
D_MIX = NA_W + MLA_W + SGU_W
N_EXPERTS = 16
N_GROUPS = 4
EXPERTS_PER_GROUP = N_EXPERTS // N_GROUPS
TOP_K = 2
D_EXPERT = 512
ALPHA = (2 * DEPTH) ** 0.25
LN_EPS = 1e-6
RMS_EPS = 1e-6

T_CTX = BATCH * SEQ
T_LAT = DEC_BATCH * DEC_SEQ
T_ALL = T_CTX + T_LAT
N_COND = 8

C_Q, C_K, C_V = 0, NA_W, 2 * NA_W
C_CQ = 3 * NA_W
C_CKV = C_CQ + Q_LORA
C_KR = C_CKV + KV_LORA
MLA_QK = 256
LANES = 128
ROW_TILE = (D_MODEL // 2 // LANES, LANES)

TM_PROJ = 256
TM_OUT = 512
TM_SGU = 256
TM_TAIL = 512
TM_DISP = 1024
TM_EXP = 256
TQ_MLA = 256
MLA_KCHUNK = 512
NA_QROWS = 4
NA_KROWS = NA_QROWS + NA_ROWS
NEG_BIG = -1e30
VMEM_LIMIT = 56 * 1024 * 1024


def _cparams(sem, vmem=VMEM_LIMIT):
    return pltpu.CompilerParams(dimension_semantics=sem, vmem_limit_bytes=vmem)


def _cond_row(i, tm):
    t0 = i * tm
    return jnp.where(t0 < T_CTX, 0, 1 + (t0 - T_CTX) // DEC_SEQ)


def _ln(x):
    mu = jnp.mean(x, axis=-1, keepdims=True)
    xc = x - mu
    var = jnp.mean(xc * xc, axis=-1, keepdims=True)
    return xc * lax.rsqrt(var + LN_EPS)


def _rms(x):
    return x * lax.rsqrt(jnp.mean(x * x, axis=-1, keepdims=True) + RMS_EPS)


def _dot(a, b):
    return jnp.dot(a, b, preferred_element_type=F32)


def _pack_rows(x):
    xf = x.astype(BF16).astype(F32)
    half = x.shape[-1] // 2
    w = pltpu.bitcast(xf[:, :half], jnp.uint32) | (pltpu.bitcast(xf[:, half:], jnp.uint32) >> 16)
    return w.reshape(x.shape[0], *ROW_TILE)


def _unpack_rows(w3):
    w = w3.reshape(w3.shape[0], D_MODEL // 2)
    return jnp.concatenate([pltpu.bitcast(w & jnp.uint32(0xFFFF0000), F32), pltpu.bitcast(w << 16, F32)], axis=-1)


def _dot_nt(a, b):
    return lax.dot_general(a, b, (((1,), (1,)), ((), ())), preferred_element_type=F32)


def _ada_kernel(c_ref, w_ref, b_ref, o_ref):
    c = c_ref[...]
    s = c * jax.nn.sigmoid(c)
    o_ref[...] = _dot(s.astype(BF16), w_ref[...].astype(BF16)) + b_ref[...]


def _ada(cond, w_ada, b_ada):
    tn = 1024
    n = 6 * D_MODEL
    return pl.pallas_call(
        _ada_kernel,
        grid=(DEPTH, n // tn),
        in_specs=[
            pl.BlockSpec((N_COND, D_MODEL), lambda l, j: (0, 0)),
            pl.BlockSpec((None, D_MODEL, tn), lambda l, j: (l, 0, j)),
            pl.BlockSpec((None, 1, tn), lambda l, j: (l, 0, j)),
        ],
        out_specs=pl.BlockSpec((None, N_COND, tn), lambda l, j: (l, 0, j)),
        out_shape=jax.ShapeDtypeStruct((DEPTH, N_COND, n), F32),
        compiler_params=_cparams(("parallel", "parallel")),
        name="ada_mod",
    )(cond, w_ada, b_ada.reshape(DEPTH, 1, n))


def _inproj_kernel(xc_ref, xl_ref, sh_ref, sc_ref, w_ref, wkr_ref, wz_ref, gq_ref, gkv_ref, wuq_ref, wukv_ref, cs_ref,
                   k32_in, v32_in, ckv32_in, kr32_in,
                   q_ref, k_ref, v_ref, k32_ref, v32_ref, qm_ref, km_ref, vv_ref,
                   ckv32_ref, kr32_ref, z_ref):
    del k32_in, v32_in, ckv32_in, kr32_in
    i = pl.program_id(0)
    is_ctx = i * TM_PROJ < T_CTX
    x = jnp.where(is_ctx, xc_ref[...], xl_ref[...])
    h = (_ln(x) * (1.0 + sc_ref[...]) + sh_ref[...]).astype(BF16)

    def proj(a, b):
        return _dot(h, w_ref[:, a:b])

    q_ref[...] = proj(C_Q, C_K).astype(BF16)
    k = proj(C_K, C_V)
    k_ref[...] = k.astype(BF16)
    v = proj(C_V, C_CQ)
    v_ref[...] = v.astype(BF16)

    cs = cs_ref[...]
    lane = lax.broadcasted_iota(jnp.int32, cs.shape, 1)

    def rotate(pair):
        t = pair * cs
        return jnp.where(lane < ROPE_DIM, t + pltpu.roll(t, ROPE_DIM, 1), 0.0)

    cqn = (_rms(proj(C_CQ, C_CKV)) * gq_ref[...]).astype(BF16)
    mq = _dot(cqn, wuq_ref[...])
    for hd in range(MLA_HEADS):
        c0 = hd * MLA_QK
        qm_ref[:, c0:c0 + NOPE_DIM] = mq[:, c0:c0 + NOPE_DIM].astype(BF16)
        qm_ref[:, c0 + NOPE_DIM:c0 + MLA_QK] = rotate(mq[:, c0 + NOPE_DIM:c0 + MLA_QK]).astype(BF16)

    ckvn = _rms(proj(C_CKV, C_KR)) * gkv_ref[...]
    kv = _dot(ckvn.astype(BF16), wukv_ref[...])
    kr2 = _dot(h, wkr_ref[...])
    krot = rotate(kr2).astype(BF16)
    for hd in range(MLA_HEADS):
        c0 = hd * MLA_QK
        km_ref[:, c0:c0 + NOPE_DIM] = kv[:, hd * NOPE_DIM:(hd + 1) * NOPE_DIM].astype(BF16)
        km_ref[:, c0 + NOPE_DIM:c0 + MLA_QK] = krot
    vv_ref[...] = kv[:, MLA_HEADS * NOPE_DIM:].astype(BF16)

    z_ref[...] = _dot(h, wz_ref[...])

    @pl.when(is_ctx)
    def _():
        k32_ref[...] = k
        v32_ref[...] = v
        ckv32_ref[...] = ckvn
        kr32_ref[...] = kr2[:, :ROPE_DIM]


def _inproj(l, x_ctx, x_lat, mods, w_main, w_kr, w_z, g_q, g_kv, w_uq_p, w_ukv_p, cs_tab, new_k, new_v, new_ckv, new_kr):
    tm = TM_PROJ
    assert tm == SEQ
    nct = T_CTX // tm
    per_b = DEC_SEQ // tm

    def row(i):
        return (i, 0)

    def ctx_row(i):
        return (jnp.minimum(i, nct - 1), 0)

    def lat_row(i):
        return (jnp.maximum(i - nct, 0), 0)

    def cs_row(i):
        return (jnp.where(i < nct, 0, 1 + (i - nct) % per_b), 0)

    def mod(kind):
        return pl.BlockSpec((None, None, None, 1, D_MODEL), lambda i: (l, _cond_row(i, tm), kind, 0, 0))

    def const2(shape):
        return pl.BlockSpec(shape, lambda i: (l, 0, 0), pipeline_mode=pl.Buffered(1))

    def cache(w):
        return pl.BlockSpec((None, None, SEQ, w), lambda i: (jnp.minimum(i, nct - 1), l, 0, 0))

    anyspec = pl.BlockSpec(memory_space=pl.ANY)
    out_shapes = [
        jax.ShapeDtypeStruct((T_ALL, NA_W), BF16),
        jax.ShapeDtypeStruct((T_ALL, NA_W), BF16),
        jax.ShapeDtypeStruct((T_ALL, NA_W), BF16),
        jax.ShapeDtypeStruct(new_k.shape, F32),
        jax.ShapeDtypeStruct(new_v.shape, F32),
        jax.ShapeDtypeStruct((T_ALL, MLA_HEADS * MLA_QK), BF16),
        jax.ShapeDtypeStruct((T_ALL, MLA_HEADS * MLA_QK), BF16),
        jax.ShapeDtypeStruct((T_ALL, MLA_W), BF16),
        jax.ShapeDtypeStruct(new_ckv.shape, F32),
        jax.ShapeDtypeStruct(new_kr.shape, F32),
        jax.ShapeDtypeStruct((T_ALL, 2 * SGU_W), F32),
    ]
    out_specs = [
        pl.BlockSpec((tm, NA_W), row), pl.BlockSpec((tm, NA_W), row), pl.BlockSpec((tm, NA_W), row),
        cache(NA_W), cache(NA_W),
        pl.BlockSpec((tm, MLA_HEADS * MLA_QK), row), pl.BlockSpec((tm, MLA_HEADS * MLA_QK), row),
        pl.BlockSpec((tm, MLA_W), row),
        cache(KV_LORA), cache(ROPE_DIM),
        pl.BlockSpec((tm, 2 * SGU_W), row),
    ]
    return pl.pallas_call(
        _inproj_kernel,
        grid=(T_ALL // tm,),
        in_specs=[
            pl.BlockSpec((tm, D_MODEL), ctx_row), pl.BlockSpec((tm, D_MODEL), lat_row),
            mod(0), mod(1),
            const2((None, D_MODEL, C_KR)), const2((None, D_MODEL, 2 * ROPE_DIM)), const2((None, D_MODEL, 2 * SGU_W)),
            pl.BlockSpec((None, 1, Q_LORA), lambda i: (l, 0, 0)),
            pl.BlockSpec((None, 1, KV_LORA), lambda i: (l, 0, 0)),
            const2((None, Q_LORA, MLA_HEADS * MLA_QK)),
            const2((None, KV_LORA, 2 * MLA_W)),
            pl.BlockSpec((tm, 2 * ROPE_DIM), cs_row),
            anyspec, anyspec, anyspec, anyspec,
        ],
        out_specs=out_specs,
        out_shape=out_shapes,
        input_output_aliases={12: 3, 13: 4, 14: 8, 15: 9},
        compiler_params=_cparams(("arbitrary",)),
        name="in_proj",
    )(x_ctx, x_lat, mods, mods, w_main, w_kr, w_z, g_q.reshape(DEPTH, 1, Q_LORA), g_kv.reshape(DEPTH, 1, KV_LORA),
      w_uq_p, w_ukv_p, cs_tab, new_k, new_v, new_ckv, new_kr)


def _ctxkv_kernel(ckv_ref, kr_ref, w_ref, km_ref, vv_ref):
    kv = _dot(ckv_ref[...].astype(BF16), w_ref[...])
    kr = jnp.concatenate([kr_ref[...], jnp.zeros((PAST_LEN, ROPE_DIM), F32)], axis=-1).astype(BF16)
    for hd in range(MLA_HEADS):
        c0 = hd * MLA_QK
        km_ref[:, c0:c0 + NOPE_DIM] = kv[:, hd * NOPE_DIM:(hd + 1) * NOPE_DIM].astype(BF16)
        km_ref[:, c0 + NOPE_DIM:c0 + MLA_QK] = kr
    vv_ref[...] = kv[:, MLA_HEADS * NOPE_DIM:].astype(BF16)


def _ctxkv(cache_ckv, cache_krope, w_ukv_p):
    return pl.pallas_call(
        _ctxkv_kernel,
        grid=(DEPTH, DEC_BATCH),
        in_specs=[
            pl.BlockSpec((None, None, PAST_LEN, KV_LORA), lambda l, b: (b, l, 0, 0)),
            pl.BlockSpec((None, None, PAST_LEN, ROPE_DIM), lambda l, b: (b, l, 0, 0)),
            pl.BlockSpec((None, KV_LORA, 2 * MLA_W), lambda l, b: (l, 0, 0)),
        ],
        out_specs=[
            pl.BlockSpec((None, None, PAST_LEN, MLA_HEADS * MLA_QK), lambda l, b: (l, b, 0, 0)),
            pl.BlockSpec((None, None, PAST_LEN, MLA_W), lambda l, b: (l, b, 0, 0)),
        ],
        out_shape=[
            jax.ShapeDtypeStruct((DEPTH, DEC_BATCH, PAST_LEN, MLA_HEADS * MLA_QK), BF16),
            jax.ShapeDtypeStruct((DEPTH, DEC_BATCH, PAST_LEN, MLA_W), BF16),
        ],
        compiler_params=_cparams(("parallel", "parallel")),
        name="ctx_cache_kv",
    )(cache_ckv, cache_krope, w_ukv_p)


def _softmax_pv(blocks):
    ss = [s for s, _ in blocks]
    m = ss[0].max(axis=-1, keepdims=True)
    for s in ss[1:]:
        m = jnp.maximum(m, s.max(axis=-1, keepdims=True))
    num = None
    den = None
    for s, (_, v) in zip(ss, blocks):
        e = jnp.exp(s - m)
        d = e.sum(axis=-1, keepdims=True)
        o = _dot(e.astype(BF16), v)
        num = o if num is None else num + o
        den = d if den is None else den + d
    return num / den


def _ctx_attn_kernel(q_ref, k_ref, v_ref, qm_ref, km_ref, vv_ref, oa_ref, ob_ref):
    for hd in range(NA_HEADS):
        sl = slice(hd * NA_HEAD_DIM, (hd + 1) * NA_HEAD_DIM)
        s = _dot_nt(q_ref[:, sl], k_ref[:, sl]) * NA_SCALE
        oa_ref[:, sl] = _softmax_pv([(s, v_ref[:, sl])]).astype(BF16)
    for hd in range(MLA_HEADS):
        sq = slice(hd * MLA_QK, (hd + 1) * MLA_QK)
        sv = slice(hd * V_DIM, (hd + 1) * V_DIM)
        s = _dot_nt(qm_ref[:, sq], km_ref[:, sq]) * MLA_SCALE
        ob_ref[:, sv] = _softmax_pv([(s, vv_ref[:, sv])]).astype(BF16)


def _ctx_attn(q, k, v, qm, km, vv):
    def spec(w):
        return pl.BlockSpec((SEQ, w), lambda b: (b, 0))

    return pl.pallas_call(
        _ctx_attn_kernel,
        grid=(BATCH,),
        in_specs=[spec(NA_W), spec(NA_W), spec(NA_W),
                  spec(MLA_HEADS * MLA_QK), spec(MLA_HEADS * MLA_QK), spec(MLA_W)],
        out_specs=[spec(NA_W), spec(MLA_W)],
        out_shape=[jax.ShapeDtypeStruct((T_CTX, NA_W), BF16), jax.ShapeDtypeStruct((T_CTX, MLA_W), BF16)],
        compiler_params=_cparams(("parallel",)),
        name="ctx_attn",
    )(q, k, v, qm, km, vv)


def _lat_mla_kernel(q_ref, k_ref, v_ref, kc_ref, vc_ref, o_ref):
    q = q_ref[...]
    blocks = []
    for c in range(DEC_SEQ // MLA_KCHUNK):
        ks = slice(c * MLA_KCHUNK, (c + 1) * MLA_KCHUNK)
        blocks.append((_dot_nt(q, k_ref[ks, :]) * MLA_SCALE, v_ref[ks, :]))
    blocks.append((_dot_nt(q, kc_ref[...]) * MLA_SCALE, vc_ref[...]))
    o_ref[...] = _softmax_pv(blocks).astype(BF16)


def _lat_mla(l, qm, km, vv, kmc, vvc):
    tq = TQ_MLA
    nq = DEC_SEQ // tq
    row0 = T_CTX // tq
    kb0 = T_CTX // DEC_SEQ
    return pl.pallas_call(
        _lat_mla_kernel,
        grid=(DEC_BATCH, MLA_HEADS, nq),
        in_specs=[
            pl.BlockSpec((tq, MLA_QK), lambda b, h, t: (row0 + b * nq + t, h)),
            pl.BlockSpec((DEC_SEQ, MLA_QK), lambda b, h, t: (kb0 + b, h)),
            pl.BlockSpec((DEC_SEQ, V_DIM), lambda b, h, t: (kb0 + b, h)),
            pl.BlockSpec((None, None, PAST_LEN, MLA_QK), lambda b, h, t: (l, b, 0, h)),
            pl.BlockSpec((None, None, PAST_LEN, V_DIM), lambda b, h, t: (l, b, 0, h)),
        ],
        out_specs=pl.BlockSpec((tq, V_DIM), lambda b, h, t: (b * nq + t, h)),
        out_shape=jax.ShapeDtypeStruct((T_LAT, MLA_W), BF16),
        compiler_params=_cparams(("parallel", "parallel", "arbitrary")),
        name="lat_mla",
    )(qm, km, vv, kmc, vvc)


def _na_block_plan():
    plan = []
    for r0 in range(0, GRID_R, NA_QROWS):
        ws = min(max(r0 - NA_ROWS // 2, 0), GRID_R - NA_KROWS)
        rows = []
        for r in range(r0, r0 + NA_QROWS):
            rs = min(max(r - NA_ROWS // 2, 0), GRID_R - NA_ROWS)
            rel = [kr - r + NA_ROWS - 1 if rs <= kr < rs + NA_ROWS else None for kr in range(ws, ws + NA_KROWS)]
            rows.append([(rel[j], rel[j + 1]) for j in range(0, NA_KROWS, 2)])
        plan.append((ws, rows))
    return plan


def _lat_na_kernel(q_ref, k_ref, v_ref, ck_ref, cv_ref, tab_ref, o_ref):
    ck = ck_ref[...].astype(BF16)
    cv = cv_ref[...].astype(BF16)
    nq = NA_QROWS * GRID_W
    nk = NA_KROWS * GRID_W
    lane = lax.broadcasted_iota(jnp.int32, (GRID_W, 2 * GRID_W), 1)
    masked = jnp.full((GRID_W, 2 * GRID_W), NEG_BIG, F32)

    def pair_bias(d0, d1):
        if d0 is None and d1 is None:
            return masked
        if d1 is None:
            return jnp.where(lane < GRID_W, tab_ref[d0], NEG_BIG)
        if d0 is None:
            return jnp.where(lane >= GRID_W, tab_ref[d1 - 1], NEG_BIG)
        return tab_ref[d0]

    for blk, (ws, rows) in enumerate(_na_block_plan()):
        q = q_ref[blk * nq:(blk + 1) * nq, :]
        kw = k_ref[ws * GRID_W:ws * GRID_W + nk, :]
        vw = v_ref[ws * GRID_W:ws * GRID_W + nk, :]
        bias = jnp.concatenate(
            [jnp.concatenate([pair_bias(d0, d1) for d0, d1 in pairs], axis=1) for pairs in rows], axis=0)
        s1 = _dot_nt(q, kw) * NA_SCALE + bias
        s2 = _dot_nt(q, ck) * NA_SCALE
        o_ref[blk * nq:(blk + 1) * nq, :] = _softmax_pv([(s1, vw), (s2, cv)]).astype(BF16)


def _lat_na(l, q, k, v, cache_k, cache_v, bias_tab):
    kb0 = T_CTX // DEC_SEQ

    def lat(b, h):
        return (kb0 + b, h)

    return pl.pallas_call(
        _lat_na_kernel,
        grid=(DEC_BATCH, NA_HEADS),
        in_specs=[
            pl.BlockSpec((DEC_SEQ, NA_HEAD_DIM), lat),
            pl.BlockSpec((DEC_SEQ, NA_HEAD_DIM), lat),
            pl.BlockSpec((DEC_SEQ, NA_HEAD_DIM), lat),
            pl.BlockSpec((None, None, PAST_LEN, NA_HEAD_DIM), lambda b, h: (b, l, 0, h)),
            pl.BlockSpec((None, None, PAST_LEN, NA_HEAD_DIM), lambda b, h: (b, l, 0, h)),
            pl.BlockSpec((None, None, 2 * NA_ROWS - 1, GRID_W, 2 * GRID_W), lambda b, h: (l, h, 0, 0, 0)),
        ],
        out_specs=pl.BlockSpec((DEC_SEQ, NA_HEAD_DIM), lambda b, h: (b, h)),
        out_shape=jax.ShapeDtypeStruct((T_LAT, NA_W), BF16),
        compiler_params=_cparams(("parallel", "parallel")),
        name="lat_na",
    )(q, k, v, cache_k, cache_v, bias_tab)


def _sgu_kernel(z_ref, g_ref, b_ref, w_ref, bs_ref, o_ref):
    z = z_ref[...]
    a = 0.5 * z * (1.0 + lax.erf(z * (0.5 ** 0.5)))
    for c in range(TM_SGU // CHUNK):
        rows = slice(c * CHUNK, (c + 1) * CHUNK)
        for g in range(SGU_GROUPS):
            u = a[rows, g * SGU_DIM:(g + 1) * SGU_DIM]
            v = a[rows, SGU_W + g * SGU_DIM:SGU_W + (g + 1) * SGU_DIM]
            vn = _ln(v) * g_ref[g:g + 1, :] + b_ref[g:g + 1, :]
            t = _dot(w_ref[g].astype(BF16), vn.astype(BF16)) + bs_ref[g]
            o_ref[rows, g * SGU_DIM:(g + 1) * SGU_DIM] = (u * t).astype(BF16)


def _sgu(l, z, ln_g, ln_b, w_s, b_s):
    tm = TM_SGU
    return pl.pallas_call(
        _sgu_kernel,
        grid=(T_ALL // tm,),
        in_specs=[
            pl.BlockSpec((tm, 2 * SGU_W), lambda i: (i, 0)),
            pl.BlockSpec((None, SGU_GROUPS, SGU_DIM), lambda i: (l, 0, 0)),
            pl.BlockSpec((None, SGU_GROUPS, SGU_DIM), lambda i: (l, 0, 0)),
            pl.BlockSpec((None, SGU_GROUPS, CHUNK, CHUNK), lambda i: (l, 0, 0, 0)),
            pl.BlockSpec((None, SGU_GROUPS, CHUNK, 1), lambda i: (l, 0, 0, 0)),
        ],
        out_specs=pl.BlockSpec((tm, SGU_W), lambda i: (i, 0)),
        out_shape=jax.ShapeDtypeStruct((T_ALL, SGU_W), BF16),
        compiler_params=_cparams(("parallel",)),
        name="sgu",
    )(z, ln_g, ln_b, w_s, b_s.reshape(DEPTH, SGU_GROUPS, CHUNK, 1))


def _route_rows(logits_t, bias_col):
    s = jax.nn.sigmoid(logits_t)
    sb = s + bias_col
    rows = [sb[r:r + 1, :] for r in range(N_EXPERTS)]
    srows = [s[r:r + 1, :] for r in range(N_EXPERTS)]
    best, gsel = None, None
    for g in range(N_GROUPS):
        m = rows[g * EXPERTS_PER_GROUP:(g + 1) * EXPERTS_PER_GROUP]
        score = None
        for a in range(EXPERTS_PER_GROUP):
            for b in range(a + 1, EXPERTS_PER_GROUP):
                pair = m[a] + m[b]
                score = pair if score is None else jnp.maximum(score, pair)
        if best is None:
            best, gsel = score, jnp.zeros(score.shape, jnp.int32)
        else:
            upd = score > best
            best = jnp.where(upd, score, best)
            gsel = jnp.where(upd, g, gsel)

    def pick(table, j):
        out = table[j]
        for g in range(1, N_GROUPS):
            out = jnp.where(gsel == g, table[g * EXPERTS_PER_GROUP + j], out)
        return out

    v = [pick(rows, j) for j in range(EXPERTS_PER_GROUP)]
    sv = [pick(srows, j) for j in range(EXPERTS_PER_GROUP)]
    m1, i1, w1 = v[0], jnp.zeros(gsel.shape, jnp.int32), sv[0]
    for j in range(1, EXPERTS_PER_GROUP):
        upd = v[j] > m1
        m1 = jnp.where(upd, v[j], m1)
        i1 = jnp.where(upd, j, i1)
        w1 = jnp.where(upd, sv[j], w1)
    m2 = jnp.full(gsel.shape, -jnp.inf, F32)
    i2 = jnp.zeros(gsel.shape, jnp.int32)
    w2 = jnp.zeros(gsel.shape, F32)
    for j in range(EXPERTS_PER_GROUP):
        upd = jnp.logical_and(i1 != j, v[j] > m2)
        m2 = jnp.where(upd, v[j], m2)
        i2 = jnp.where(upd, j, i2)
        w2 = jnp.where(upd, sv[j], w2)
    den = w1 + w2
    return gsel * EXPERTS_PER_GROUP + i1, gsel * EXPERTS_PER_GROUP + i2, w1 / den, w2 / den


def _outproj_kernel(ac_ref, al_ref, bc_ref, bl_ref, c_ref, wa_ref, wb_ref, wc_ref, xc_ref, xl_ref, g1_ref, sh2_ref, sc2_ref,
                    lg_ref, lb_ref, wrt_ref, br_ref, x1_ref, hp_ref, e_ref, rank_ref, gate_ref, cnt_ref, base_ref):
    is_ctx = pl.program_id(0) * TM_OUT < T_CTX

    @pl.when(pl.program_id(0) == 0)
    def _():
        base_ref[...] = jnp.zeros_like(base_ref)

    tm = TM_OUT
    a = jnp.where(is_ctx, ac_ref[...], al_ref[...])
    b = jnp.where(is_ctx, bc_ref[...], bl_ref[...])
    x = jnp.where(is_ctx, xc_ref[...], xl_ref[...])
    mixed = _dot(a, wa_ref[...]) + _dot(b, wb_ref[...]) + _dot(c_ref[...], wc_ref[...])
    x1 = _ln(ALPHA * x + g1_ref[...] * mixed) * lg_ref[...] + lb_ref[...]
    x1_ref[...] = x1
    h2 = (_ln(x1) * (1.0 + sc2_ref[...]) + sh2_ref[...]).astype(BF16)
    hp_ref[...] = _pack_rows(h2)

    e1, e2, w1, w2 = _route_rows(_dot_nt(wrt_ref[...], h2), br_ref[...])
    e_ref[0:1, :] = e1
    e_ref[1:2, :] = e2
    gate_ref[0:1, :] = w1
    gate_ref[1:2, :] = w2

    sub = lax.broadcasted_iota(jnp.int32, (N_EXPERTS, tm), 0)
    before = (lax.broadcasted_iota(jnp.int32, (tm, tm), 0) < lax.broadcasted_iota(jnp.int32, (tm, tm), 1)).astype(BF16)
    o1 = (sub == e1).astype(F32)
    o2 = (sub == e2).astype(F32)
    p1 = _dot(o1.astype(BF16), before)
    p2 = _dot(o2.astype(BF16), before)
    c1 = o1.sum(axis=1, keepdims=True)
    c2 = o2.sum(axis=1, keepdims=True)
    base = base_ref[:, 0:1]
    rank_ref[0:1, :] = jnp.sum(o1 * (base + p1), axis=0, keepdims=True).astype(jnp.int32)
    rank_ref[1:2, :] = jnp.sum(o2 * (base + c1 + p2), axis=0, keepdims=True).astype(jnp.int32)
    base_ref[...] = base_ref[...] + (c1 + c2)
    cnt_ref[...] = base_ref[...]


def _outproj(l, oa_c, oa_l, ob_c, ob_l, oc, w_out_b, x_ctx, x_lat, mods, ln_g, ln_b, w_router_t, b_router_col):
    tm = TM_OUT
    nct = T_CTX // tm

    def row(i):
        return (i, 0)

    def ctx_row(i):
        return (jnp.minimum(i, nct - 1), 0)

    def lat_row(i):
        return (jnp.maximum(i - nct, 0), 0)

    def tok(i):
        return (0, i)

    def mod(kind):
        return pl.BlockSpec((None, None, None, 1, D_MODEL), lambda i: (l, _cond_row(i, tm), kind, 0, 0))

    def wblk(rows, blk):
        return pl.BlockSpec((None, rows, D_MODEL), lambda i: (l, blk, 0), pipeline_mode=pl.Buffered(1))

    vec = pl.BlockSpec((None, 1, D_MODEL), lambda i: (l, 0, 0))
    return pl.pallas_call(
        _outproj_kernel,
        grid=(T_ALL // tm,),
        in_specs=[
            pl.BlockSpec((tm, NA_W), ctx_row), pl.BlockSpec((tm, NA_W), lat_row),
            pl.BlockSpec((tm, MLA_W), ctx_row), pl.BlockSpec((tm, MLA_W), lat_row),
            pl.BlockSpec((tm, SGU_W), row),
            wblk(NA_W, 0), wblk(MLA_W, 1), wblk(SGU_W, (NA_W + MLA_W) // SGU_W),
            pl.BlockSpec((tm, D_MODEL), ctx_row), pl.BlockSpec((tm, D_MODEL), lat_row),
            mod(2), mod(3), mod(4), vec, vec,
            pl.BlockSpec((N_EXPERTS, D_MODEL), lambda i: (0, 0)),
            pl.BlockSpec((N_EXPERTS, 1), lambda i: (0, 0)),
        ],
        out_specs=[pl.BlockSpec((tm, D_MODEL), row), pl.BlockSpec((tm, *ROW_TILE), lambda i: (i, 0, 0)),
                   pl.BlockSpec((TOP_K, tm), tok), pl.BlockSpec((TOP_K, tm), tok), pl.BlockSpec((TOP_K, tm), tok),
                   pl.BlockSpec((N_EXPERTS, LANES), lambda i: (0, 0))],
        out_shape=[jax.ShapeDtypeStruct((T_ALL, D_MODEL), F32),
                   jax.ShapeDtypeStruct((T_ALL, *ROW_TILE), jnp.uint32),
                   jax.ShapeDtypeStruct((TOP_K, T_ALL), jnp.int32),
                   jax.ShapeDtypeStruct((TOP_K, T_ALL), jnp.int32),
                   jax.ShapeDtypeStruct((TOP_K, T_ALL), F32),
                   jax.ShapeDtypeStruct((N_EXPERTS, LANES), F32)],
        scratch_shapes=[pltpu.VMEM((N_EXPERTS, LANES), F32)],
        compiler_params=_cparams(("arbitrary",)),
        name="out_proj",
    )(oa_c, oa_l, ob_c, ob_l, oc, w_out_b, w_out_b, w_out_b, x_ctx, x_lat, mods, mods, mods,
      ln_g.reshape(DEPTH, 1, D_MODEL), ln_b.reshape(DEPTH, 1, D_MODEL), w_router_t, b_router_col)


def _dispatch_plan(e, rank, counts, max_tiles):
    tiles = (counts + TM_EXP - 1) // TM_EXP
    tile_end = jnp.cumsum(tiles)
    off = (tile_end - tiles) * TM_EXP
    n_tiles = tile_end[-1]
    ids = jnp.arange(N_EXPERTS, dtype=jnp.int32)
    pos = rank + jnp.sum(jnp.where(e[..., None] == ids, off, 0), axis=-1)
    ti = jnp.arange(max_tiles, dtype=jnp.int32)
    owner = jnp.sum((tile_end[None, :] <= ti[:, None]).astype(jnp.int32), axis=1)
    last = jnp.sum((tile_end <= n_tiles - 1).astype(jnp.int32))
    tile_expert = jnp.where(ti < n_tiles, owner, last).astype(jnp.int32)
    later_used = jnp.logical_and(ids[None, :] > ids[:, None], tiles[None, :] > 0)
    next_expert = jnp.min(jnp.where(later_used, ids[None, :], N_EXPERTS), axis=1)
    next_expert = jnp.where(next_expert < N_EXPERTS, next_expert, -1).astype(jnp.int32)
    pad_lo = (off + counts).astype(jnp.int32)
    pad_hi = (off + tiles * TM_EXP).astype(jnp.int32)
    return pos.astype(jnp.int32), tile_expert, next_expert, n_tiles.reshape(1).astype(jnp.int32), pad_lo, pad_hi


def _row_copy(src_ref, src_row, dst_ref, dst_row, sem):
    return pltpu.make_async_copy(src_ref.at[pl.ds(src_row, 1)], dst_ref.at[pl.ds(dst_row, 1)], sem)


def _dispatch_kernel(lo_ref, hi_ref, nt_ref, p0_ref, p1_ref, hp_ref, xs_ref, zeros, sem, *, max_tiles):
    @pl.when(pl.program_id(0) == 0)
    def _():
        zeros[...] = jnp.zeros_like(zeros)

        def fill(r, carry):
            _row_copy(zeros, 0, xs_ref, r, sem).start()
            return carry

        def fill_done(r, carry):
            _row_copy(zeros, 0, xs_ref, 0, sem).wait()
            return carry

        for ex in range(N_EXPERTS):
            lax.fori_loop(lo_ref[ex], hi_ref[ex], fill, 0)
        for ex in range(N_EXPERTS):
            lax.fori_loop(lo_ref[ex], hi_ref[ex], fill_done, 0)

        def tile_copy(t):
            return pltpu.make_async_copy(zeros, xs_ref.at[pl.ds(t * TM_EXP, TM_EXP)], sem)

        def fill_tile(t, carry):
            tile_copy(t).start()
            return carry

        def fill_tile_done(t, carry):
            tile_copy(0).wait()
            return carry

        lax.fori_loop(nt_ref[0], max_tiles, fill_tile, 0)
        lax.fori_loop(nt_ref[0], max_tiles, fill_tile_done, 0)

    def issue(t, carry):
        for p_ref in (p0_ref, p1_ref):
            _row_copy(hp_ref, t, xs_ref, p_ref[t], sem).start()
        return carry

    lax.fori_loop(0, TM_DISP, issue, 0, unroll=8)

    def drain(t, carry):
        for k in range(TOP_K):
            _row_copy(hp_ref, 0, xs_ref, 0, sem).wait()
        return carry

    lax.fori_loop(0, TM_DISP, drain, 0, unroll=8)


def _dispatch(pad_lo, pad_hi, n_tiles, pos0, pos1, hp, max_tiles):
    grid_spec = pltpu.PrefetchScalarGridSpec(
        num_scalar_prefetch=3,
        grid=(T_ALL // TM_DISP,),
        in_specs=[
            pl.BlockSpec((TM_DISP,), lambda i, lo, hi, nt: (i,), memory_space=pltpu.SMEM),
            pl.BlockSpec((TM_DISP,), lambda i, lo, hi, nt: (i,), memory_space=pltpu.SMEM),
            pl.BlockSpec((TM_DISP, *ROW_TILE), lambda i, lo, hi, nt: (i, 0, 0)),
        ],
        out_specs=pl.BlockSpec(memory_space=pl.ANY),
        scratch_shapes=[pltpu.VMEM((TM_EXP, *ROW_TILE), jnp.uint32), pltpu.SemaphoreType.DMA(())],
    )
    return pl.pallas_call(
        functools.partial(_dispatch_kernel, max_tiles=max_tiles),
        grid_spec=grid_spec,
        out_shape=jax.ShapeDtypeStruct((max_tiles * TM_EXP, *ROW_TILE), jnp.uint32),
        compiler_params=_cparams(("arbitrary",)),
        name="dispatch",
    )(pad_lo, pad_hi, n_tiles, pos0, pos1, hp)


def _expert_kernel(te_ref, nxt_ref, nt_ref, x_ref, wg_hbm, wu_hbm, wd_hbm, y_ref,
                   wg_f, wu_f, wd_f, wg_s, wu_s, wd_s, switches, sem, *, layer):
    i = pl.program_id(0)
    e = te_ref[i]

    def weight_copies(expert, slot):
        return [pltpu.make_async_copy(w_hbm.at[layer, expert], buf.at[slot], sem.at[slot, j])
                for j, (w_hbm, buf) in enumerate(((wg_hbm, wg_f), (wu_hbm, wu_f), (wd_hbm, wd_f)))]

    @pl.when(i == 0)
    def _():
        for c in weight_copies(e, 0):
            c.start()

    fresh = jnp.logical_or(i == 0, e != te_ref[jnp.maximum(i - 1, 0)])

    @pl.when(fresh)
    def _():
        n_sw = jnp.where(i == 0, 0, switches[0] + 1)
        switches[0] = n_sw
        slot = n_sw % 2
        for c in weight_copies(e, slot):
            c.wait()
        wg_s[...] = wg_f[slot].astype(BF16)
        wu_s[...] = wu_f[slot].astype(BF16)
        wd_s[...] = wd_f[slot].astype(BF16)
        nx = nxt_ref[e]

        @pl.when(nx >= 0)
        def _():
            for c in weight_copies(nx, 1 - slot):
                c.start()

    @pl.when(i < nt_ref[0])
    def _():
        x = _unpack_rows(x_ref[...]).astype(BF16)
        g = _dot(x, wg_s[...])
        u = _dot(x, wu_s[...])
        a = (g * jax.nn.sigmoid(g) * u).astype(BF16)
        y_ref[...] = _pack_rows(_dot(a, wd_s[...]))

    @pl.when(i >= nt_ref[0])
    def _():
        y_ref[...] = jnp.zeros_like(y_ref)


def _experts(l, tile_expert, next_expert, n_tiles, xs, w_gate, w_up, w_down, max_tiles):
    tm = TM_EXP
    anyspec = pl.BlockSpec(memory_space=pl.ANY)
    grid_spec = pltpu.PrefetchScalarGridSpec(
        num_scalar_prefetch=3,
        grid=(max_tiles,),
        in_specs=[pl.BlockSpec((tm, *ROW_TILE), lambda i, te, nx, nt: (i, 0, 0)), anyspec, anyspec, anyspec],
        out_specs=pl.BlockSpec((tm, *ROW_TILE), lambda i, te, nx, nt: (i, 0, 0)),
        scratch_shapes=[pltpu.VMEM((2, D_MODEL, D_EXPERT), F32), pltpu.VMEM((2, D_MODEL, D_EXPERT), F32),
                        pltpu.VMEM((2, D_EXPERT, D_MODEL), F32),
                        pltpu.VMEM((D_MODEL, D_EXPERT), BF16), pltpu.VMEM((D_MODEL, D_EXPERT), BF16),
                        pltpu.VMEM((D_EXPERT, D_MODEL), BF16),
                        pltpu.SMEM((1,), jnp.int32), pltpu.SemaphoreType.DMA((2, 3))],
    )
    return pl.pallas_call(
        functools.partial(_expert_kernel, layer=l),
        grid_spec=grid_spec,
        out_shape=jax.ShapeDtypeStruct((max_tiles * tm, *ROW_TILE), jnp.uint32),
        compiler_params=_cparams(("arbitrary",)),
        name="experts",
    )(tile_expert, next_expert, n_tiles, xs, w_gate, w_up, w_down)


def _tail_kernel(p0_ref, p1_ref, p0n_ref, p1n_ref, x1_ref, gate_ref, g2_ref, lg_ref, lb_ref, ys_ref, o_ref,
                 ybuf, sem, *, n_tiles):
    tm = TM_TAIL
    i = pl.program_id(0)
    slot = i % 2

    def issue(p_refs, s):
        def body(t, carry):
            for k, p_ref in enumerate(p_refs):
                _row_copy(ys_ref, p_ref[t], ybuf.at[s, k], t, sem.at[s]).start()
            return carry

        lax.fori_loop(0, tm, body, 0, unroll=8)

    @pl.when(i == 0)
    def _():
        issue((p0_ref, p1_ref), 0)

    @pl.when(i + 1 < n_tiles)
    def _():
        issue((p0n_ref, p1n_ref), 1 - slot)

    def drain(t, carry):
        for k in range(TOP_K):
            _row_copy(ys_ref, 0, ybuf.at[slot, k], 0, sem.at[slot]).wait()
        return carry

    lax.fori_loop(0, tm, drain, 0, unroll=8)

    gate = gate_ref[...]
    y = gate[:, 0:1] * _unpack_rows(ybuf[slot, 0]) + gate[:, 1:2] * _unpack_rows(ybuf[slot, 1])
    o_ref[...] = _ln(ALPHA * x1_ref[...] + g2_ref[...] * y) * lg_ref[...] + lb_ref[...]


def _tail(l, t_start, t_count, pos0, pos1, x1, gate_t, mods, ln_g, ln_b, ys):
    tm = TM_TAIL
    n = t_count // tm
    i0 = t_start // tm
    vec = pl.BlockSpec((None, 1, D_MODEL), lambda i: (l, 0, 0))
    return pl.pallas_call(
        functools.partial(_tail_kernel, n_tiles=n),
        grid=(n,),
        in_specs=[
            pl.BlockSpec((tm,), lambda i: (i0 + i,), memory_space=pltpu.SMEM),
            pl.BlockSpec((tm,), lambda i: (i0 + i,), memory_space=pltpu.SMEM),
            pl.BlockSpec((tm,), lambda i: (i0 + jnp.minimum(i + 1, n - 1),), memory_space=pltpu.SMEM),
            pl.BlockSpec((tm,), lambda i: (i0 + jnp.minimum(i + 1, n - 1),), memory_space=pltpu.SMEM),
            pl.BlockSpec((tm, D_MODEL), lambda i: (i0 + i, 0)),
            pl.BlockSpec((tm, TOP_K), lambda i: (i0 + i, 0)),
            pl.BlockSpec((None, None, None, 1, D_MODEL), lambda i: (l, _cond_row(i0 + i, tm), 5, 0, 0)),
            vec, vec,
            pl.BlockSpec(memory_space=pl.ANY),
        ],
        out_specs=pl.BlockSpec((tm, D_MODEL), lambda i: (i, 0)),
        out_shape=jax.ShapeDtypeStruct((t_count, D_MODEL), F32),
        scratch_shapes=[pltpu.VMEM((2, TOP_K, tm, *ROW_TILE), jnp.uint32), pltpu.SemaphoreType.DMA((2,))],
        compiler_params=_cparams(("arbitrary",)),
        name="tail",
    )(pos0, pos1, pos0, pos1, x1, gate_t, mods, ln_g.reshape(DEPTH, 1, D_MODEL), ln_b.reshape(DEPTH, 1, D_MODEL), ys)


def _swap_partners(w):
    nf = ROPE_DIM // 4
    return jnp.concatenate([w[..., nf:2 * nf], w[..., :nf], w[..., 3 * nf:], w[..., 2 * nf:3 * nf]], axis=-1)


def _prep_w_in(w_in):
    kr = w_in[..., C_KR:C_KR + ROPE_DIM]
    w_kr = jnp.concatenate([kr, _swap_partners(kr)], axis=-1).astype(BF16)
    return w_in.astype(BF16), w_kr, w_in[..., C_KR + ROPE_DIM:].astype(BF16)


def _prep_w_uq(w_uq):
    w = w_uq.reshape(DEPTH, Q_LORA, MLA_HEADS, NOPE_DIM + ROPE_DIM)
    return jnp.concatenate([w, _swap_partners(w[..., NOPE_DIM:])], axis=-1).reshape(
        DEPTH, Q_LORA, MLA_HEADS * MLA_QK).astype(BF16)


def _prep_w_ukv(w_ukv):
    w = w_ukv.reshape(DEPTH, KV_LORA, MLA_HEADS, NOPE_DIM + V_DIM)
    return jnp.concatenate([w[..., :NOPE_DIM].reshape(DEPTH, KV_LORA, MLA_W),
                            w[..., NOPE_DIM:].reshape(DEPTH, KV_LORA, MLA_W)], axis=-1).astype(BF16)


def _rope_table(tm):
    half = ROPE_DIM // 2
    nf = half // 2
    t = jnp.arange(DEC_SEQ)
    inv = ROPE_BASE ** (-jnp.arange(nf, dtype=F32) * 2.0 / half)
    ar = (t // GRID_W).astype(F32)[:, None] * inv[None, :]
    ac = (t % GRID_W).astype(F32)[:, None] * inv[None, :]
    cos = jnp.concatenate([jnp.cos(ar), jnp.cos(ar), jnp.cos(ac), jnp.cos(ac)], axis=-1)
    sin = jnp.concatenate([-jnp.sin(ar), jnp.sin(ar), -jnp.sin(ac), jnp.sin(ac)], axis=-1)
    ident = jnp.concatenate([jnp.ones((tm, ROPE_DIM), F32), jnp.zeros((tm, ROPE_DIM), F32)], axis=-1)
    return jnp.concatenate([ident, jnp.concatenate([cos, sin], axis=-1)], axis=0)


def _na_bias_table(na_rpb):
    cols = np.arange(GRID_W)
    col_start = np.clip(cols - NA_COLS // 2, 0, GRID_W - NA_COLS)
    col_ok = (cols[None, :] >= col_start[:, None]) & (cols[None, :] < col_start[:, None] + NA_COLS)
    d_col = np.clip(cols[None, :] - cols[:, None] + NA_COLS - 1, 0, 2 * NA_COLS - 2)
    pick_col = (d_col[:, :, None] == np.arange(2 * NA_COLS - 1)[None, None, :]).astype(np.float32)
    tab = jnp.einsum("lhrc,qwc->lhrqw", na_rpb, pick_col, precision=lax.Precision.HIGHEST)
    tab = jnp.where(col_ok[None, None, None], tab, NEG_BIG)
    nxt = jnp.concatenate([tab[:, :, 1:], jnp.full_like(tab[:, :, :1], NEG_BIG)], axis=2)
    return jnp.concatenate([tab, nxt], axis=-1)


def kernel(x_prompt, x_sample, cache_na_k, cache_na_v, cache_mla_ckv, cache_mla_krope, c, c_ctx, w_ada, b_ada,
           w_in, g_q, g_kv, w_uq, w_ukv, na_rpb, sgu_ln_g, sgu_ln_b, sgu_w, sgu_b, w_out, ln1_g, ln1_b, ln2_g,
           ln2_b, w_router, b_router, w_gate, w_up, w_down):
    max_tiles = (T_ALL * TOP_K) // TM_EXP + N_EXPERTS

    cond = jnp.zeros((N_COND, D_MODEL), F32).at[0].set(c_ctx).at[1:1 + DEC_BATCH].set(c)
    mods = _ada(cond, w_ada, b_ada).reshape(DEPTH, N_COND, 6, 1, D_MODEL)

    w_main, w_kr, w_z = _prep_w_in(w_in)
    w_uq_p = _prep_w_uq(w_uq)
    w_ukv_p = _prep_w_ukv(w_ukv)
    w_out_b = w_out.astype(BF16)
    w_router_t = w_router.T.astype(BF16)
    b_router_col = b_router.reshape(N_EXPERTS, 1).astype(F32)
    cs_tab = _rope_table(TM_PROJ)
    bias_tab = _na_bias_table(na_rpb)
    cache_k = cache_na_k.reshape(DEC_BATCH, DEPTH, PAST_LEN, NA_W)
    cache_v = cache_na_v.reshape(DEC_BATCH, DEPTH, PAST_LEN, NA_W)
    kmc, vvc = _ctxkv(cache_mla_ckv, cache_mla_krope, w_ukv_p)

    new_k = jnp.zeros((BATCH, DEPTH, SEQ, NA_W), F32)
    new_v = jnp.zeros((BATCH, DEPTH, SEQ, NA_W), F32)
    new_ckv = jnp.zeros((BATCH, DEPTH, SEQ, KV_LORA), F32)
    new_kr = jnp.zeros((BATCH, DEPTH, SEQ, ROPE_DIM), F32)

    x_ctx = x_prompt.reshape(T_CTX, D_MODEL)
    x_lat = x_sample.reshape(T_LAT, D_MODEL)
    for l in range(DEPTH):
        q, k, v, new_k, new_v, qm, km, vv, new_ckv, new_kr, z = _inproj(
            l, x_ctx, x_lat, mods, w_main, w_kr, w_z, g_q, g_kv, w_uq_p, w_ukv_p, cs_tab, new_k, new_v, new_ckv, new_kr)

        oa_c, ob_c = _ctx_attn(q, k, v, qm, km, vv)
        oa_l = _lat_na(l, q, k, v, cache_k, cache_v, bias_tab)
        ob_l = _lat_mla(l, qm, km, vv, kmc, vvc)
        oc = _sgu(l, z, sgu_ln_g, sgu_ln_b, sgu_w, sgu_b)

        x1, hp, e, rank, gate, cnt = _outproj(l, oa_c, oa_l, ob_c, ob_l, oc, w_out_b, x_ctx, x_lat, mods, ln1_g, ln1_b,
                                              w_router_t, b_router_col)
        pos, tile_expert, next_expert, n_tiles, pad_lo, pad_hi = _dispatch_plan(
            e, rank, cnt[:, 0].astype(jnp.int32), max_tiles)
        pos0, pos1 = pos[0], pos[1]
        xs = _dispatch(pad_lo, pad_hi, n_tiles, pos0, pos1, hp, max_tiles)
        ys = _experts(l, tile_expert, next_expert, n_tiles, xs, w_gate, w_up, w_down, max_tiles)
        gate_t = gate.T
        x_ctx = _tail(l, 0, T_CTX, pos0, pos1, x1, gate_t, mods, ln2_g, ln2_b, ys)
        x_lat = _tail(l, T_CTX, T_LAT, pos0, pos1, x1, gate_t, mods, ln2_g, ln2_b, ys)

    return (x_ctx.reshape(BATCH, SEQ, D_MODEL), x_lat.reshape(DEC_BATCH, DEC_SEQ, D_MODEL),
            new_k.reshape(BATCH, DEPTH, SEQ, NA_HEADS, NA_HEAD_DIM), new_v.reshape(BATCH, DEPTH, SEQ, NA_HEADS, NA_HEAD_DIM),
            new_ckv, new_kr)
```

```python
import functools

import numpy as np
import jax
import jax.numpy as jnp
from jax import lax
from jax.experimental import pallas as pl
from jax.experimental.pallas import tpu as pltpu

F32 = jnp.float32
BF16 = jnp.bfloat16

D_MODEL = 2048
BATCH = 32
SEQ = 256
DEPTH = 2
DEC_BATCH = 2
DEC_SEQ = 2048
PAST_LEN = 256
GRID_W = 64
GRID_R = DEC_SEQ // GRID_W
NA_HEADS = 6
NA_HEAD_DIM = 128
NA_ROWS = 8
NA_COLS = 16
MLA_HEADS = 6
Q_LORA = 512
KV_LORA = 512
NOPE_DIM = 128
ROPE_DIM = 64
V_DIM = 128
ROPE_BASE = 10000.0
MLA_SCALE = (NOPE_DIM + ROPE_DIM) ** -0.5
NA_SCALE = NA_HEAD_DIM ** -0.5
SGU_GROUPS = 4
SGU_DIM = 128
CHUNK = 128
NA_W = NA_HEADS * NA_HEAD_DIM
MLA_W = MLA_HEADS * V_DIM
SGU_W = SGU_GROUPS * SGU_DIM
D_MIX = NA_W + MLA_W + SGU_W
N_EXPERTS = 16
N_GROUPS = 4
EXPERTS_PER_GROUP = N_EXPERTS // N_GROUPS
TOP_K = 2
D_EXPERT = 512
ALPHA = (2 * DEPTH) ** 0.25
LN_EPS = 1e-6
RMS_EPS = 1e-6

T_CTX = BATCH * SEQ
T_LAT = DEC_BATCH * DEC_SEQ
T_ALL = T_CTX + T_LAT
N_COND = 8

C_Q, C_K, C_V = 0, NA_W, 2 * NA_W
C_CQ = 3 * NA_W
C_CKV = C_CQ + Q_LORA
C_KR = C_CKV + KV_LORA
MLA_QK = 256
LANES = 128
ROW_TILE = (D_MODEL // 2 // LANES, LANES)

TM_PROJ = 256
TM_OUT = 512
TM_SGU = 256
TM_TAIL = 256
TM_DISP = 512
TM_EXP = 256
TQ_MLA = 256
MLA_KCHUNK = 512
NA_QROWS = 4
NA_KROWS = NA_QROWS + NA_ROWS
NEG_BIG = -1e30
VMEM_LIMIT = 56 * 1024 * 1024


def _cparams(sem, vmem=VMEM_LIMIT):
    return pltpu.CompilerParams(dimension_semantics=sem, vmem_limit_bytes=vmem)


def _cond_row(i, tm):
    t0 = i * tm
    return jnp.where(t0 < T_CTX, 0, 1 + (t0 - T_CTX) // DEC_SEQ)


def _ln(x):
    mu = jnp.mean(x, axis=-1, keepdims=True)
    xc = x - mu
    var = jnp.mean(xc * xc, axis=-1, keepdims=True)
    return xc * lax.rsqrt(var + LN_EPS)


def _rms(x):
    return x * lax.rsqrt(jnp.mean(x * x, axis=-1, keepdims=True) + RMS_EPS)


def _dot(a, b):
    return jnp.dot(a, b, preferred_element_type=F32)


def _pack_rows(x):
    xf = x.astype(BF16).astype(F32)
    half = x.shape[-1] // 2
    w = pltpu.bitcast(xf[:, :half], jnp.uint32) | (pltpu.bitcast(xf[:, half:], jnp.uint32) >> 16)
    return w.reshape(x.shape[0], *ROW_TILE)


def _unpack_rows(w3):
    w = w3.reshape(w3.shape[0], D_MODEL // 2)
    return jnp.concatenate([pltpu.bitcast(w & jnp.uint32(0xFFFF0000), F32), pltpu.bitcast(w << 16, F32)], axis=-1)


def _dot_nt(a, b):
    return lax.dot_general(a, b, (((1,), (1,)), ((), ())), preferred_element_type=F32)


def _ada_kernel(c_ref, w_ref, b_ref, o_ref):
    c = c_ref[...]
    s = c * jax.nn.sigmoid(c)
    o_ref[...] = _dot(s.astype(BF16), w_ref[...].astype(BF16)) + b_ref[...]


def _ada(cond, w_ada, b_ada):
    tn = 1024
    n = 6 * D_MODEL
    return pl.pallas_call(
        _ada_kernel,
        grid=(DEPTH, n // tn),
        in_specs=[
            pl.BlockSpec((N_COND, D_MODEL), lambda l, j: (0, 0)),
            pl.BlockSpec((None, D_MODEL, tn), lambda l, j: (l, 0, j)),
            pl.BlockSpec((None, 1, tn), lambda l, j: (l, 0, j)),
        ],
        out_specs=pl.BlockSpec((None, N_COND, tn), lambda l, j: (l, 0, j)),
        out_shape=jax.ShapeDtypeStruct((DEPTH, N_COND, n), F32),
        compiler_params=_cparams(("parallel", "parallel")),
        name="ada_mod",
    )(cond, w_ada, b_ada.reshape(DEPTH, 1, n))


def _inproj_kernel(xc_ref, xl_ref, sh_ref, sc_ref, w_ref, wkr_ref, wz_ref, gq_ref, gkv_ref, wuq_ref, wukv_ref, cs_ref,
                   k32_in, v32_in, ckv32_in, kr32_in,
                   q_ref, k_ref, v_ref, k32_ref, v32_ref, qm_ref, km_ref, vv_ref,
                   ckv32_ref, kr32_ref, z_ref):
    del k32_in, v32_in, ckv32_in, kr32_in
    i = pl.program_id(0)
    is_ctx = i * TM_PROJ < T_CTX
    x = jnp.where(is_ctx, xc_ref[...], xl_ref[...])
    h = (_ln(x) * (1.0 + sc_ref[...]) + sh_ref[...]).astype(BF16)

    def proj(a, b):
        return _dot(h, w_ref[:, a:b])

    q_ref[...] = proj(C_Q, C_K).astype(BF16)
    k = proj(C_K, C_V)
    k_ref[...] = k.astype(BF16)
    v = proj(C_V, C_CQ)
    v_ref[...] = v.astype(BF16)

    cs = cs_ref[...]
    lane = lax.broadcasted_iota(jnp.int32, cs.shape, 1)

    def rotate(pair):
        t = pair * cs
        return jnp.where(lane < ROPE_DIM, t + pltpu.roll(t, ROPE_DIM, 1), 0.0)

    cqn = (_rms(proj(C_CQ, C_CKV)) * gq_ref[...]).astype(BF16)
    mq = _dot(cqn, wuq_ref[...])
    for hd in range(MLA_HEADS):
        c0 = hd * MLA_QK
        qm_ref[:, c0:c0 + NOPE_DIM] = mq[:, c0:c0 + NOPE_DIM].astype(BF16)
        qm_ref[:, c0 + NOPE_DIM:c0 + MLA_QK] = rotate(mq[:, c0 + NOPE_DIM:c0 + MLA_QK]).astype(BF16)

    ckvn = _rms(proj(C_CKV, C_KR)) * gkv_ref[...]
    kv = _dot(ckvn.astype(BF16), wukv_ref[...])
    kr2 = _dot(h, wkr_ref[...])
    krot = rotate(kr2).astype(BF16)
    for hd in range(MLA_HEADS):
        c0 = hd * MLA_QK
        km_ref[:, c0:c0 + NOPE_DIM] = kv[:, hd * NOPE_DIM:(hd + 1) * NOPE_DIM].astype(BF16)
        km_ref[:, c0 + NOPE_DIM:c0 + MLA_QK] = krot
    vv_ref[...] = kv[:, MLA_HEADS * NOPE_DIM:].astype(BF16)

    z_ref[...] = _dot(h, wz_ref[...])

    @pl.when(is_ctx)
    def _():
        k32_ref[...] = k
        v32_ref[...] = v
        ckv32_ref[...] = ckvn
        kr32_ref[...] = kr2[:, :ROPE_DIM]


def _inproj(l, x_ctx, x_lat, mods, w_main, w_kr, w_z, g_q, g_kv, w_uq_p, w_ukv_p, cs_tab, new_k, new_v, new_ckv, new_kr):
    tm = TM_PROJ
    assert tm == SEQ
    nct = T_CTX // tm
    per_b = DEC_SEQ // tm

    def row(i):
        return (i, 0)

    def ctx_row(i):
        return (jnp.minimum(i, nct - 1), 0)

    def lat_row(i):
        return (jnp.maximum(i - nct, 0), 0)

    def cs_row(i):
        return (jnp.where(i < nct, 0, 1 + (i - nct) % per_b), 0)

    def mod(kind):
        return pl.BlockSpec((None, None, None, 1, D_MODEL), lambda i: (l, _cond_row(i, tm), kind, 0, 0))

    def const2(shape):
        return pl.BlockSpec(shape, lambda i: (l, 0, 0), pipeline_mode=pl.Buffered(1))

    def cache(w):
        return pl.BlockSpec((None, None, SEQ, w), lambda i: (jnp.minimum(i, nct - 1), l, 0, 0))

    anyspec = pl.BlockSpec(memory_space=pl.ANY)
    out_shapes = [
        jax.ShapeDtypeStruct((T_ALL, NA_W), BF16),
        jax.ShapeDtypeStruct((T_ALL, NA_W), BF16),
        jax.ShapeDtypeStruct((T_ALL, NA_W), BF16),
        jax.ShapeDtypeStruct(new_k.shape, F32),
        jax.ShapeDtypeStruct(new_v.shape, F32),
        jax.ShapeDtypeStruct((T_ALL, MLA_HEADS * MLA_QK), BF16),
        jax.ShapeDtypeStruct((T_ALL, MLA_HEADS * MLA_QK), BF16),
        jax.ShapeDtypeStruct((T_ALL, MLA_W), BF16),
        jax.ShapeDtypeStruct(new_ckv.shape, F32),
        jax.ShapeDtypeStruct(new_kr.shape, F32),
        jax.ShapeDtypeStruct((T_ALL, 2 * SGU_W), F32),
    ]
    out_specs = [
        pl.BlockSpec((tm, NA_W), row), pl.BlockSpec((tm, NA_W), row), pl.BlockSpec((tm, NA_W), row),
        cache(NA_W), cache(NA_W),
        pl.BlockSpec((tm, MLA_HEADS * MLA_QK), row), pl.BlockSpec((tm, MLA_HEADS * MLA_QK), row),
        pl.BlockSpec((tm, MLA_W), row),
        cache(KV_LORA), cache(ROPE_DIM),
        pl.BlockSpec((tm, 2 * SGU_W), row),
    ]
    return pl.pallas_call(
        _inproj_kernel,
        grid=(T_ALL // tm,),
        in_specs=[
            pl.BlockSpec((tm, D_MODEL), ctx_row), pl.BlockSpec((tm, D_MODEL), lat_row),
            mod(0), mod(1),
            const2((None, D_MODEL, C_KR)), const2((None, D_MODEL, 2 * ROPE_DIM)), const2((None, D_MODEL, 2 * SGU_W)),
            pl.BlockSpec((None, 1, Q_LORA), lambda i: (l, 0, 0)),
            pl.BlockSpec((None, 1, KV_LORA), lambda i: (l, 0, 0)),
            const2((None, Q_LORA, MLA_HEADS * MLA_QK)),
            const2((None, KV_LORA, 2 * MLA_W)),
            pl.BlockSpec((tm, 2 * ROPE_DIM), cs_row),
            anyspec, anyspec, anyspec, anyspec,
        ],
        out_specs=out_specs,
        out_shape=out_shapes,
        input_output_aliases={12: 3, 13: 4, 14: 8, 15: 9},
        compiler_params=_cparams(("arbitrary",)),
        name="in_proj",
    )(x_ctx, x_lat, mods, mods, w_main, w_kr, w_z, g_q.reshape(DEPTH, 1, Q_LORA), g_kv.reshape(DEPTH, 1, KV_LORA),
      w_uq_p, w_ukv_p, cs_tab, new_k, new_v, new_ckv, new_kr)


def _ctxkv_kernel(ckv_ref, kr_ref, w_ref, km_ref, vv_ref):
    kv = _dot(ckv_ref[...].astype(BF16), w_ref[...])
    kr = jnp.concatenate([kr_ref[...], jnp.zeros((PAST_LEN, ROPE_DIM), F32)], axis=-1).astype(BF16)
    for hd in range(MLA_HEADS):
        c0 = hd * MLA_QK
        km_ref[:, c0:c0 + NOPE_DIM] = kv[:, hd * NOPE_DIM:(hd + 1) * NOPE_DIM].astype(BF16)
        km_ref[:, c0 + NOPE_DIM:c0 + MLA_QK] = kr
    vv_ref[...] = kv[:, MLA_HEADS * NOPE_DIM:].astype(BF16)


def _ctxkv(cache_ckv, cache_krope, w_ukv_p):
    return pl.pallas_call(
        _ctxkv_kernel,
        grid=(DEPTH, DEC_BATCH),
        in_specs=[
            pl.BlockSpec((None, None, PAST_LEN, KV_LORA), lambda l, b: (b, l, 0, 0)),
            pl.BlockSpec((None, None, PAST_LEN, ROPE_DIM), lambda l, b: (b, l, 0, 0)),
            pl.BlockSpec((None, KV_LORA, 2 * MLA_W), lambda l, b: (l, 0, 0)),
        ],
        out_specs=[
            pl.BlockSpec((None, None, PAST_LEN, MLA_HEADS * MLA_QK), lambda l, b: (l, b, 0, 0)),
            pl.BlockSpec((None, None, PAST_LEN, MLA_W), lambda l, b: (l, b, 0, 0)),
        ],
        out_shape=[
            jax.ShapeDtypeStruct((DEPTH, DEC_BATCH, PAST_LEN, MLA_HEADS * MLA_QK), BF16),
            jax.ShapeDtypeStruct((DEPTH, DEC_BATCH, PAST_LEN, MLA_W), BF16),
        ],
        compiler_params=_cparams(("parallel", "parallel")),
        name="ctx_cache_kv",
    )(cache_ckv, cache_krope, w_ukv_p)


def _softmax_pv(blocks):
    ss = [s for s, _ in blocks]
    m = ss[0].max(axis=-1, keepdims=True)
    for s in ss[1:]:
        m = jnp.maximum(m, s.max(axis=-1, keepdims=True))
    num = None
    den = None
    for s, (_, v) in zip(ss, blocks):
        e = jnp.exp(s - m)
        d = e.sum(axis=-1, keepdims=True)
        o = _dot(e.astype(BF16), v)
        num = o if num is None else num + o
        den = d if den is None else den + d
    return num / den


def _ctx_attn_kernel(q_ref, k_ref, v_ref, qm_ref, km_ref, vv_ref, oa_ref, ob_ref):
    for hd in range(NA_HEADS):
        sl = slice(hd * NA_HEAD_DIM, (hd + 1) * NA_HEAD_DIM)
        s = _dot_nt(q_ref[:, sl], k_ref[:, sl]) * NA_SCALE
        oa_ref[:, sl] = _softmax_pv([(s, v_ref[:, sl])]).astype(BF16)
    for hd in range(MLA_HEADS):
        sq = slice(hd * MLA_QK, (hd + 1) * MLA_QK)
        sv = slice(hd * V_DIM, (hd + 1) * V_DIM)
        s = _dot_nt(qm_ref[:, sq], km_ref[:, sq]) * MLA_SCALE
        ob_ref[:, sv] = _softmax_pv([(s, vv_ref[:, sv])]).astype(BF16)


def _ctx_attn(q, k, v, qm, km, vv):
    def spec(w):
        return pl.BlockSpec((SEQ, w), lambda b: (b, 0))

    return pl.pallas_call(
        _ctx_attn_kernel,
        grid=(BATCH,),
        in_specs=[spec(NA_W), spec(NA_W), spec(NA_W),
                  spec(MLA_HEADS * MLA_QK), spec(MLA_HEADS * MLA_QK), spec(MLA_W)],
        out_specs=[spec(NA_W), spec(MLA_W)],
        out_shape=[jax.ShapeDtypeStruct((T_CTX, NA_W), BF16), jax.ShapeDtypeStruct((T_CTX, MLA_W), BF16)],
        compiler_params=_cparams(("parallel",)),
        name="ctx_attn",
    )(q, k, v, qm, km, vv)


def _lat_mla_kernel(q_ref, k_ref, v_ref, kc_ref, vc_ref, o_ref):
    q = q_ref[...]
    blocks = []
    for c in range(DEC_SEQ // MLA_KCHUNK):
        ks = slice(c * MLA_KCHUNK, (c + 1) * MLA_KCHUNK)
        blocks.append((_dot_nt(q, k_ref[ks, :]) * MLA_SCALE, v_ref[ks, :]))
    blocks.append((_dot_nt(q, kc_ref[...]) * MLA_SCALE, vc_ref[...]))
    o_ref[...] = _softmax_pv(blocks).astype(BF16)


def _lat_mla(l, qm, km, vv, kmc, vvc):
    tq = TQ_MLA
    nq = DEC_SEQ // tq
    row0 = T_CTX // tq
    kb0 = T_CTX // DEC_SEQ
    return pl.pallas_call(
        _lat_mla_kernel,
        grid=(DEC_BATCH, MLA_HEADS, nq),
        in_specs=[
            pl.BlockSpec((tq, MLA_QK), lambda b, h, t: (row0 + b * nq + t, h)),
            pl.BlockSpec((DEC_SEQ, MLA_QK), lambda b, h, t: (kb0 + b, h)),
            pl.BlockSpec((DEC_SEQ, V_DIM), lambda b, h, t: (kb0 + b, h)),
            pl.BlockSpec((None, None, PAST_LEN, MLA_QK), lambda b, h, t: (l, b, 0, h)),
            pl.BlockSpec((None, None, PAST_LEN, V_DIM), lambda b, h, t: (l, b, 0, h)),
        ],
        out_specs=pl.BlockSpec((tq, V_DIM), lambda b, h, t: (b * nq + t, h)),
        out_shape=jax.ShapeDtypeStruct((T_LAT, MLA_W), BF16),
        compiler_params=_cparams(("parallel", "parallel", "arbitrary")),
        name="lat_mla",
    )(qm, km, vv, kmc, vvc)


def _na_block_plan():
    plan = []
    for r0 in range(0, GRID_R, NA_QROWS):
        ws = min(max(r0 - NA_ROWS // 2, 0), GRID_R - NA_KROWS)
        rows = []
        for r in range(r0, r0 + NA_QROWS):
            rs = min(max(r - NA_ROWS // 2, 0), GRID_R - NA_ROWS)
            rel = [kr - r + NA_ROWS - 1 if rs <= kr < rs + NA_ROWS else None for kr in range(ws, ws + NA_KROWS)]
            rows.append([(rel[j], rel[j + 1]) for j in range(0, NA_KROWS, 2)])
        plan.append((ws, rows))
    return plan


def _lat_na_kernel(q_ref, k_ref, v_ref, ck_ref, cv_ref, tab_ref, o_ref):
    ck = ck_ref[...].astype(BF16)
    cv = cv_ref[...].astype(BF16)
    nq = NA_QROWS * GRID_W
    nk = NA_KROWS * GRID_W
    lane = lax.broadcasted_iota(jnp.int32, (GRID_W, 2 * GRID_W), 1)
    masked = jnp.full((GRID_W, 2 * GRID_W), NEG_BIG, F32)

    def pair_bias(d0, d1):
        if d0 is None and d1 is None:
            return masked
        if d1 is None:
            return jnp.where(lane < GRID_W, tab_ref[d0], NEG_BIG)
        if d0 is None:
            return jnp.where(lane >= GRID_W, tab_ref[d1 - 1], NEG_BIG)
        return tab_ref[d0]

    for blk, (ws, rows) in enumerate(_na_block_plan()):
        q = q_ref[blk * nq:(blk + 1) * nq, :]
        kw = k_ref[ws * GRID_W:ws * GRID_W + nk, :]
        vw = v_ref[ws * GRID_W:ws * GRID_W + nk, :]
        bias = jnp.concatenate(
            [jnp.concatenate([pair_bias(d0, d1) for d0, d1 in pairs], axis=1) for pairs in rows], axis=0)
        s1 = _dot_nt(q, kw) * NA_SCALE + bias
        s2 = _dot_nt(q, ck) * NA_SCALE
        o_ref[blk * nq:(blk + 1) * nq, :] = _softmax_pv([(s1, vw), (s2, cv)]).astype(BF16)


def _lat_na(l, q, k, v, cache_k, cache_v, bias_tab):
    kb0 = T_CTX // DEC_SEQ

    def lat(b, h):
        return (kb0 + b, h)

    return pl.pallas_call(
        _lat_na_kernel,
        grid=(DEC_BATCH, NA_HEADS),
        in_specs=[
            pl.BlockSpec((DEC_SEQ, NA_HEAD_DIM), lat),
            pl.BlockSpec((DEC_SEQ, NA_HEAD_DIM), lat),
            pl.BlockSpec((DEC_SEQ, NA_HEAD_DIM), lat),
            pl.BlockSpec((None, None, PAST_LEN, NA_HEAD_DIM), lambda b, h: (b, l, 0, h)),
            pl.BlockSpec((None, None, PAST_LEN, NA_HEAD_DIM), lambda b, h: (b, l, 0, h)),
            pl.BlockSpec((None, None, 2 * NA_ROWS - 1, GRID_W, 2 * GRID_W), lambda b, h: (l, h, 0, 0, 0)),
        ],
        out_specs=pl.BlockSpec((DEC_SEQ, NA_HEAD_DIM), lambda b, h: (b, h)),
        out_shape=jax.ShapeDtypeStruct((T_LAT, NA_W), BF16),
        compiler_params=_cparams(("parallel", "parallel")),
        name="lat_na",
    )(q, k, v, cache_k, cache_v, bias_tab)


def _sgu_kernel(z_ref, g_ref, b_ref, w_ref, bs_ref, o_ref):
    z = z_ref[...]
    a = 0.5 * z * (1.0 + lax.erf(z * (0.5 ** 0.5)))
    for c in range(TM_SGU // CHUNK):
        rows = slice(c * CHUNK, (c + 1) * CHUNK)
        for g in range(SGU_GROUPS):
            u = a[rows, g * SGU_DIM:(g + 1) * SGU_DIM]
            v = a[rows, SGU_W + g * SGU_DIM:SGU_W + (g + 1) * SGU_DIM]
            vn = _ln(v) * g_ref[g:g + 1, :] + b_ref[g:g + 1, :]
            t = _dot(w_ref[g].astype(BF16), vn.astype(BF16)) + bs_ref[g]
            o_ref[rows, g * SGU_DIM:(g + 1) * SGU_DIM] = (u * t).astype(BF16)


def _sgu(l, z, ln_g, ln_b, w_s, b_s):
    tm = TM_SGU
    return pl.pallas_call(
        _sgu_kernel,
        grid=(T_ALL // tm,),
        in_specs=[
            pl.BlockSpec((tm, 2 * SGU_W), lambda i: (i, 0)),
            pl.BlockSpec((None, SGU_GROUPS, SGU_DIM), lambda i: (l, 0, 0)),
            pl.BlockSpec((None, SGU_GROUPS, SGU_DIM), lambda i: (l, 0, 0)),
            pl.BlockSpec((None, SGU_GROUPS, CHUNK, CHUNK), lambda i: (l, 0, 0, 0)),
            pl.BlockSpec((None, SGU_GROUPS, CHUNK, 1), lambda i: (l, 0, 0, 0)),
        ],
        out_specs=pl.BlockSpec((tm, SGU_W), lambda i: (i, 0)),
        out_shape=jax.ShapeDtypeStruct((T_ALL, SGU_W), BF16),
        compiler_params=_cparams(("parallel",)),
        name="sgu",
    )(z, ln_g, ln_b, w_s, b_s.reshape(DEPTH, SGU_GROUPS, CHUNK, 1))


def _route_rows(logits_t, bias_col):
    s = jax.nn.sigmoid(logits_t)
    sb = s + bias_col
    rows = [sb[r:r + 1, :] for r in range(N_EXPERTS)]
    srows = [s[r:r + 1, :] for r in range(N_EXPERTS)]
    best, gsel = None, None
    for g in range(N_GROUPS):
        m = rows[g * EXPERTS_PER_GROUP:(g + 1) * EXPERTS_PER_GROUP]
        score = None
        for a in range(EXPERTS_PER_GROUP):
            for b in range(a + 1, EXPERTS_PER_GROUP):
                pair = m[a] + m[b]
                score = pair if score is None else jnp.maximum(score, pair)
        if best is None:
            best, gsel = score, jnp.zeros(score.shape, jnp.int32)
        else:
            upd = score > best
            best = jnp.where(upd, score, best)
            gsel = jnp.where(upd, g, gsel)

    def pick(table, j):
        out = table[j]
        for g in range(1, N_GROUPS):
            out = jnp.where(gsel == g, table[g * EXPERTS_PER_GROUP + j], out)
        return out

    v = [pick(rows, j) for j in range(EXPERTS_PER_GROUP)]
    sv = [pick(srows, j) for j in range(EXPERTS_PER_GROUP)]
    m1, i1, w1 = v[0], jnp.zeros(gsel.shape, jnp.int32), sv[0]
    for j in range(1, EXPERTS_PER_GROUP):
        upd = v[j] > m1
        m1 = jnp.where(upd, v[j], m1)
        i1 = jnp.where(upd, j, i1)
        w1 = jnp.where(upd, sv[j], w1)
    m2 = jnp.full(gsel.shape, -jnp.inf, F32)
    i2 = jnp.zeros(gsel.shape, jnp.int32)
    w2 = jnp.zeros(gsel.shape, F32)
    for j in range(EXPERTS_PER_GROUP):
        upd = jnp.logical_and(i1 != j, v[j] > m2)
        m2 = jnp.where(upd, v[j], m2)
        i2 = jnp.where(upd, j, i2)
        w2 = jnp.where(upd, sv[j], w2)
    den = w1 + w2
    return gsel * EXPERTS_PER_GROUP + i1, gsel * EXPERTS_PER_GROUP + i2, w1 / den, w2 / den


def _outproj_kernel(ac_ref, al_ref, bc_ref, bl_ref, c_ref, wa_ref, wb_ref, wc_ref, xc_ref, xl_ref, g1_ref, sh2_ref, sc2_ref,
                    lg_ref, lb_ref, wrt_ref, br_ref, x1_ref, hp_ref, e_ref, rank_ref, gate_ref, cnt_ref, base_ref):
    is_ctx = pl.program_id(0) * TM_OUT < T_CTX

    @pl.when(pl.program_id(0) == 0)
    def _():
        base_ref[...] = jnp.zeros_like(base_ref)

    tm = TM_OUT
    a = jnp.where(is_ctx, ac_ref[...], al_ref[...])
    b = jnp.where(is_ctx, bc_ref[...], bl_ref[...])
    x = jnp.where(is_ctx, xc_ref[...], xl_ref[...])
    mixed = _dot(a, wa_ref[...]) + _dot(b, wb_ref[...]) + _dot(c_ref[...], wc_ref[...])
    x1 = _ln(ALPHA * x + g1_ref[...] * mixed) * lg_ref[...] + lb_ref[...]
    x1_ref[...] = x1
    h2 = (_ln(x1) * (1.0 + sc2_ref[...]) + sh2_ref[...]).astype(BF16)
    hp_ref[...] = _pack_rows(h2)

    e1, e2, w1, w2 = _route_rows(_dot_nt(wrt_ref[...], h2), br_ref[...])
    e_ref[0:1, :] = e1
    e_ref[1:2, :] = e2
    gate_ref[0:1, :] = w1
    gate_ref[1:2, :] = w2

    sub = lax.broadcasted_iota(jnp.int32, (N_EXPERTS, tm), 0)
    before = (lax.broadcasted_iota(jnp.int32, (tm, tm), 0) < lax.broadcasted_iota(jnp.int32, (tm, tm), 1)).astype(BF16)
    o1 = (sub == e1).astype(F32)
    o2 = (sub == e2).astype(F32)
    p1 = _dot(o1.astype(BF16), before)
    p2 = _dot(o2.astype(BF16), before)
    c1 = o1.sum(axis=1, keepdims=True)
    c2 = o2.sum(axis=1, keepdims=True)
    base = base_ref[:, 0:1]
    rank_ref[0:1, :] = jnp.sum(o1 * (base + p1), axis=0, keepdims=True).astype(jnp.int32)
    rank_ref[1:2, :] = jnp.sum(o2 * (base + c1 + p2), axis=0, keepdims=True).astype(jnp.int32)
    base_ref[...] = base_ref[...] + (c1 + c2)
    cnt_ref[...] = base_ref[...]


def _outproj(l, oa_c, oa_l, ob_c, ob_l, oc, w_out_b, x_ctx, x_lat, mods, ln_g, ln_b, w_router_t, b_router_col):
    tm = TM_OUT
    nct = T_CTX // tm

    def row(i):
        return (i, 0)

    def ctx_row(i):
        return (jnp.minimum(i, nct - 1), 0)

    def lat_row(i):
        return (jnp.maximum(i - nct, 0), 0)

    def tok(i):
        return (0, i)

    def mod(kind):
        return pl.BlockSpec((None, None, None, 1, D_MODEL), lambda i: (l, _cond_row(i, tm), kind, 0, 0))

    def wblk(rows, blk):
        return pl.BlockSpec((None, rows, D_MODEL), lambda i: (l, blk, 0), pipeline_mode=pl.Buffered(1))

    vec = pl.BlockSpec((None, 1, D_MODEL), lambda i: (l, 0, 0))
    return pl.pallas_call(
        _outproj_kernel,
        grid=(T_ALL // tm,),
        in_specs=[
            pl.BlockSpec((tm, NA_W), ctx_row), pl.BlockSpec((tm, NA_W), lat_row),
            pl.BlockSpec((tm, MLA_W), ctx_row), pl.BlockSpec((tm, MLA_W), lat_row),
            pl.BlockSpec((tm, SGU_W), row),
            wblk(NA_W, 0), wblk(MLA_W, 1), wblk(SGU_W, (NA_W + MLA_W) // SGU_W),
            pl.BlockSpec((tm, D_MODEL), ctx_row), pl.BlockSpec((tm, D_MODEL), lat_row),
            mod(2), mod(3), mod(4), vec, vec,
            pl.BlockSpec((N_EXPERTS, D_MODEL), lambda i: (0, 0)),
            pl.BlockSpec((N_EXPERTS, 1), lambda i: (0, 0)),
        ],
        out_specs=[pl.BlockSpec((tm, D_MODEL), row), pl.BlockSpec((tm, *ROW_TILE), lambda i: (i, 0, 0)),
                   pl.BlockSpec((TOP_K, tm), tok), pl.BlockSpec((TOP_K, tm), tok), pl.BlockSpec((TOP_K, tm), tok),
                   pl.BlockSpec((N_EXPERTS, LANES), lambda i: (0, 0))],
        out_shape=[jax.ShapeDtypeStruct((T_ALL, D_MODEL), F32),
                   jax.ShapeDtypeStruct((T_ALL, *ROW_TILE), jnp.uint32),
                   jax.ShapeDtypeStruct((TOP_K, T_ALL), jnp.int32),
                   jax.ShapeDtypeStruct((TOP_K, T_ALL), jnp.int32),
                   jax.ShapeDtypeStruct((TOP_K, T_ALL), F32),
                   jax.ShapeDtypeStruct((N_EXPERTS, LANES), F32)],
        scratch_shapes=[pltpu.VMEM((N_EXPERTS, LANES), F32)],
        compiler_params=_cparams(("arbitrary",)),
        name="out_proj",
    )(oa_c, oa_l, ob_c, ob_l, oc, w_out_b, w_out_b, w_out_b, x_ctx, x_lat, mods, mods, mods,
      ln_g.reshape(DEPTH, 1, D_MODEL), ln_b.reshape(DEPTH, 1, D_MODEL), w_router_t, b_router_col)


def _dispatch_plan(e, rank, counts, max_tiles):
    tiles = (counts + TM_EXP - 1) // TM_EXP
    tile_end = jnp.cumsum(tiles)
    off = (tile_end - tiles) * TM_EXP
    n_tiles = tile_end[-1]
    ids = jnp.arange(N_EXPERTS, dtype=jnp.int32)
    pos = rank + jnp.sum(jnp.where(e[..., None] == ids, off, 0), axis=-1)
    ti = jnp.arange(max_tiles, dtype=jnp.int32)
    owner = jnp.sum((tile_end[None, :] <= ti[:, None]).astype(jnp.int32), axis=1)
    last = jnp.sum((tile_end <= n_tiles - 1).astype(jnp.int32))
    tile_expert = jnp.where(ti < n_tiles, owner, last).astype(jnp.int32)
    later_used = jnp.logical_and(ids[None, :] > ids[:, None], tiles[None, :] > 0)
    next_expert = jnp.min(jnp.where(later_used, ids[None, :], N_EXPERTS), axis=1)
    next_expert = jnp.where(next_expert < N_EXPERTS, next_expert, -1).astype(jnp.int32)
    pad_lo = (off + counts).astype(jnp.int32)
    pad_hi = (off + tiles * TM_EXP).astype(jnp.int32)
    return pos.astype(jnp.int32), tile_expert, next_expert, n_tiles.reshape(1).astype(jnp.int32), pad_lo, pad_hi


def _row_copy(src_ref, src_row, dst_ref, dst_row, sem):
    return pltpu.make_async_copy(src_ref.at[pl.ds(src_row, 1)], dst_ref.at[pl.ds(dst_row, 1)], sem)


def _dispatch_kernel(lo_ref, hi_ref, nt_ref, p0_ref, p1_ref, hp_ref, xs_ref, zeros, sem, *, max_tiles):
    @pl.when(pl.program_id(0) == 0)
    def _():
        zeros[...] = jnp.zeros_like(zeros)

        def fill(r, carry):
            _row_copy(zeros, 0, xs_ref, r, sem).start()
            return carry

        def fill_done(r, carry):
            _row_copy(zeros, 0, xs_ref, 0, sem).wait()
            return carry

        for ex in range(N_EXPERTS):
            lax.fori_loop(lo_ref[ex], hi_ref[ex], fill, 0)
        for ex in range(N_EXPERTS):
            lax.fori_loop(lo_ref[ex], hi_ref[ex], fill_done, 0)

        def tile_copy(t):
            return pltpu.make_async_copy(zeros, xs_ref.at[pl.ds(t * TM_EXP, TM_EXP)], sem)

        def fill_tile(t, carry):
            tile_copy(t).start()
            return carry

        def fill_tile_done(t, carry):
            tile_copy(0).wait()
            return carry

        lax.fori_loop(nt_ref[0], max_tiles, fill_tile, 0)
        lax.fori_loop(nt_ref[0], max_tiles, fill_tile_done, 0)

    def issue(t, carry):
        for p_ref in (p0_ref, p1_ref):
            _row_copy(hp_ref, t, xs_ref, p_ref[t], sem).start()
        return carry

    lax.fori_loop(0, TM_DISP, issue, 0, unroll=8)

    def drain(t, carry):
        for k in range(TOP_K):
            _row_copy(hp_ref, 0, xs_ref, 0, sem).wait()
        return carry

    lax.fori_loop(0, TM_DISP, drain, 0, unroll=8)


def _dispatch(pad_lo, pad_hi, n_tiles, pos0, pos1, hp, max_tiles):
    grid_spec = pltpu.PrefetchScalarGridSpec(
        num_scalar_prefetch=3,
        grid=(T_ALL // TM_DISP,),
        in_specs=[
            pl.BlockSpec((TM_DISP,), lambda i, lo, hi, nt: (i,), memory_space=pltpu.SMEM),
            pl.BlockSpec((TM_DISP,), lambda i, lo, hi, nt: (i,), memory_space=pltpu.SMEM),
            pl.BlockSpec((TM_DISP, *ROW_TILE), lambda i, lo, hi, nt: (i, 0, 0)),
        ],
        out_specs=pl.BlockSpec(memory_space=pl.ANY),
        scratch_shapes=[pltpu.VMEM((TM_EXP, *ROW_TILE), jnp.uint32), pltpu.SemaphoreType.DMA(())],
    )
    return pl.pallas_call(
        functools.partial(_dispatch_kernel, max_tiles=max_tiles),
        grid_spec=grid_spec,
        out_shape=jax.ShapeDtypeStruct((max_tiles * TM_EXP, *ROW_TILE), jnp.uint32),
        compiler_params=_cparams(("arbitrary",)),
        name="dispatch",
    )(pad_lo, pad_hi, n_tiles, pos0, pos1, hp)


def _expert_kernel(te_ref, nxt_ref, nt_ref, x_ref, wg_hbm, wu_hbm, wd_hbm, y_ref,
                   wg_f, wu_f, wd_f, wg_s, wu_s, wd_s, switches, sem, *, layer):
    i = pl.program_id(0)
    e = te_ref[i]

    def weight_copies(expert, slot):
        return [pltpu.make_async_copy(w_hbm.at[layer, expert], buf.at[slot], sem.at[slot, j])
                for j, (w_hbm, buf) in enumerate(((wg_hbm, wg_f), (wu_hbm, wu_f), (wd_hbm, wd_f)))]

    @pl.when(i == 0)
    def _():
        for c in weight_copies(e, 0):
            c.start()

    fresh = jnp.logical_or(i == 0, e != te_ref[jnp.maximum(i - 1, 0)])

    @pl.when(fresh)
    def _():
        n_sw = jnp.where(i == 0, 0, switches[0] + 1)
        switches[0] = n_sw
        slot = n_sw % 2
        for c in weight_copies(e, slot):
            c.wait()
        wg_s[...] = wg_f[slot].astype(BF16)
        wu_s[...] = wu_f[slot].astype(BF16)
        wd_s[...] = wd_f[slot].astype(BF16)
        nx = nxt_ref[e]

        @pl.when(nx >= 0)
        def _():
            for c in weight_copies(nx, 1 - slot):
                c.start()

    @pl.when(i < nt_ref[0])
    def _():
        x = _unpack_rows(x_ref[...]).astype(BF16)
        g = _dot(x, wg_s[...])
        u = _dot(x, wu_s[...])
        a = (g * jax.nn.sigmoid(g) * u).astype(BF16)
        y_ref[...] = _pack_rows(_dot(a, wd_s[...]))

    @pl.when(i >= nt_ref[0])
    def _():
        y_ref[...] = jnp.zeros_like(y_ref)


def _experts(l, tile_expert, next_expert, n_tiles, xs, w_gate, w_up, w_down, max_tiles):
    tm = TM_EXP
    anyspec = pl.BlockSpec(memory_space=pl.ANY)
    grid_spec = pltpu.PrefetchScalarGridSpec(
        num_scalar_prefetch=3,
        grid=(max_tiles,),
        in_specs=[pl.BlockSpec((tm, *ROW_TILE), lambda i, te, nx, nt: (i, 0, 0)), anyspec, anyspec, anyspec],
        out_specs=pl.BlockSpec((tm, *ROW_TILE), lambda i, te, nx, nt: (i, 0, 0)),
        scratch_shapes=[pltpu.VMEM((2, D_MODEL, D_EXPERT), F32), pltpu.VMEM((2, D_MODEL, D_EXPERT), F32),
                        pltpu.VMEM((2, D_EXPERT, D_MODEL), F32),
                        pltpu.VMEM((D_MODEL, D_EXPERT), BF16), pltpu.VMEM((D_MODEL, D_EXPERT), BF16),
                        pltpu.VMEM((D_EXPERT, D_MODEL), BF16),
                        pltpu.SMEM((1,), jnp.int32), pltpu.SemaphoreType.DMA((2, 3))],
    )
    return pl.pallas_call(
        functools.partial(_expert_kernel, layer=l),
        grid_spec=grid_spec,
        out_shape=jax.ShapeDtypeStruct((max_tiles * tm, *ROW_TILE), jnp.uint32),
        compiler_params=_cparams(("arbitrary",)),
        name="experts",
    )(tile_expert, next_expert, n_tiles, xs, w_gate, w_up, w_down)


def _tail_kernel(p0_ref, p1_ref, p0n_ref, p1n_ref, x1_ref, gate_ref, g2_ref, lg_ref, lb_ref, ys_ref, o_ref,
                 ybuf, sem, *, n_tiles):
    tm = TM_TAIL
    i = pl.program_id(0)
    slot = i % 2

    def issue(p_refs, s):
        def body(t, carry):
            for k, p_ref in enumerate(p_refs):
                _row_copy(ys_ref, p_ref[t], ybuf.at[s, k], t, sem.at[s]).start()
            return carry

        lax.fori_loop(0, tm, body, 0, unroll=8)

    @pl.when(i == 0)
    def _():
        issue((p0_ref, p1_ref), 0)

    @pl.when(i + 1 < n_tiles)
    def _():
        issue((p0n_ref, p1n_ref), 1 - slot)

    def drain(t, carry):
        for k in range(TOP_K):
            _row_copy(ys_ref, 0, ybuf.at[slot, k], 0, sem.at[slot]).wait()
        return carry

    lax.fori_loop(0, tm, drain, 0, unroll=8)

    gate = gate_ref[...]
    y = gate[:, 0:1] * _unpack_rows(ybuf[slot, 0]) + gate[:, 1:2] * _unpack_rows(ybuf[slot, 1])
    o_ref[...] = _ln(ALPHA * x1_ref[...] + g2_ref[...] * y) * lg_ref[...] + lb_ref[...]


def _tail(l, t_start, t_count, pos0, pos1, x1, gate_t, mods, ln_g, ln_b, ys):
    tm = TM_TAIL
    n = t_count // tm
    i0 = t_start // tm
    vec = pl.BlockSpec((None, 1, D_MODEL), lambda i: (l, 0, 0))
    return pl.pallas_call(
        functools.partial(_tail_kernel, n_tiles=n),
        grid=(n,),
        in_specs=[
            pl.BlockSpec((tm,), lambda i: (i0 + i,), memory_space=pltpu.SMEM),
            pl.BlockSpec((tm,), lambda i: (i0 + i,), memory_space=pltpu.SMEM),
            pl.BlockSpec((tm,), lambda i: (i0 + jnp.minimum(i + 1, n - 1),), memory_space=pltpu.SMEM),
            pl.BlockSpec((tm,), lambda i: (i0 + jnp.minimum(i + 1, n - 1),), memory_space=pltpu.SMEM),
            pl.BlockSpec((tm, D_MODEL), lambda i: (i0 + i, 0)),
            pl.BlockSpec((tm, TOP_K), lambda i: (i0 + i, 0)),
            pl.BlockSpec((None, None, None, 1, D_MODEL), lambda i: (l, _cond_row(i0 + i, tm), 5, 0, 0)),
            vec, vec,
            pl.BlockSpec(memory_space=pl.ANY),
        ],
        out_specs=pl.BlockSpec((tm, D_MODEL), lambda i: (i, 0)),
        out_shape=jax.ShapeDtypeStruct((t_count, D_MODEL), F32),
        scratch_shapes=[pltpu.VMEM((2, TOP_K, tm, *ROW_TILE), jnp.uint32), pltpu.SemaphoreType.DMA((2,))],
        compiler_params=_cparams(("arbitrary",)),
        name="tail",
    )(pos0, pos1, pos0, pos1, x1, gate_t, mods, ln_g.reshape(DEPTH, 1, D_MODEL), ln_b.reshape(DEPTH, 1, D_MODEL), ys)


def _swap_partners(w):
    nf = ROPE_DIM // 4
    return jnp.concatenate([w[..., nf:2 * nf], w[..., :nf], w[..., 3 * nf:], w[..., 2 * nf:3 * nf]], axis=-1)


def _prep_w_in(w_in):
    kr = w_in[..., C_KR:C_KR + ROPE_DIM]
    w_kr = jnp.concatenate([kr, _swap_partners(kr)], axis=-1).astype(BF16)
    return w_in.astype(BF16), w_kr, w_in[..., C_KR + ROPE_DIM:].astype(BF16)


def _prep_w_uq(w_uq):
    w = w_uq.reshape(DEPTH, Q_LORA, MLA_HEADS, NOPE_DIM + ROPE_DIM)
    return jnp.concatenate([w, _swap_partners(w[..., NOPE_DIM:])], axis=-1).reshape(
        DEPTH, Q_LORA, MLA_HEADS * MLA_QK).astype(BF16)


def _prep_w_ukv(w_ukv):
    w = w_ukv.reshape(DEPTH, KV_LORA, MLA_HEADS, NOPE_DIM + V_DIM)
    return jnp.concatenate([w[..., :NOPE_DIM].reshape(DEPTH, KV_LORA, MLA_W),
                            w[..., NOPE_DIM:].reshape(DEPTH, KV_LORA, MLA_W)], axis=-1).astype(BF16)


def _rope_table(tm):
    half = ROPE_DIM // 2
    nf = half // 2
    t = jnp.arange(DEC_SEQ)
    inv = ROPE_BASE ** (-jnp.arange(nf, dtype=F32) * 2.0 / half)
    ar = (t // GRID_W).astype(F32)[:, None] * inv[None, :]
    ac = (t % GRID_W).astype(F32)[:, None] * inv[None, :]
    cos = jnp.concatenate([jnp.cos(ar), jnp.cos(ar), jnp.cos(ac), jnp.cos(ac)], axis=-1)
    sin = jnp.concatenate([-jnp.sin(ar), jnp.sin(ar), -jnp.sin(ac), jnp.sin(ac)], axis=-1)
    ident = jnp.concatenate([jnp.ones((tm, ROPE_DIM), F32), jnp.zeros((tm, ROPE_DIM), F32)], axis=-1)
    return jnp.concatenate([ident, jnp.concatenate([cos, sin], axis=-1)], axis=0)


def _na_bias_table(na_rpb):
    cols = np.arange(GRID_W)
    col_start = np.clip(cols - NA_COLS // 2, 0, GRID_W - NA_COLS)
    col_ok = (cols[None, :] >= col_start[:, None]) & (cols[None, :] < col_start[:, None] + NA_COLS)
    d_col = np.clip(cols[None, :] - cols[:, None] + NA_COLS - 1, 0, 2 * NA_COLS - 2)
    pick_col = (d_col[:, :, None] == np.arange(2 * NA_COLS - 1)[None, None, :]).astype(np.float32)
    tab = jnp.einsum("lhrc,qwc->lhrqw", na_rpb, pick_col, precision=lax.Precision.HIGHEST)
    tab = jnp.where(col_ok[None, None, None], tab, NEG_BIG)
    nxt = jnp.concatenate([tab[:, :, 1:], jnp.full_like(tab[:, :, :1], NEG_BIG)], axis=2)
    return jnp.concatenate([tab, nxt], axis=-1)


def kernel(x_prompt, x_sample, cache_na_k, cache_na_v, cache_mla_ckv, cache_mla_krope, c, c_ctx, w_ada, b_ada,
           w_in, g_q, g_kv, w_uq, w_ukv, na_rpb, sgu_ln_g, sgu_ln_b, sgu_w, sgu_b, w_out, ln1_g, ln1_b, ln2_g,
           ln2_b, w_router, b_router, w_gate, w_up, w_down):
    max_tiles = (T_ALL * TOP_K) // TM_EXP + N_EXPERTS

    cond = jnp.zeros((N_COND, D_MODEL), F32).at[0].set(c_ctx).at[1:1 + DEC_BATCH].set(c)
    mods = _ada(cond, w_ada, b_ada).reshape(DEPTH, N_COND, 6, 1, D_MODEL)

    w_main, w_kr, w_z = _prep_w_in(w_in)
    w_uq_p = _prep_w_uq(w_uq)
    w_ukv_p = _prep_w_ukv(w_ukv)
    w_out_b = w_out.astype(BF16)
    w_router_t = w_router.T.astype(BF16)
    b_router_col = b_router.reshape(N_EXPERTS, 1).astype(F32)
    cs_tab = _rope_table(TM_PROJ)
    bias_tab = _na_bias_table(na_rpb)
    cache_k = cache_na_k.reshape(DEC_BATCH, DEPTH, PAST_LEN, NA_W)
    cache_v = cache_na_v.reshape(DEC_BATCH, DEPTH, PAST_LEN, NA_W)
    kmc, vvc = _ctxkv(cache_mla_ckv, cache_mla_krope, w_ukv_p)

    new_k = jnp.zeros((BATCH, DEPTH, SEQ, NA_W), F32)
    new_v = jnp.zeros((BATCH, DEPTH, SEQ, NA_W), F32)
    new_ckv = jnp.zeros((BATCH, DEPTH, SEQ, KV_LORA), F32)
    new_kr = jnp.zeros((BATCH, DEPTH, SEQ, ROPE_DIM), F32)

    x_ctx = x_prompt.reshape(T_CTX, D_MODEL)
    x_lat = x_sample.reshape(T_LAT, D_MODEL)
    for l in range(DEPTH):
        q, k, v, new_k, new_v, qm, km, vv, new_ckv, new_kr, z = _inproj(
            l, x_ctx, x_lat, mods, w_main, w_kr, w_z, g_q, g_kv, w_uq_p, w_ukv_p, cs_tab, new_k, new_v, new_ckv, new_kr)

        oa_c, ob_c = _ctx_attn(q, k, v, qm, km, vv)
        oa_l = _lat_na(l, q, k, v, cache_k, cache_v, bias_tab)
        ob_l = _lat_mla(l, qm, km, vv, kmc, vvc)
        oc = _sgu(l, z, sgu_ln_g, sgu_ln_b, sgu_w, sgu_b)

        x1, hp, e, rank, gate, cnt = _outproj(l, oa_c, oa_l, ob_c, ob_l, oc, w_out_b, x_ctx, x_lat, mods, ln1_g, ln1_b,
                                              w_router_t, b_router_col)
        pos, tile_expert, next_expert, n_tiles, pad_lo, pad_hi = _dispatch_plan(
            e, rank, cnt[:, 0].astype(jnp.int32), max_tiles)
        pos0, pos1 = pos[0], pos[1]
        xs = _dispatch(pad_lo, pad_hi, n_tiles, pos0, pos1, hp, max_tiles)
        ys = _experts(l, tile_expert, next_expert, n_tiles, xs, w_gate, w_up, w_down, max_tiles)
        gate_t = gate.T
        x_ctx = _tail(l, 0, T_CTX, pos0, pos1, x1, gate_t, mods, ln2_g, ln2_b, ys)
        x_lat = _tail(l, T_CTX, T_LAT, pos0, pos1, x1, gate_t, mods, ln2_g, ln2_b, ys)

    return (x_ctx.reshape(BATCH, SEQ, D_MODEL), x_lat.reshape(DEC_BATCH, DEC_SEQ, D_MODEL),
            new_k.reshape(BATCH, DEPTH, SEQ, NA_HEADS, NA_HEAD_DIM), new_v.reshape(BATCH, DEPTH, SEQ, NA_HEADS, NA_HEAD_DIM),
            new_ckv, new_kr)
```

```python
import functools

import numpy as np
import jax
import jax.numpy as jnp
from jax import lax
from jax.experimental import pallas as pl
from jax.experimental.pallas import tpu as pltpu

F32 = jnp.float32
BF16 = jnp.bfloat16

D_MODEL = 2048
BATCH = 32
SEQ = 256
DEPTH = 2
DEC_BATCH = 2
DEC_SEQ = 2048
PAST_LEN = 256
GRID_W = 64
GRID_R = DEC_SEQ // GRID_W
NA_HEADS = 6
NA_HEAD_DIM = 128
NA_ROWS = 8
NA_COLS = 16
MLA_HEADS = 6
Q_LORA = 512
KV_LORA = 512
NOPE_DIM = 128
ROPE_DIM = 64
V_DIM = 128
ROPE_BASE = 10000.0
MLA_SCALE = (NOPE_DIM + ROPE_DIM) ** -0.5
NA_SCALE = NA_HEAD_DIM ** -0.5
SGU_GROUPS = 4
SGU_DIM = 128
CHUNK = 128
NA_W = NA_HEADS * NA_HEAD_DIM
MLA_W = MLA_HEADS * V_DIM
SGU_W = SGU_GROUPS * SGU_DIM
D_MIX = NA_W + MLA_W + SGU_W
N_EXPERTS = 16
N_GROUPS = 4
EXPERTS_PER_GROUP = N_EXPERTS // N_GROUPS
TOP_K = 2
D_EXPERT = 512
ALPHA = (2 * DEPTH) ** 0.25
LN_EPS = 1e-6
RMS_EPS = 1e-6

T_CTX = BATCH * SEQ
T_LAT = DEC_BATCH * DEC_SEQ
T_ALL = T_CTX + T_LAT
N_COND = 8

C_Q, C_K, C_V = 0, NA_W, 2 * NA_W
C_CQ = 3 * NA_W
C_CKV = C_CQ + Q_LORA
C_KR = C_CKV + KV_LORA
MLA_QK = 256
LANES = 128
ROW_TILE = (D_MODEL // 2 // LANES, LANES)

TM_PROJ = 256
TM_OUT = 512
TM_SGU = 256
TM_TAIL = 256
TM_DISP = 512
TM_EXP = 256
TQ_MLA = 256
MLA_KCHUNK = 512
NA_QROWS = 4
NA_KROWS = NA_QROWS + NA_ROWS
NEG_BIG = -1e30
VMEM_LIMIT = 56 * 1024 * 1024


def _cparams(sem, vmem=VMEM_LIMIT):
    return pltpu.CompilerParams(dimension_semantics=sem, vmem_limit_bytes=vmem)


def _cond_row(i, tm):
    t0 = i * tm
    return jnp.where(t0 < T_CTX, 0, 1 + (t0 - T_CTX) // DEC_SEQ)


def _ln(x):
    mu = jnp.mean(x, axis=-1, keepdims=True)
    xc = x - mu
    var = jnp.mean(xc * xc, axis=-1, keepdims=True)
    return xc * lax.rsqrt(var + LN_EPS)


def _rms(x):
    return x * lax.rsqrt(jnp.mean(x * x, axis=-1, keepdims=True) + RMS_EPS)


def _dot(a, b):
    return jnp.dot(a, b, preferred_element_type=F32)


def _pack_rows(x):
    xf = x.astype(BF16).astype(F32)
    half = x.shape[-1] // 2
    w = pltpu.bitcast(xf[:, :half], jnp.uint32) | (pltpu.bitcast(xf[:, half:], jnp.uint32) >> 16)
    return w.reshape(x.shape[0], *ROW_TILE)


def _unpack_rows(w3):
    w = w3.reshape(w3.shape[0], D_MODEL // 2)
    return jnp.concatenate([pltpu.bitcast(w & jnp.uint32(0xFFFF0000), F32), pltpu.bitcast(w << 16, F32)], axis=-1)


def _dot_nt(a, b):
    return lax.dot_general(a, b, (((1,), (1,)), ((), ())), preferred_element_type=F32)


def _ada_kernel(c_ref, w_ref, b_ref, o_ref):
    c = c_ref[...]
    s = c * jax.nn.sigmoid(c)
    o_ref[...] = _dot(s.astype(BF16), w_ref[...].astype(BF16)) + b_ref[...]


def _ada(cond, w_ada, b_ada):
    tn = 1024
    n = 6 * D_MODEL
    return pl.pallas_call(
        _ada_kernel,
        grid=(DEPTH, n // tn),
        in_specs=[
            pl.BlockSpec((N_COND, D_MODEL), lambda l, j: (0, 0)),
            pl.BlockSpec((None, D_MODEL, tn), lambda l, j: (l, 0, j)),
            pl.BlockSpec((None, 1, tn), lambda l, j: (l, 0, j)),
        ],
        out_specs=pl.BlockSpec((None, N_COND, tn), lambda l, j: (l, 0, j)),
        out_shape=jax.ShapeDtypeStruct((DEPTH, N_COND, n), F32),
        compiler_params=_cparams(("parallel", "parallel")),
        name="ada_mod",
    )(cond, w_ada, b_ada.reshape(DEPTH, 1, n))


def _inproj_kernel(xc_ref, xl_ref, sh_ref, sc_ref, w_ref, wkr_ref, wz_ref, gq_ref, gkv_ref, wuq_ref, wukv_ref, cs_ref,
                   k32_in, v32_in, ckv32_in, kr32_in,
                   q_ref, k_ref, v_ref, k32_ref, v32_ref, qm_ref, km_ref, vv_ref,
                   ckv32_ref, kr32_ref, z_ref):
    del k32_in, v32_in, ckv32_in, kr32_in
    i = pl.program_id(0)
    is_ctx = i * TM_PROJ < T_CTX
    x = jnp.where(is_ctx, xc_ref[...], xl_ref[...])
    h = (_ln(x) * (1.0 + sc_ref[...]) + sh_ref[...]).astype(BF16)

    def proj(a, b):
        return _dot(h, w_ref[:, a:b])

    q_ref[...] = proj(C_Q, C_K).astype(BF16)
    k = proj(C_K, C_V)
    k_ref[...] = k.astype(BF16)
    v = proj(C_V, C_CQ)
    v_ref[...] = v.astype(BF16)

    cs = cs_ref[...]
    lane = lax.broadcasted_iota(jnp.int32, cs.shape, 1)

    def rotate(pair):
        t = pair * cs
        return jnp.where(lane < ROPE_DIM, t + pltpu.roll(t, ROPE_DIM, 1), 0.0)

    cqn = (_rms(proj(C_CQ, C_CKV)) * gq_ref[...]).astype(BF16)
    mq = _dot(cqn, wuq_ref[...])
    for hd in range(MLA_HEADS):
        c0 = hd * MLA_QK
        qm_ref[:, c0:c0 + NOPE_DIM] = mq[:, c0:c0 + NOPE_DIM].astype(BF16)
        qm_ref[:, c0 + NOPE_DIM:c0 + MLA_QK] = rotate(mq[:, c0 + NOPE_DIM:c0 + MLA_QK]).astype(BF16)

    ckvn = _rms(proj(C_CKV, C_KR)) * gkv_ref[...]
    kv = _dot(ckvn.astype(BF16), wukv_ref[...])
    kr2 = _dot(h, wkr_ref[...])
    krot = rotate(kr2).astype(BF16)
    for hd in range(MLA_HEADS):
        c0 = hd * MLA_QK
        km_ref[:, c0:c0 + NOPE_DIM] = kv[:, hd * NOPE_DIM:(hd + 1) * NOPE_DIM].astype(BF16)
        km_ref[:, c0 + NOPE_DIM:c0 + MLA_QK] = krot
    vv_ref[...] = kv[:, MLA_HEADS * NOPE_DIM:].astype(BF16)

    z_ref[...] = _dot(h, wz_ref[...])

    @pl.when(is_ctx)
    def _():
        k32_ref[...] = k
        v32_ref[...] = v
        ckv32_ref[...] = ckvn
        kr32_ref[...] = kr2[:, :ROPE_DIM]


def _inproj(l, x_ctx, x_lat, mods, w_main, w_kr, w_z, g_q, g_kv, w_uq_p, w_ukv_p, cs_tab, new_k, new_v, new_ckv, new_kr):
    tm = TM_PROJ
    assert tm == SEQ
    nct = T_CTX // tm
    per_b = DEC_SEQ // tm

    def row(i):
        return (i, 0)

    def ctx_row(i):
        return (jnp.minimum(i, nct - 1), 0)

    def lat_row(i):
        return (jnp.maximum(i - nct, 0), 0)

    def cs_row(i):
        return (jnp.where(i < nct, 0, 1 + (i - nct) % per_b), 0)

    def mod(kind):
        return pl.BlockSpec((None, None, None, 1, D_MODEL), lambda i: (l, _cond_row(i, tm), kind, 0, 0))

    def const2(shape):
        return pl.BlockSpec(shape, lambda i: (l, 0, 0), pipeline_mode=pl.Buffered(1))

    def cache(w):
        return pl.BlockSpec((None, None, SEQ, w), lambda i: (jnp.minimum(i, nct - 1), l, 0, 0))

    anyspec = pl.BlockSpec(memory_space=pl.ANY)
    out_shapes = [
        jax.ShapeDtypeStruct((T_ALL, NA_W), BF16),
        jax.ShapeDtypeStruct((T_ALL, NA_W), BF16),
        jax.ShapeDtypeStruct((T_ALL, NA_W), BF16),
        jax.ShapeDtypeStruct(new_k.shape, F32),
        jax.ShapeDtypeStruct(new_v.shape, F32),
        jax.ShapeDtypeStruct((T_ALL, MLA_HEADS * MLA_QK), BF16),
        jax.ShapeDtypeStruct((T_ALL, MLA_HEADS * MLA_QK), BF16),
        jax.ShapeDtypeStruct((T_ALL, MLA_W), BF16),
        jax.ShapeDtypeStruct(new_ckv.shape, F32),
        jax.ShapeDtypeStruct(new_kr.shape, F32),
        jax.ShapeDtypeStruct((T_ALL, 2 * SGU_W), F32),
    ]
    out_specs = [
        pl.BlockSpec((tm, NA_W), row), pl.BlockSpec((tm, NA_W), row), pl.BlockSpec((tm, NA_W), row),
        cache(NA_W), cache(NA_W),
        pl.BlockSpec((tm, MLA_HEADS * MLA_QK), row), pl.BlockSpec((tm, MLA_HEADS * MLA_QK), row),
        pl.BlockSpec((tm, MLA_W), row),
        cache(KV_LORA), cache(ROPE_DIM),
        pl.BlockSpec((tm, 2 * SGU_W), row),
    ]
    return pl.pallas_call(
        _inproj_kernel,
        grid=(T_ALL // tm,),
        in_specs=[
            pl.BlockSpec((tm, D_MODEL), ctx_row), pl.BlockSpec((tm, D_MODEL), lat_row),
            mod(0), mod(1),
            const2((None, D_MODEL, C_KR)), const2((None, D_MODEL, 2 * ROPE_DIM)), const2((None, D_MODEL, 2 * SGU_W)),
            pl.BlockSpec((None, 1, Q_LORA), lambda i: (l, 0, 0)),
            pl.BlockSpec((None, 1, KV_LORA), lambda i: (l, 0, 0)),
            const2((None, Q_LORA, MLA_HEADS * MLA_QK)),
            const2((None, KV_LORA, 2 * MLA_W)),
            pl.BlockSpec((tm, 2 * ROPE_DIM), cs_row),
            anyspec, anyspec, anyspec, anyspec,
        ],
        out_specs=out_specs,
        out_shape=out_shapes,
        input_output_aliases={12: 3, 13: 4, 14: 8, 15: 9},
        compiler_params=_cparams(("arbitrary",)),
        name="in_proj",
    )(x_ctx, x_lat, mods, mods, w_main, w_kr, w_z, g_q.reshape(DEPTH, 1, Q_LORA), g_kv.reshape(DEPTH, 1, KV_LORA),
      w_uq_p, w_ukv_p, cs_tab, new_k, new_v, new_ckv, new_kr)


def _ctxkv_kernel(ckv_ref, kr_ref, w_ref, km_ref, vv_ref):
    kv = _dot(ckv_ref[...].astype(BF16), w_ref[...])
    kr = jnp.concatenate([kr_ref[...], jnp.zeros((PAST_LEN, ROPE_DIM), F32)], axis=-1).astype(BF16)
    for hd in range(MLA_HEADS):
        c0 = hd * MLA_QK
        km_ref[:, c0:c0 + NOPE_DIM] = kv[:, hd * NOPE_DIM:(hd + 1) * NOPE_DIM].astype(BF16)
        km_ref[:, c0 + NOPE_DIM:c0 + MLA_QK] = kr
    vv_ref[...] = kv[:, MLA_HEADS * NOPE_DIM:].astype(BF16)


def _ctxkv(cache_ckv, cache_krope, w_ukv_p):
    return pl.pallas_call(
        _ctxkv_kernel,
        grid=(DEPTH, DEC_BATCH),
        in_specs=[
            pl.BlockSpec((None, None, PAST_LEN, KV_LORA), lambda l, b: (b, l, 0, 0)),
            pl.BlockSpec((None, None, PAST_LEN, ROPE_DIM), lambda l, b: (b, l, 0, 0)),
            pl.BlockSpec((None, KV_LORA, 2 * MLA_W), lambda l, b: (l, 0, 0)),
        ],
        out_specs=[
            pl.BlockSpec((None, None, PAST_LEN, MLA_HEADS * MLA_QK), lambda l, b: (l, b, 0, 0)),
            pl.BlockSpec((None, None, PAST_LEN, MLA_W), lambda l, b: (l, b, 0, 0)),
        ],
        out_shape=[
            jax.ShapeDtypeStruct((DEPTH, DEC_BATCH, PAST_LEN, MLA_HEADS * MLA_QK), BF16),
            jax.ShapeDtypeStruct((DEPTH, DEC_BATCH, PAST_LEN, MLA_W), BF16),
        ],
        compiler_params=_cparams(("parallel", "parallel")),
        name="ctx_cache_kv",
    )(cache_ckv, cache_krope, w_ukv_p)


def _softmax_pv(blocks):
    ss = [s for s, _ in blocks]
    m = ss[0].max(axis=-1, keepdims=True)
    for s in ss[1:]:
        m = jnp.maximum(m, s.max(axis=-1, keepdims=True))
    num = None
    den = None
    for s, (_, v) in zip(ss, blocks):
        e = jnp.exp(s - m)
        d = e.sum(axis=-1, keepdims=True)
        o = _dot(e.astype(BF16), v)
        num = o if num is None else num + o
        den = d if den is None else den + d
    return num / den


def _ctx_attn_kernel(q_ref, k_ref, v_ref, qm_ref, km_ref, vv_ref, oa_ref, ob_ref):
    for hd in range(NA_HEADS):
        sl = slice(hd * NA_HEAD_DIM, (hd + 1) * NA_HEAD_DIM)
        s = _dot_nt(q_ref[:, sl], k_ref[:, sl]) * NA_SCALE
        oa_ref[:, sl] = _softmax_pv([(s, v_ref[:, sl])]).astype(BF16)
    for hd in range(MLA_HEADS):
        sq = slice(hd * MLA_QK, (hd + 1) * MLA_QK)
        sv = slice(hd * V_DIM, (hd + 1) * V_DIM)
        s = _dot_nt(qm_ref[:, sq], km_ref[:, sq]) * MLA_SCALE
        ob_ref[:, sv] = _softmax_pv([(s, vv_ref[:, sv])]).astype(BF16)


def _ctx_attn(q, k, v, qm, km, vv):
    def spec(w):
        return pl.BlockSpec((SEQ, w), lambda b: (b, 0))

    return pl.pallas_call(
        _ctx_attn_kernel,
        grid=(BATCH,),
        in_specs=[spec(NA_W), spec(NA_W), spec(NA_W),
                  spec(MLA_HEADS * MLA_QK), spec(MLA_HEADS * MLA_QK), spec(MLA_W)],
        out_specs=[spec(NA_W), spec(MLA_W)],
        out_shape=[jax.ShapeDtypeStruct((T_CTX, NA_W), BF16), jax.ShapeDtypeStruct((T_CTX, MLA_W), BF16)],
        compiler_params=_cparams(("parallel",)),
        name="ctx_attn",
    )(q, k, v, qm, km, vv)


def _lat_mla_kernel(q_ref, k_ref, v_ref, kc_ref, vc_ref, o_ref):
    q = q_ref[...]
    blocks = []
    for c in range(DEC_SEQ // MLA_KCHUNK):
        ks = slice(c * MLA_KCHUNK, (c + 1) * MLA_KCHUNK)
        blocks.append((_dot_nt(q, k_ref[ks, :]) * MLA_SCALE, v_ref[ks, :]))
    blocks.append((_dot_nt(q, kc_ref[...]) * MLA_SCALE, vc_ref[...]))
    o_ref[...] = _softmax_pv(blocks).astype(BF16)


def _lat_mla(l, qm, km, vv, kmc, vvc):
    tq = TQ_MLA
    nq = DEC_SEQ // tq
    row0 = T_CTX // tq
    kb0 = T_CTX // DEC_SEQ
    return pl.pallas_call(
        _lat_mla_kernel,
        grid=(DEC_BATCH, MLA_HEADS, nq),
        in_specs=[
            pl.BlockSpec((tq, MLA_QK), lambda b, h, t: (row0 + b * nq + t, h)),
            pl.BlockSpec((DEC_SEQ, MLA_QK), lambda b, h, t: (kb0 + b, h)),
            pl.BlockSpec((DEC_SEQ, V_DIM), lambda b, h, t: (kb0 + b, h)),
            pl.BlockSpec((None, None, PAST_LEN, MLA_QK), lambda b, h, t: (l, b, 0, h)),
            pl.BlockSpec((None, None, PAST_LEN, V_DIM), lambda b, h, t: (l, b, 0, h)),
        ],
        out_specs=pl.BlockSpec((tq, V_DIM), lambda b, h, t: (b * nq + t, h)),
        out_shape=jax.ShapeDtypeStruct((T_LAT, MLA_W), BF16),
        compiler_params=_cparams(("parallel", "parallel", "arbitrary")),
        name="lat_mla",
    )(qm, km, vv, kmc, vvc)


def _na_block_plan():
    plan = []
    for r0 in range(0, GRID_R, NA_QROWS):
        ws = min(max(r0 - NA_ROWS // 2, 0), GRID_R - NA_KROWS)
        rows = []
        for r in range(r0, r0 + NA_QROWS):
            rs = min(max(r - NA_ROWS // 2, 0), GRID_R - NA_ROWS)
            rel = [kr - r + NA_ROWS - 1 if rs <= kr < rs + NA_ROWS else None for kr in range(ws, ws + NA_KROWS)]
            rows.append([(rel[j], rel[j + 1]) for j in range(0, NA_KROWS, 2)])
        plan.append((ws, rows))
    return plan


def _lat_na_kernel(q_ref, k_ref, v_ref, ck_ref, cv_ref, tab_ref, o_ref):
    ck = ck_ref[...].astype(BF16)
    cv = cv_ref[...].astype(BF16)
    nq = NA_QROWS * GRID_W
    nk = NA_KROWS * GRID_W
    lane = lax.broadcasted_iota(jnp.int32, (GRID_W, 2 * GRID_W), 1)
    masked = jnp.full((GRID_W, 2 * GRID_W), NEG_BIG, F32)

    def pair_bias(d0, d1):
        if d0 is None and d1 is None:
            return masked
        if d1 is None:
            return jnp.where(lane < GRID_W, tab_ref[d0], NEG_BIG)
        if d0 is None:
            return jnp.where(lane >= GRID_W, tab_ref[d1 - 1], NEG_BIG)
        return tab_ref[d0]

    for blk, (ws, rows) in enumerate(_na_block_plan()):
        q = q_ref[blk * nq:(blk + 1) * nq, :]
        kw = k_ref[ws * GRID_W:ws * GRID_W + nk, :]
        vw = v_ref[ws * GRID_W:ws * GRID_W + nk, :]
        bias = jnp.concatenate(
            [jnp.concatenate([pair_bias(d0, d1) for d0, d1 in pairs], axis=1) for pairs in rows], axis=0)
        s1 = _dot_nt(q, kw) * NA_SCALE + bias
        s2 = _dot_nt(q, ck) * NA_SCALE
        o_ref[blk * nq:(blk + 1) * nq, :] = _softmax_pv([(s1, vw), (s2, cv)]).astype(BF16)


def _lat_na(l, q, k, v, cache_k, cache_v, bias_tab):
    kb0 = T_CTX // DEC_SEQ

    def lat(b, h):
        return (kb0 + b, h)

    return pl.pallas_call(
        _lat_na_kernel,
        grid=(DEC_BATCH, NA_HEADS),
        in_specs=[
            pl.BlockSpec((DEC_SEQ, NA_HEAD_DIM), lat),
            pl.BlockSpec((DEC_SEQ, NA_HEAD_DIM), lat),
            pl.BlockSpec((DEC_SEQ, NA_HEAD_DIM), lat),
            pl.BlockSpec((None, None, PAST_LEN, NA_HEAD_DIM), lambda b, h: (b, l, 0, h)),
            pl.BlockSpec((None, None, PAST_LEN, NA_HEAD_DIM), lambda b, h: (b, l, 0, h)),
            pl.BlockSpec((None, None, 2 * NA_ROWS - 1, GRID_W, 2 * GRID_W), lambda b, h: (l, h, 0, 0, 0)),
        ],
        out_specs=pl.BlockSpec((DEC_SEQ, NA_HEAD_DIM), lambda b, h: (b, h)),
        out_shape=jax.ShapeDtypeStruct((T_LAT, NA_W), BF16),
        compiler_params=_cparams(("parallel", "parallel")),
        name="lat_na",
    )(q, k, v, cache_k, cache_v, bias_tab)


def _sgu_kernel(z_ref, g_ref, b_ref, w_ref, bs_ref, o_ref):
    z = z_ref[...]
    a = 0.5 * z * (1.0 + lax.erf(z * (0.5 ** 0.5)))
    for c in range(TM_SGU // CHUNK):
        rows = slice(c * CHUNK, (c + 1) * CHUNK)
        for g in range(SGU_GROUPS):
            u = a[rows, g * SGU_DIM:(g + 1) * SGU_DIM]
            v = a[rows, SGU_W + g * SGU_DIM:SGU_W + (g + 1) * SGU_DIM]
            vn = _ln(v) * g_ref[g:g + 1, :] + b_ref[g:g + 1, :]
            t = _dot(w_ref[g].astype(BF16), vn.astype(BF16)) + bs_ref[g]
            o_ref[rows, g * SGU_DIM:(g + 1) * SGU_DIM] = (u * t).astype(BF16)


def _sgu(l, z, ln_g, ln_b, w_s, b_s):
    tm = TM_SGU
    return pl.pallas_call(
        _sgu_kernel,
        grid=(T_ALL // tm,),
        in_specs=[
            pl.BlockSpec((tm, 2 * SGU_W), lambda i: (i, 0)),
            pl.BlockSpec((None, SGU_GROUPS, SGU_DIM), lambda i: (l, 0, 0)),
            pl.BlockSpec((None, SGU_GROUPS, SGU_DIM), lambda i: (l, 0, 0)),
            pl.BlockSpec((None, SGU_GROUPS, CHUNK, CHUNK), lambda i: (l, 0, 0, 0)),
            pl.BlockSpec((None, SGU_GROUPS, CHUNK, 1), lambda i: (l, 0, 0, 0)),
        ],
        out_specs=pl.BlockSpec((tm, SGU_W), lambda i: (i, 0)),
        out_shape=jax.ShapeDtypeStruct((T_ALL, SGU_W), BF16),
        compiler_params=_cparams(("parallel",)),
        name="sgu",
    )(z, ln_g, ln_b, w_s, b_s.reshape(DEPTH, SGU_GROUPS, CHUNK, 1))


def _route_rows(logits_t, bias_col):
    s = jax.nn.sigmoid(logits_t)
    sb = s + bias_col
    rows = [sb[r:r + 1, :] for r in range(N_EXPERTS)]
    srows = [s[r:r + 1, :] for r in range(N_EXPERTS)]
    best, gsel = None, None
    for g in range(N_GROUPS):
        m = rows[g * EXPERTS_PER_GROUP:(g + 1) * EXPERTS_PER_GROUP]
        score = None
        for a in range(EXPERTS_PER_GROUP):
            for b in range(a + 1, EXPERTS_PER_GROUP):
                pair = m[a] + m[b]
                score = pair if score is None else jnp.maximum(score, pair)
        if best is None:
            best, gsel = score, jnp.zeros(score.shape, jnp.int32)
        else:
            upd = score > best
            best = jnp.where(upd, score, best)
            gsel = jnp.where(upd, g, gsel)

    def pick(table, j):
        out = table[j]
        for g in range(1, N_GROUPS):
            out = jnp.where(gsel == g, table[g * EXPERTS_PER_GROUP + j], out)
        return out

    v = [pick(rows, j) for j in range(EXPERTS_PER_GROUP)]
    sv = [pick(srows, j) for j in range(EXPERTS_PER_GROUP)]
    m1, i1, w1 = v[0], jnp.zeros(gsel.shape, jnp.int32), sv[0]
    for j in range(1, EXPERTS_PER_GROUP):
        upd = v[j] > m1
        m1 = jnp.where(upd, v[j], m1)
        i1 = jnp.where(upd, j, i1)
        w1 = jnp.where(upd, sv[j], w1)
    m2 = jnp.full(gsel.shape, -jnp.inf, F32)
    i2 = jnp.zeros(gsel.shape, jnp.int32)
    w2 = jnp.zeros(gsel.shape, F32)
    for j in range(EXPERTS_PER_GROUP):
        upd = jnp.logical_and(i1 != j, v[j] > m2)
        m2 = jnp.where(upd, v[j], m2)
        i2 = jnp.where(upd, j, i2)
        w2 = jnp.where(upd, sv[j], w2)
    den = w1 + w2
    return gsel * EXPERTS_PER_GROUP + i1, gsel * EXPERTS_PER_GROUP + i2, w1 / den, w2 / den


def _outproj_kernel(ac_ref, al_ref, bc_ref, bl_ref, c_ref, wa_ref, wb_ref, wc_ref, xc_ref, xl_ref, g1_ref, sh2_ref, sc2_ref,
                    lg_ref, lb_ref, wrt_ref, br_ref, x1_ref, hp_ref, e_ref, rank_ref, gate_ref, cnt_ref, base_ref):
    is_ctx = pl.program_id(0) * TM_OUT < T_CTX

    @pl.when(pl.program_id(0) == 0)
    def _():
        base_ref[...] = jnp.zeros_like(base_ref)

    tm = TM_OUT
    a = jnp.where(is_ctx, ac_ref[...], al_ref[...])
    b = jnp.where(is_ctx, bc_ref[...], bl_ref[...])
    x = jnp.where(is_ctx, xc_ref[...], xl_ref[...])
    mixed = _dot(a, wa_ref[...]) + _dot(b, wb_ref[...]) + _dot(c_ref[...], wc_ref[...])
    x1 = _ln(ALPHA * x + g1_ref[...] * mixed) * lg_ref[...] + lb_ref[...]
    x1_ref[...] = x1
    h2 = (_ln(x1) * (1.0 + sc2_ref[...]) + sh2_ref[...]).astype(BF16)
    hp_ref[...] = _pack_rows(h2)

    e1, e2, w1, w2 = _route_rows(_dot_nt(wrt_ref[...], h2), br_ref[...])
    e_ref[0:1, :] = e1
    e_ref[1:2, :] = e2
    gate_ref[0:1, :] = w1
    gate_ref[1:2, :] = w2

    sub = lax.broadcasted_iota(jnp.int32, (N_EXPERTS, tm), 0)
    before = (lax.broadcasted_iota(jnp.int32, (tm, tm), 0) < lax.broadcasted_iota(jnp.int32, (tm, tm), 1)).astype(BF16)
    o1 = (sub == e1).astype(F32)
    o2 = (sub == e2).astype(F32)
    p1 = _dot(o1.astype(BF16), before)
    p2 = _dot(o2.astype(BF16), before)
    c1 = o1.sum(axis=1, keepdims=True)
    c2 = o2.sum(axis=1, keepdims=True)
    base = base_ref[:, 0:1]
    rank_ref[0:1, :] = jnp.sum(o1 * (base + p1), axis=0, keepdims=True).astype(jnp.int32)
    rank_ref[1:2, :] = jnp.sum(o2 * (base + c1 + p2), axis=0, keepdims=True).astype(jnp.int32)
    base_ref[...] = base_ref[...] + (c1 + c2)
    cnt_ref[...] = base_ref[...]


def _outproj(l, oa_c, oa_l, ob_c, ob_l, oc, w_out_b, x_ctx, x_lat, mods, ln_g, ln_b, w_router_t, b_router_col):
    tm = TM_OUT
    nct = T_CTX // tm

    def row(i):
        return (i, 0)

    def ctx_row(i):
        return (jnp.minimum(i, nct - 1), 0)

    def lat_row(i):
        return (jnp.maximum(i - nct, 0), 0)

    def tok(i):
        return (0, i)

    def mod(kind):
        return pl.BlockSpec((None, None, None, 1, D_MODEL), lambda i: (l, _cond_row(i, tm), kind, 0, 0))

    def wblk(rows, blk):
        return pl.BlockSpec((None, rows, D_MODEL), lambda i: (l, blk, 0), pipeline_mode=pl.Buffered(1))

    vec = pl.BlockSpec((None, 1, D_MODEL), lambda i: (l, 0, 0))
    return pl.pallas_call(
        _outproj_kernel,
        grid=(T_ALL // tm,),
        in_specs=[
            pl.BlockSpec((tm, NA_W), ctx_row), pl.BlockSpec((tm, NA_W), lat_row),
            pl.BlockSpec((tm, MLA_W), ctx_row), pl.BlockSpec((tm, MLA_W), lat_row),
            pl.BlockSpec((tm, SGU_W), row),
            wblk(NA_W, 0), wblk(MLA_W, 1), wblk(SGU_W, (NA_W + MLA_W) // SGU_W),
            pl.BlockSpec((tm, D_MODEL), ctx_row), pl.BlockSpec((tm, D_MODEL), lat_row),
            mod(2), mod(3), mod(4), vec, vec,
            pl.BlockSpec((N_EXPERTS, D_MODEL), lambda i: (0, 0)),
            pl.BlockSpec((N_EXPERTS, 1), lambda i: (0, 0)),
        ],
        out_specs=[pl.BlockSpec((tm, D_MODEL), row), pl.BlockSpec((tm, *ROW_TILE), lambda i: (i, 0, 0)),
                   pl.BlockSpec((TOP_K, tm), tok), pl.BlockSpec((TOP_K, tm), tok), pl.BlockSpec((TOP_K, tm), tok),
                   pl.BlockSpec((N_EXPERTS, LANES), lambda i: (0, 0))],
        out_shape=[jax.ShapeDtypeStruct((T_ALL, D_MODEL), F32),
                   jax.ShapeDtypeStruct((T_ALL, *ROW_TILE), jnp.uint32),
                   jax.ShapeDtypeStruct((TOP_K, T_ALL), jnp.int32),
                   jax.ShapeDtypeStruct((TOP_K, T_ALL), jnp.int32),
                   jax.ShapeDtypeStruct((TOP_K, T_ALL), F32),
                   jax.ShapeDtypeStruct((N_EXPERTS, LANES), F32)],
        scratch_shapes=[pltpu.VMEM((N_EXPERTS, LANES), F32)],
        compiler_params=_cparams(("arbitrary",)),
        name="out_proj",
    )(oa_c, oa_l, ob_c, ob_l, oc, w_out_b, w_out_b, w_out_b, x_ctx, x_lat, mods, mods, mods,
      ln_g.reshape(DEPTH, 1, D_MODEL), ln_b.reshape(DEPTH, 1, D_MODEL), w_router_t, b_router_col)


def _dispatch_plan(e, rank, counts, max_tiles):
    tiles = (counts + TM_EXP - 1) // TM_EXP
    tile_end = jnp.cumsum(tiles)
    off = (tile_end - tiles) * TM_EXP
    n_tiles = tile_end[-1]
    ids = jnp.arange(N_EXPERTS, dtype=jnp.int32)
    pos = rank + jnp.sum(jnp.where(e[..., None] == ids, off, 0), axis=-1)
    ti = jnp.arange(max_tiles, dtype=jnp.int32)
    owner = jnp.sum((tile_end[None, :] <= ti[:, None]).astype(jnp.int32), axis=1)
    last = jnp.sum((tile_end <= n_tiles - 1).astype(jnp.int32))
    tile_expert = jnp.where(ti < n_tiles, owner, last).astype(jnp.int32)
    later_used = jnp.logical_and(ids[None, :] > ids[:, None], tiles[None, :] > 0)
    next_expert = jnp.min(jnp.where(later_used, ids[None, :], N_EXPERTS), axis=1)
    next_expert = jnp.where(next_expert < N_EXPERTS, next_expert, -1).astype(jnp.int32)
    pad_lo = (off + counts).astype(jnp.int32)
    pad_hi = (off + tiles * TM_EXP).astype(jnp.int32)
    return pos.astype(jnp.int32), tile_expert, next_expert, n_tiles.reshape(1).astype(jnp.int32), pad_lo, pad_hi


def _row_copy(src_ref, src_row, dst_ref, dst_row, sem):
    return pltpu.make_async_copy(src_ref.at[pl.ds(src_row, 1)], dst_ref.at[pl.ds(dst_row, 1)], sem)


def _dispatch_kernel(lo_ref, hi_ref, nt_ref, p0_ref, p1_ref, hp_ref, xs_ref, zeros, sem, *, max_tiles):
    @pl.when(pl.program_id(0) == 0)
    def _():
        zeros[...] = jnp.zeros_like(zeros)

        def fill(r, carry):
            _row_copy(zeros, 0, xs_ref, r, sem).start()
            return carry

        def fill_done(r, carry):
            _row_copy(zeros, 0, xs_ref, 0, sem).wait()
            return carry

        for ex in range(N_EXPERTS):
            lax.fori_loop(lo_ref[ex], hi_ref[ex], fill, 0)
        for ex in range(N_EXPERTS):
            lax.fori_loop(lo_ref[ex], hi_ref[ex], fill_done, 0)

        def tile_copy(t):
            return pltpu.make_async_copy(zeros, xs_ref.at[pl.ds(t * TM_EXP, TM_EXP)], sem)

        def fill_tile(t, carry):
            tile_copy(t).start()
            return carry

        def fill_tile_done(t, carry):
            tile_copy(0).wait()
            return carry

        lax.fori_loop(nt_ref[0], max_tiles, fill_tile, 0)
        lax.fori_loop(nt_ref[0], max_tiles, fill_tile_done, 0)

    def issue(t, carry):
        for k, p_ref in enumerate((p0_ref, p1_ref)):
            _row_copy(hp_ref, t, xs_ref, p_ref[t], sem).start(priority=k)
        return carry

    lax.fori_loop(0, TM_DISP, issue, 0, unroll=8)

    def drain(t, carry):
        for k in range(TOP_K):
            _row_copy(hp_ref, 0, xs_ref, 0, sem).wait()
        return carry

    lax.fori_loop(0, TM_DISP, drain, 0, unroll=8)


def _dispatch(pad_lo, pad_hi, n_tiles, pos0, pos1, hp, max_tiles):
    grid_spec = pltpu.PrefetchScalarGridSpec(
        num_scalar_prefetch=3,
        grid=(T_ALL // TM_DISP,),
        in_specs=[
            pl.BlockSpec((TM_DISP,), lambda i, lo, hi, nt: (i,), memory_space=pltpu.SMEM),
            pl.BlockSpec((TM_DISP,), lambda i, lo, hi, nt: (i,), memory_space=pltpu.SMEM),
            pl.BlockSpec((TM_DISP, *ROW_TILE), lambda i, lo, hi, nt: (i, 0, 0)),
        ],
        out_specs=pl.BlockSpec(memory_space=pl.ANY),
        scratch_shapes=[pltpu.VMEM((TM_EXP, *ROW_TILE), jnp.uint32), pltpu.SemaphoreType.DMA(())],
    )
    return pl.pallas_call(
        functools.partial(_dispatch_kernel, max_tiles=max_tiles),
        grid_spec=grid_spec,
        out_shape=jax.ShapeDtypeStruct((max_tiles * TM_EXP, *ROW_TILE), jnp.uint32),
        compiler_params=_cparams(("arbitrary",)),
        name="dispatch",
    )(pad_lo, pad_hi, n_tiles, pos0, pos1, hp)


def _expert_kernel(te_ref, nxt_ref, nt_ref, x_ref, wg_hbm, wu_hbm, wd_hbm, y_ref,
                   wg_f, wu_f, wd_f, wg_s, wu_s, wd_s, switches, sem, *, layer):
    i = pl.program_id(0)
    e = te_ref[i]

    def weight_copies(expert, slot):
        return [pltpu.make_async_copy(w_hbm.at[layer, expert], buf.at[slot], sem.at[slot, j])
                for j, (w_hbm, buf) in enumerate(((wg_hbm, wg_f), (wu_hbm, wu_f), (wd_hbm, wd_f)))]

    @pl.when(i == 0)
    def _():
        for c in weight_copies(e, 0):
            c.start()

    fresh = jnp.logical_or(i == 0, e != te_ref[jnp.maximum(i - 1, 0)])

    @pl.when(fresh)
    def _():
        n_sw = jnp.where(i == 0, 0, switches[0] + 1)
        switches[0] = n_sw
        slot = n_sw % 2
        for c in weight_copies(e, slot):
            c.wait()
        wg_s[...] = wg_f[slot].astype(BF16)
        wu_s[...] = wu_f[slot].astype(BF16)
        wd_s[...] = wd_f[slot].astype(BF16)
        nx = nxt_ref[e]

        @pl.when(nx >= 0)
        def _():
            for c in weight_copies(nx, 1 - slot):
                c.start()

    @pl.when(i < nt_ref[0])
    def _():
        x = _unpack_rows(x_ref[...]).astype(BF16)
        g = _dot(x, wg_s[...])
        u = _dot(x, wu_s[...])
        a = (g * jax.nn.sigmoid(g) * u).astype(BF16)
        y_ref[...] = _pack_rows(_dot(a, wd_s[...]))

    @pl.when(i >= nt_ref[0])
    def _():
        y_ref[...] = jnp.zeros_like(y_ref)


def _experts(l, tile_expert, next_expert, n_tiles, xs, w_gate, w_up, w_down, max_tiles):
    tm = TM_EXP
    anyspec = pl.BlockSpec(memory_space=pl.ANY)
    grid_spec = pltpu.PrefetchScalarGridSpec(
        num_scalar_prefetch=3,
        grid=(max_tiles,),
        in_specs=[pl.BlockSpec((tm, *ROW_TILE), lambda i, te, nx, nt: (i, 0, 0)), anyspec, anyspec, anyspec],
        out_specs=pl.BlockSpec((tm, *ROW_TILE), lambda i, te, nx, nt: (i, 0, 0)),
        scratch_shapes=[pltpu.VMEM((2, D_MODEL, D_EXPERT), F32), pltpu.VMEM((2, D_MODEL, D_EXPERT), F32),
                        pltpu.VMEM((2, D_EXPERT, D_MODEL), F32),
                        pltpu.VMEM((D_MODEL, D_EXPERT), BF16), pltpu.VMEM((D_MODEL, D_EXPERT), BF16),
                        pltpu.VMEM((D_EXPERT, D_MODEL), BF16),
                        pltpu.SMEM((1,), jnp.int32), pltpu.SemaphoreType.DMA((2, 3))],
    )
    return pl.pallas_call(
        functools.partial(_expert_kernel, layer=l),
        grid_spec=grid_spec,
        out_shape=jax.ShapeDtypeStruct((max_tiles * tm, *ROW_TILE), jnp.uint32),
        compiler_params=_cparams(("arbitrary",)),
        name="experts",
    )(tile_expert, next_expert, n_tiles, xs, w_gate, w_up, w_down)


def _tail_kernel(p0_ref, p1_ref, p0n_ref, p1n_ref, x1_ref, gate_ref, g2_ref, lg_ref, lb_ref, ys_ref, o_ref,
                 ybuf, sem, *, n_tiles):
    tm = TM_TAIL
    i = pl.program_id(0)
    slot = i % 2

    def issue(p_refs, s):
        def body(t, carry):
            for k, p_ref in enumerate(p_refs):
                _row_copy(ys_ref, p_ref[t], ybuf.at[s, k], t, sem.at[s]).start()
            return carry

        lax.fori_loop(0, tm, body, 0, unroll=8)

    @pl.when(i == 0)
    def _():
        issue((p0_ref, p1_ref), 0)

    @pl.when(i + 1 < n_tiles)
    def _():
        issue((p0n_ref, p1n_ref), 1 - slot)

    def drain(t, carry):
        for k in range(TOP_K):
            _row_copy(ys_ref, 0, ybuf.at[slot, k], 0, sem.at[slot]).wait()
        return carry

    lax.fori_loop(0, tm, drain, 0, unroll=8)

    gate = gate_ref[...]
    y = gate[:, 0:1] * _unpack_rows(ybuf[slot, 0]) + gate[:, 1:2] * _unpack_rows(ybuf[slot, 1])
    o_ref[...] = _ln(ALPHA * x1_ref[...] + g2_ref[...] * y) * lg_ref[...] + lb_ref[...]


def _tail(l, t_start, t_count, pos0, pos1, x1, gate_t, mods, ln_g, ln_b, ys):
    tm = TM_TAIL
    n = t_count // tm
    i0 = t_start // tm
    vec = pl.BlockSpec((None, 1, D_MODEL), lambda i: (l, 0, 0))
    return pl.pallas_call(
        functools.partial(_tail_kernel, n_tiles=n),
        grid=(n,),
        in_specs=[
            pl.BlockSpec((tm,), lambda i: (i0 + i,), memory_space=pltpu.SMEM),
            pl.BlockSpec((tm,), lambda i: (i0 + i,), memory_space=pltpu.SMEM),
            pl.BlockSpec((tm,), lambda i: (i0 + jnp.minimum(i + 1, n - 1),), memory_space=pltpu.SMEM),
            pl.BlockSpec((tm,), lambda i: (i0 + jnp.minimum(i + 1, n - 1),), memory_space=pltpu.SMEM),
            pl.BlockSpec((tm, D_MODEL), lambda i: (i0 + i, 0)),
            pl.BlockSpec((tm, TOP_K), lambda i: (i0 + i, 0)),
            pl.BlockSpec((None, None, None, 1, D_MODEL), lambda i: (l, _cond_row(i0 + i, tm), 5, 0, 0)),
            vec, vec,
            pl.BlockSpec(memory_space=pl.ANY),
        ],
        out_specs=pl.BlockSpec((tm, D_MODEL), lambda i: (i, 0)),
        out_shape=jax.ShapeDtypeStruct((t_count, D_MODEL), F32),
        scratch_shapes=[pltpu.VMEM((2, TOP_K, tm, *ROW_TILE), jnp.uint32), pltpu.SemaphoreType.DMA((2,))],
        compiler_params=_cparams(("arbitrary",)),
        name="tail",
    )(pos0, pos1, pos0, pos1, x1, gate_t, mods, ln_g.reshape(DEPTH, 1, D_MODEL), ln_b.reshape(DEPTH, 1, D_MODEL), ys)


def _swap_partners(w):
    nf = ROPE_DIM // 4
    return jnp.concatenate([w[..., nf:2 * nf], w[..., :nf], w[..., 3 * nf:], w[..., 2 * nf:3 * nf]], axis=-1)


def _prep_w_in(w_in):
    kr = w_in[..., C_KR:C_KR + ROPE_DIM]
    w_kr = jnp.concatenate([kr, _swap_partners(kr)], axis=-1).astype(BF16)
    return w_in.astype(BF16), w_kr, w_in[..., C_KR + ROPE_DIM:].astype(BF16)


def _prep_w_uq(w_uq):
    w = w_uq.reshape(DEPTH, Q_LORA, MLA_HEADS, NOPE_DIM + ROPE_DIM)
    return jnp.concatenate([w, _swap_partners(w[..., NOPE_DIM:])], axis=-1).reshape(
        DEPTH, Q_LORA, MLA_HEADS * MLA_QK).astype(BF16)


def _prep_w_ukv(w_ukv):
    w = w_ukv.reshape(DEPTH, KV_LORA, MLA_HEADS, NOPE_DIM + V_DIM)
    return jnp.concatenate([w[..., :NOPE_DIM].reshape(DEPTH, KV_LORA, MLA_W),
                            w[..., NOPE_DIM:].reshape(DEPTH, KV_LORA, MLA_W)], axis=-1).astype(BF16)


def _rope_table(tm):
    half = ROPE_DIM // 2
    nf = half // 2
    t = jnp.arange(DEC_SEQ)
    inv = ROPE_BASE ** (-jnp.arange(nf, dtype=F32) * 2.0 / half)
    ar = (t // GRID_W).astype(F32)[:, None] * inv[None, :]
    ac = (t % GRID_W).astype(F32)[:, None] * inv[None, :]
    cos = jnp.concatenate([jnp.cos(ar), jnp.cos(ar), jnp.cos(ac), jnp.cos(ac)], axis=-1)
    sin = jnp.concatenate([-jnp.sin(ar), jnp.sin(ar), -jnp.sin(ac), jnp.sin(ac)], axis=-1)
    ident = jnp.concatenate([jnp.ones((tm, ROPE_DIM), F32), jnp.zeros((tm, ROPE_DIM), F32)], axis=-1)
    return jnp.concatenate([ident, jnp.concatenate([cos, sin], axis=-1)], axis=0)


def _na_bias_table(na_rpb):
    cols = np.arange(GRID_W)
    col_start = np.clip(cols - NA_COLS // 2, 0, GRID_W - NA_COLS)
    col_ok = (cols[None, :] >= col_start[:, None]) & (cols[None, :] < col_start[:, None] + NA_COLS)
    d_col = np.clip(cols[None, :] - cols[:, None] + NA_COLS - 1, 0, 2 * NA_COLS - 2)
    pick_col = (d_col[:, :, None] == np.arange(2 * NA_COLS - 1)[None, None, :]).astype(np.float32)
    tab = jnp.einsum("lhrc,qwc->lhrqw", na_rpb, pick_col, precision=lax.Precision.HIGHEST)
    tab = jnp.where(col_ok[None, None, None], tab, NEG_BIG)
    nxt = jnp.concatenate([tab[:, :, 1:], jnp.full_like(tab[:, :, :1], NEG_BIG)], axis=2)
    return jnp.concatenate([tab, nxt], axis=-1)


def kernel(x_prompt, x_sample, cache_na_k, cache_na_v, cache_mla_ckv, cache_mla_krope, c, c_ctx, w_ada, b_ada,
           w_in, g_q, g_kv, w_uq, w_ukv, na_rpb, sgu_ln_g, sgu_ln_b, sgu_w, sgu_b, w_out, ln1_g, ln1_b, ln2_g,
           ln2_b, w_router, b_router, w_gate, w_up, w_down):
    max_tiles = (T_ALL * TOP_K) // TM_EXP + N_EXPERTS

    cond = jnp.zeros((N_COND, D_MODEL), F32).at[0].set(c_ctx).at[1:1 + DEC_BATCH].set(c)
    mods = _ada(cond, w_ada, b_ada).reshape(DEPTH, N_COND, 6, 1, D_MODEL)

    w_main, w_kr, w_z = _prep_w_in(w_in)
    w_uq_p = _prep_w_uq(w_uq)
    w_ukv_p = _prep_w_ukv(w_ukv)
    w_out_b = w_out.astype(BF16)
    w_router_t = w_router.T.astype(BF16)
    b_router_col = b_router.reshape(N_EXPERTS, 1).astype(F32)
    cs_tab = _rope_table(TM_PROJ)
    bias_tab = _na_bias_table(na_rpb)
    cache_k = cache_na_k.reshape(DEC_BATCH, DEPTH, PAST_LEN, NA_W)
    cache_v = cache_na_v.reshape(DEC_BATCH, DEPTH, PAST_LEN, NA_W)
    kmc, vvc = _ctxkv(cache_mla_ckv, cache_mla_krope, w_ukv_p)

    new_k = jnp.zeros((BATCH, DEPTH, SEQ, NA_W), F32)
    new_v = jnp.zeros((BATCH, DEPTH, SEQ, NA_W), F32)
    new_ckv = jnp.zeros((BATCH, DEPTH, SEQ, KV_LORA), F32)
    new_kr = jnp.zeros((BATCH, DEPTH, SEQ, ROPE_DIM), F32)

    x_ctx = x_prompt.reshape(T_CTX, D_MODEL)
    x_lat = x_sample.reshape(T_LAT, D_MODEL)
    for l in range(DEPTH):
        q, k, v, new_k, new_v, qm, km, vv, new_ckv, new_kr, z = _inproj(
            l, x_ctx, x_lat, mods, w_main, w_kr, w_z, g_q, g_kv, w_uq_p, w_ukv_p, cs_tab, new_k, new_v, new_ckv, new_kr)

        oa_c, ob_c = _ctx_attn(q, k, v, qm, km, vv)
        oa_l = _lat_na(l, q, k, v, cache_k, cache_v, bias_tab)
        ob_l = _lat_mla(l, qm, km, vv, kmc, vvc)
        oc = _sgu(l, z, sgu_ln_g, sgu_ln_b, sgu_w, sgu_b)

        x1, hp, e, rank, gate, cnt = _outproj(l, oa_c, oa_l, ob_c, ob_l, oc, w_out_b, x_ctx, x_lat, mods, ln1_g, ln1_b,
                                              w_router_t, b_router_col)
        pos, tile_expert, next_expert, n_tiles, pad_lo, pad_hi = _dispatch_plan(
            e, rank, cnt[:, 0].astype(jnp.int32), max_tiles)
        pos0, pos1 = pos[0], pos[1]
        xs = _dispatch(pad_lo, pad_hi, n_tiles, pos0, pos1, hp, max_tiles)
        ys = _experts(l, tile_expert, next_expert, n_tiles, xs, w_gate, w_up, w_down, max_tiles)
        gate_t = gate.T
        x_ctx = _tail(l, 0, T_CTX, pos0, pos1, x1, gate_t, mods, ln2_g, ln2_b, ys)
        x_lat = _tail(l, T_CTX, T_LAT, pos0, pos1, x1, gate_t, mods, ln2_g, ln2_b, ys)

    return (x_ctx.reshape(BATCH, SEQ, D_MODEL), x_lat.reshape(DEC_BATCH, DEC_SEQ, D_MODEL),
            new_k.reshape(BATCH, DEPTH, SEQ, NA_HEADS, NA_HEAD_DIM), new_v.reshape(BATCH, DEPTH, SEQ, NA_HEADS, NA_HEAD_DIM),
            new_ckv, new_kr)
```
